```python
import math
import jax, jax.numpy as jnp
from jax import lax
import numpy as np


D_MODEL = 1024
BATCH = 8
SEQ = 4096
DEPTH = 1

GRID_W = 64
CTX_LEN = 256
MIX_WIDTH = D_MODEL
HEAD_DIM = 64
ATTN_WIDTH = MIX_WIDTH // 2
N_Q_HEADS = ATTN_WIDTH // HEAD_DIM
N_KV_HEADS = max(1, N_Q_HEADS // 4)
GROUP = N_Q_HEADS // N_KV_HEADS
KV_WIDTH = N_KV_HEADS * HEAD_DIM
HYENA_WIDTH = MIX_WIDTH - ATTN_WIDTH
IN_WIDTH = ATTN_WIDTH + 2 * KV_WIDTH + 3 * HYENA_WIDTH
WINDOW = 128
ATTN_BLOCK = 128
ATTN_SCALE = HEAD_DIM ** -0.5
ROPE_BASE = 10000.0
SHORT_CONV = 3
POS_EMB_DIM = 33
POS_BANDS = (POS_EMB_DIM - 1) // 2
FILTER_FFN = 64
DECAY_TARGET = 1e-2
FAST_DECAY_PCT = 0.3
SLOW_DECAY_PCT = 1.5
MAX_DECAY = -math.log(DECAY_TARGET) / FAST_DECAY_PCT
MIN_DECAY = -math.log(DECAY_TARGET) / SLOW_DECAY_PCT
N_EXPERTS = 32
TOP_K = 4
D_FF = D_MODEL
SWIGLU_LIMIT = 7.0
SWIGLU_ALPHA = 1.702
EXPERT_BLOCK = 128
EPS = 1e-6
NEG_INF = -1e30

kernel_name = 'hymba_hyena_swa_moe_dit_layer'


def rmsnorm(x, g):
    xf = x.astype(jnp.float32)
    y = xf * lax.rsqrt(jnp.mean(xf * xf, axis=-1, keepdims=True) + EPS)
    return y.astype(x.dtype) * g


def modulate(h, shift, scale):
    return h * (1 + scale) + shift


def adaln(cond, w, b):
    return jax.nn.silu(cond) @ w + b


def split_projection(p):
    return jnp.split(p, [ATTN_WIDTH, ATTN_WIDTH + KV_WIDTH, ATTN_WIDTH + 2 * KV_WIDTH], axis=-1)


def axial_rope(x):
    L = x.shape[1]
    rows = L // GRID_W
    row = jnp.repeat(jnp.arange(rows, dtype=jnp.float32), GRID_W)
    col = jnp.tile(jnp.arange(GRID_W, dtype=jnp.float32), rows)
    half = HEAD_DIM // 2
    n_freq = half // 2
    inv_freq = ROPE_BASE ** (-jnp.arange(n_freq, dtype=jnp.float32) / n_freq)

    def rotate(xa, pos):
        ang = pos[:, None] * inv_freq[None, :]
        cos = jnp.cos(ang)[None, :, None, :].astype(x.dtype)
        sin = jnp.sin(ang)[None, :, None, :].astype(x.dtype)
        x1, x2 = xa[..., :n_freq], xa[..., n_freq:]
        return jnp.concatenate([x1 * cos - x2 * sin, x1 * sin + x2 * cos], axis=-1)

    return jnp.concatenate([rotate(x[..., :half], row), rotate(x[..., half:], col)], axis=-1)


def windowed_attention(q, k, v, kc, vc, sink):
    B, L = q.shape[:2]
    Lc = kc.shape[1]
    nb = L // ATTN_BLOCK
    nk = 3 * ATTN_BLOCK
    qb = q.reshape(B, nb, ATTN_BLOCK, N_KV_HEADS, GROUP, HEAD_DIM)

    def band(t):
        tp = jnp.pad(t, ((0, 0), (ATTN_BLOCK, ATTN_BLOCK), (0, 0), (0, 0)))
        tp = tp.reshape(B, nb + 2, ATTN_BLOCK, N_KV_HEADS, HEAD_DIM)
        return jnp.concatenate([tp[:, :-2], tp[:, 1:-1], tp[:, 2:]], axis=2)

    kw, vw = band(k), band(v)
    s_loc = jnp.einsum('bnqhgd,bnkhd->bnhgqk', qb, kw).astype(jnp.float32) * ATTN_SCALE
    i = jnp.arange(ATTN_BLOCK)[:, None]
    j = jnp.arange(nk)[None, :]
    key_pos = (jnp.arange(nb)[:, None, None] - 1) * ATTN_BLOCK + j[None]
    valid = (jnp.abs(j - ATTN_BLOCK - i) <= WINDOW)[None] & (key_pos >= 0) & (key_pos < L)
    s_loc = jnp.where(valid[None, :, None, None], s_loc, NEG_INF)
    s_ctx = jnp.einsum('bnqhgd,bchd->bnhgqc', qb, kc).astype(jnp.float32) * ATTN_SCALE
    s_sink = jnp.broadcast_to(sink.astype(jnp.float32).reshape(1, 1, N_KV_HEADS, GROUP, 1, 1),
                              s_loc.shape[:-1] + (1,))
    p = jax.nn.softmax(jnp.concatenate([s_loc, s_ctx, s_sink], axis=-1), axis=-1)
    p_loc = p[..., :nk].astype(v.dtype)
    p_ctx = p[..., nk:nk + Lc].astype(v.dtype)
    out = (jnp.einsum('bnhgqk,bnkhd->bnqhgd', p_loc, vw)
           + jnp.einsum('bnhgqc,bchd->bnqhgd', p_ctx, vc))
    return out.reshape(B, L, ATTN_WIDTH)


def context_attention(qc, kc, vc, sink):
    B, Lc = qc.shape[:2]
    qg = qc.reshape(B, Lc, N_KV_HEADS, GROUP, HEAD_DIM)
    s = jnp.einsum('bqhgd,bkhd->bhgqk', qg, kc).astype(jnp.float32) * ATTN_SCALE
    s_sink = jnp.broadcast_to(sink.astype(jnp.float32).reshape(1, N_KV_HEADS, GROUP, 1, 1),
                              s.shape[:-1] + (1,))
    p = jax.nn.softmax(jnp.concatenate([s, s_sink], axis=-1), axis=-1)[..., :Lc].astype(vc.dtype)
    return jnp.einsum('bhgqk,bkhd->bqhgd', p, vc).reshape(B, Lc, ATTN_WIDTH)


def short_conv(u, w, b):
    L = u.shape[1]
    pad = SHORT_CONV // 2
    up = jnp.pad(u, ((0, 0), (pad, pad), (0, 0)))
    y = b
    for tap in range(SHORT_CONV):
        y = y + up[:, tap:tap + L] * w[tap]
    return y


def implicit_filter(L, w1, b1, w2, b2, w3, b3, w4, freq):
    f32 = jnp.float32
    t = jnp.linspace(0.0, 1.0, L, dtype=f32)[:, None]
    w = 2.0 * math.pi * jnp.arange(L, dtype=f32)[:, None] / L
    bands = jnp.linspace(1e-4, POS_BANDS - 1, POS_BANDS, dtype=f32)[None, :]
    z = jnp.concatenate([t, jnp.cos(bands * w), -jnp.sin(bands * w)], axis=-1)
    fr = freq.astype(f32)
    h = jnp.sin(fr * (z @ w1.astype(f32) + b1.astype(f32)))
    h = jnp.sin(fr * (h @ w2.astype(f32) + b2.astype(f32)))
    h = jnp.sin(fr * (h @ w3.astype(f32) + b3.astype(f32)))
    k = h @ w4.astype(f32)
    deltas = jnp.linspace(MIN_DECAY, MAX_DECAY, HYENA_WIDTH, dtype=f32)
    decay = jnp.exp(-t * deltas[None, :])
    k_fwd = k[:, :HYENA_WIDTH] * decay
    k_bwd = k[:, HYENA_WIDTH:] * decay
    k_circ = jnp.concatenate([k_fwd, jnp.zeros((1, HYENA_WIDTH), f32), k_bwd[:0:-1]], axis=0)
    return k_circ * lax.rsqrt(jnp.sum(k_circ * k_circ, axis=0, keepdims=True) + EPS)


def bidir_fftconv(u, k_circ, bias):
    L = u.shape[1]
    n = 2 * L
    uf32 = u.astype(jnp.float32)
    uf = jnp.fft.rfft(uf32, n=n, axis=1)
    kf = jnp.fft.rfft(k_circ, n=n, axis=0)
    y = jnp.fft.irfft(uf * kf[None], n=n, axis=1)[:, :L]
    return (y + uf32 * bias.astype(jnp.float32)).astype(u.dtype)


def hyena_mixer(u, conv_w, conv_b, filt, filt_bias):
    L = u.shape[1]
    x0, x1, v = jnp.split(short_conv(u, conv_w, conv_b), 3, axis=-1)
    k_circ = implicit_filter(L, *filt)
    return x0 * bidir_fftconv(x1 * v, k_circ, filt_bias)


def merge_groups(a, y, g_a, g_y, w_out):
    return jnp.concatenate([rmsnorm(a, g_a), rmsnorm(y, g_y)], axis=-1) @ w_out


def moe(h, w_router, b_router, w_gu, b_gu, w_down, b_down):
    B, L, D = h.shape
    T = B * L
    ht = h.reshape(T, D)
    logits = (ht @ w_router + b_router).astype(jnp.float32)
    top_val, top_idx = lax.top_k(logits, TOP_K)
    gates = jax.nn.softmax(top_val, axis=-1)
    A = T * TOP_K
    flat_e = top_idx.reshape(A)
    flat_tok = jnp.arange(A, dtype=jnp.int32) // TOP_K
    flat_g = gates.reshape(A)
    order = jnp.argsort(flat_e)
    se = flat_e[order]
    counts = jnp.bincount(flat_e, length=N_EXPERTS)
    starts = jnp.cumsum(counts) - counts
    pcounts = (counts + EXPERT_BLOCK - 1) // EXPERT_BLOCK * EXPERT_BLOCK
    pends = jnp.cumsum(pcounts)
    pstarts = pends - pcounts
    dest = pstarts[se] + (jnp.arange(A, dtype=jnp.int32) - starts[se])
    n_blocks = (A + EXPERT_BLOCK - 1) // EXPERT_BLOCK + N_EXPERTS
    P = n_blocks * EXPERT_BLOCK
    slot_tok = jnp.full((P,), T, dtype=jnp.int32).at[dest].set(flat_tok[order])
    slot_g = jnp.zeros((P,), jnp.float32).at[dest].set(flat_g[order])
    block_e = jnp.minimum(jnp.searchsorted(pends, jnp.arange(n_blocks) * EXPERT_BLOCK, side='right'),
                          N_EXPERTS - 1)
    xs = jnp.concatenate([ht, jnp.zeros((1, D), ht.dtype)], axis=0)[slot_tok]
    xs = xs.reshape(n_blocks, EXPERT_BLOCK, D)

    def expert_block(args):
        xb, e = args
        gu = xb @ w_gu[e] + b_gu[e]
        gate, up = gu[:, :D_FF], gu[:, D_FF:]
        gate = jnp.minimum(gate, SWIGLU_LIMIT)
        up = jnp.clip(up, -SWIGLU_LIMIT, SWIGLU_LIMIT)
        glu = gate * jax.nn.sigmoid(SWIGLU_ALPHA * gate)
        return ((up + 1) * glu) @ w_down[e] + b_down[e]

    ys = lax.map(expert_block, (xs, block_e)).reshape(P, D)
    out = jax.ops.segment_sum(ys * slot_g[:, None].astype(ys.dtype), slot_tok, num_segments=T + 1)[:T]
    return out.reshape(B, L, D)


def setup_inputs(seed: int = 0) -> dict:
    key = jax.random.key(seed)
    ks = jax.random.split(key, 40)

    def normal(idx, shape, scale=1.0):
        return scale * jax.random.normal(ks[idx], shape, jnp.float32)

    def gain(idx, shape):
        return 1.0 + normal(idx, shape, 0.02)

    return {
        'x': normal(0, (BATCH, SEQ, D_MODEL)),
        'c': normal(1, (BATCH, D_MODEL)),
        'ctx': normal(2, (BATCH, CTX_LEN, D_MODEL)),
        'c_ctx': normal(3, (D_MODEL,)),
        'w_mod': normal(4, (DEPTH, D_MODEL, 6 * D_MODEL), D_MODEL ** -0.5),
        'b_mod': normal(5, (DEPTH, 6 * D_MODEL), 0.01),
        'norm1': gain(6, (DEPTH, D_MODEL)),
        'norm2': gain(7, (DEPTH, D_MODEL)),
        'w_in': normal(8, (DEPTH, D_MODEL, IN_WIDTH), D_MODEL ** -0.5),
        'q_norm': gain(9, (DEPTH, HEAD_DIM)),
        'k_norm': gain(10, (DEPTH, HEAD_DIM)),
        'sink': normal(11, (DEPTH, N_Q_HEADS), 0.5),
        'conv_w': normal(12, (DEPTH, SHORT_CONV, 3 * HYENA_WIDTH), SHORT_CONV ** -0.5),
        'conv_b': normal(13, (DEPTH, 3 * HYENA_WIDTH), 0.01),
        'filt_w1': normal(14, (DEPTH, POS_EMB_DIM, FILTER_FFN), POS_EMB_DIM ** -0.5),
        'filt_b1': normal(15, (DEPTH, FILTER_FFN), 0.1),
        'filt_w2': normal(16, (DEPTH, FILTER_FFN, FILTER_FFN), FILTER_FFN ** -0.5),
        'filt_b2': normal(17, (DEPTH, FILTER_FFN), 0.1),
        'filt_w3': normal(18, (DEPTH, FILTER_FFN, FILTER_FFN), FILTER_FFN ** -0.5),
        'filt_b3': normal(19, (DEPTH, FILTER_FFN), 0.1),
        'filt_w4': normal(20, (DEPTH, FILTER_FFN, 2 * HYENA_WIDTH), FILTER_FFN ** -0.5),
        'filt_freq': gain(21, (DEPTH, FILTER_FFN)),
        'filt_bias': normal(22, (DEPTH, HYENA_WIDTH), 0.1),
        'attn_out_norm': gain(23, (DEPTH, ATTN_WIDTH)),
        'hyena_out_norm': gain(24, (DEPTH, HYENA_WIDTH)),
        'w_out': normal(25, (DEPTH, MIX_WIDTH, D_MODEL), MIX_WIDTH ** -0.5),
        'w_router': normal(26, (DEPTH, D_MODEL, N_EXPERTS), D_MODEL ** -0.5),
        'b_router': normal(27, (DEPTH, N_EXPERTS), 0.01),
        'w_gu': normal(28, (DEPTH, N_EXPERTS, D_MODEL, 2 * D_FF), D_MODEL ** -0.5),
        'b_gu': normal(29, (DEPTH, N_EXPERTS, 2 * D_FF), 0.01),
        'w_down': normal(30, (DEPTH, N_EXPERTS, D_FF, D_MODEL), D_FF ** -0.5),
        'b_down': normal(31, (DEPTH, N_EXPERTS, D_MODEL), 0.01),
    }


def reference(x, c, ctx, c_ctx, w_mod, b_mod, norm1, norm2, w_in, q_norm, k_norm, sink,
              conv_w, conv_b, filt_w1, filt_b1, filt_w2, filt_b2, filt_w3, filt_b3, filt_w4,
              filt_freq, filt_bias, attn_out_norm, hyena_out_norm, w_out,
              w_router, b_router, w_gu, b_gu, w_down, b_down):
    B, L, _ = x.shape
    Lc = ctx.shape[1]
    for l in range(DEPTH):
        last = l == DEPTH - 1
        mod = adaln(c, w_mod[l], b_mod[l])[:, None, :]
        mod_c = adaln(c_ctx[None], w_mod[l], b_mod[l])[:, None, :]
        sh1, sc1, g1, sh2, sc2, g2 = jnp.split(mod, 6, axis=-1)
        csh1, csc1, cg1, csh2, csc2, cg2 = jnp.split(mod_c, 6, axis=-1)
        filt = (filt_w1[l], filt_b1[l], filt_w2[l], filt_b2[l], filt_w3[l], filt_b3[l],
                filt_w4[l], filt_freq[l])
        experts = (w_router[l], b_router[l], w_gu[l], b_gu[l], w_down[l], b_down[l])

        h = modulate(rmsnorm(x, norm1[l]), sh1, sc1)
        q, k, v, u = split_projection(h @ w_in[l])
        q = axial_rope(rmsnorm(q.reshape(B, L, N_Q_HEADS, HEAD_DIM), q_norm[l]))
        k = axial_rope(rmsnorm(k.reshape(B, L, N_KV_HEADS, HEAD_DIM), k_norm[l]))
        v = v.reshape(B, L, N_KV_HEADS, HEAD_DIM)

        hc = modulate(rmsnorm(ctx, norm1[l]), csh1, csc1)
        if last:
            kc, vc = jnp.split(hc @ w_in[l][:, ATTN_WIDTH:ATTN_WIDTH + 2 * KV_WIDTH], 2, axis=-1)
        else:
            qc, kc, vc, uc = split_projection(hc @ w_in[l])
        kc = rmsnorm(kc.reshape(B, Lc, N_KV_HEADS, HEAD_DIM), k_norm[l])
        vc = vc.reshape(B, Lc, N_KV_HEADS, HEAD_DIM)

        a = windowed_attention(q, k, v, kc, vc, sink[l])
        y = hyena_mixer(u, conv_w[l], conv_b[l], filt, filt_bias[l])
        x_new = x + g1 * merge_groups(a, y, attn_out_norm[l], hyena_out_norm[l], w_out[l])
        x_new = x_new + g2 * moe(modulate(rmsnorm(x_new, norm2[l]), sh2, sc2), *experts)

        if not last:
            qc = rmsnorm(qc.reshape(B, Lc, N_Q_HEADS, HEAD_DIM), q_norm[l])
            ac = context_attention(qc, kc, vc, sink[l])
            yc = hyena_mixer(uc, conv_w[l], conv_b[l], filt, filt_bias[l])
            ctx = ctx + cg1 * merge_groups(ac, yc, attn_out_norm[l], hyena_out_norm[l], w_out[l])
            ctx = ctx + cg2 * moe(modulate(rmsnorm(ctx, norm2[l]), csh2, csc2), *experts)
        x = x_new
    return x
```

```python
import functools
import math

import numpy as np
import jax
import jax.numpy as jnp
from jax import lax
from jax.experimental import pallas as pl
from jax.experimental.pallas import tpu as pltpu

F32 = jnp.float32
BF16 = jnp.bfloat16

LANES = 128
SUBLANES = 8
VMEM_LIMIT = 56 * 1024 * 1024

HEAD_DIM = 64
N_Q_HEADS = 8
N_KV_HEADS = 2
GROUP = N_Q_HEADS // N_KV_HEADS
GRID_W = 64
WINDOW = 128
ROPE_BASE = 10000.0
ATTN_SCALE = HEAD_DIM ** -0.5
POS_EMB_DIM = 33
POS_BANDS = (POS_EMB_DIM - 1) // 2
DECAY_TARGET = 1e-2
MAX_DECAY = -math.log(DECAY_TARGET) / 0.3
MIN_DECAY = -math.log(DECAY_TARGET) / 1.5
N_EXPERTS = 32
TOP_K = 4
SWIGLU_LIMIT = 7.0
SWIGLU_ALPHA = 1.702
EPS = 1e-6
NEG_INF = -1e30

EXPERT_ROWS = 256
COMBINE_ROWS = 256


def _cparams(*sem):
    return pltpu.CompilerParams(dimension_semantics=sem, vmem_limit_bytes=VMEM_LIMIT)


def _split(a):
    hi = a.astype(BF16)
    lo = (a - hi.astype(F32)).astype(BF16)
    return hi, lo


def _dot(a, b):
    return jnp.dot(a, b, preferred_element_type=F32)


def _dot3(a, b):
    ah, al = _split(a)
    bh, bl = _split(b)
    return _dot(ah, bh) + _dot(al, bh) + _dot(ah, bl)


def _rms(x):
    return lax.rsqrt(jnp.mean(x * x, axis=-1, keepdims=True) + EPS)


def _adaln_kernel(c_ref, w_ref, b_ref, o_ref):
    c = c_ref[...]
    o_ref[...] = _dot3(c * jax.nn.sigmoid(c), w_ref[...]) + b_ref[...]


def _adaln(c_all, w_mod, b_mod):
    rows, d = c_all.shape
    n = w_mod.shape[1]
    tn = 1024
    return pl.pallas_call(
        _adaln_kernel,
        grid=(n // tn,),
        in_specs=[pl.BlockSpec((rows, d), lambda j: (0, 0)),
                  pl.BlockSpec((d, tn), lambda j: (0, j)),
                  pl.BlockSpec((1, tn), lambda j: (0, j))],
        out_specs=pl.BlockSpec((rows, tn), lambda j: (0, j)),
        out_shape=jax.ShapeDtypeStruct((rows, n), F32),
        compiler_params=_cparams("arbitrary"),
        name="adaln",
    )(c_all, w_mod, b_mod[None])


def _head_rms(x, bd):
    hi, lo = _split(x * x)
    return x * lax.rsqrt(_dot(hi, bd) + _dot(lo, bd) + EPS)


def _rope128(x, cos, sin):
    lane = lax.broadcasted_iota(jnp.int32, x.shape, 1)
    partner = jnp.where(lane % 32 < 16, pltpu.roll(x, LANES - 16, 1), pltpu.roll(x, 16, 1))
    return x * cos + partner * sin


def _modulated(x, mod_ref, row, norm_ref, d):
    sh = mod_ref[pl.ds(row, 1), 0:d]
    sc = mod_ref[pl.ds(row, 1), d:2 * d]
    return (x * _rms(x)) * norm_ref[...] * (1 + sc) + sh


def _inproj_kernel(x_ref, mod_ref, n1_ref, w_ref, gq_ref, gk_ref, bd_ref, cos_ref, sin_ref,
                   q_ref, k_ref, v_ref, u_ref):
    d = x_ref.shape[-1]
    aw = q_ref.shape[-1]
    kw = k_ref.shape[-1]
    hb = _modulated(x_ref[...], mod_ref, pl.program_id(1), n1_ref, d).astype(BF16)
    cos = cos_ref[...]
    sin = sin_ref[...]
    bd = bd_ref[...]
    q = _head_rms(_dot(hb, w_ref[:, 0:aw]), bd) * gq_ref[...]
    for c in range(aw // LANES):
        sl = slice(c * LANES, (c + 1) * LANES)
        q_ref[:, sl] = (_rope128(q[:, sl], cos, sin) * ATTN_SCALE).astype(BF16)
    k = _head_rms(_dot(hb, w_ref[:, aw:aw + kw]), bd[0:kw, 0:kw]) * gk_ref[...]
    for c in range(kw // LANES):
        sl = slice(c * LANES, (c + 1) * LANES)
        k_ref[:, sl] = _rope128(k[:, sl], cos, sin).astype(BF16)
    v_ref[...] = _dot(hb, w_ref[:, aw + kw:aw + 2 * kw]).astype(BF16)
    u_ref[...] = _dot(hb, w_ref[:, aw + 2 * kw:])


def _inproj(x, mod, norm1, w_all, gq, gk, bd, cos_t, sin_t, tl):
    b, l, d = x.shape
    aw, kw = gq.shape[1], gk.shape[1]
    uw = w_all.shape[1] - aw - 2 * kw
    const = lambda i, j: (0, 0)
    tok = lambda i, j: (j, i, 0)
    return pl.pallas_call(
        _inproj_kernel,
        grid=(l // tl, b),
        in_specs=[pl.BlockSpec((None, tl, d), tok),
                  pl.BlockSpec(mod.shape, const),
                  pl.BlockSpec((1, d), const),
                  pl.BlockSpec(w_all.shape, const),
                  pl.BlockSpec((1, aw), const),
                  pl.BlockSpec((1, kw), const),
                  pl.BlockSpec(bd.shape, const),
                  pl.BlockSpec((tl, LANES), lambda i, j: (i, 0)),
                  pl.BlockSpec((tl, LANES), lambda i, j: (i, 0))],
        out_specs=[pl.BlockSpec((None, tl, aw), tok),
                   pl.BlockSpec((None, tl, kw), tok),
                   pl.BlockSpec((None, tl, kw), tok),
                   pl.BlockSpec((None, tl, uw), tok)],
        out_shape=[jax.ShapeDtypeStruct((b, l, aw), BF16),
                   jax.ShapeDtypeStruct((b, l, kw), BF16),
                   jax.ShapeDtypeStruct((b, l, kw), BF16),
                   jax.ShapeDtypeStruct((b, l, uw), F32)],
        compiler_params=_cparams("arbitrary", "arbitrary"),
        name="inproj",
    )(x, mod, norm1, w_all, gq, gk, bd, cos_t, sin_t)


def _ctxkv_kernel(row, x_ref, mod_ref, n1_ref, w_ref, gk_ref, bd_ref, k_ref, v_ref):
    d = x_ref.shape[-1]
    kw = k_ref.shape[-1]
    hb = _modulated(x_ref[...], mod_ref, row, n1_ref, d).astype(BF16)
    k = _head_rms(_dot(hb, w_ref[:, 0:kw]), bd_ref[...]) * gk_ref[...]
    k_ref[...] = k.astype(BF16)
    v_ref[...] = _dot(hb, w_ref[:, kw:]).astype(BF16)


def _ctxkv(ctx, mod, ctx_row, norm1, w_kv, gk, bd):
    b, lc, d = ctx.shape
    kw = gk.shape[1]
    const = lambda i: (0, 0)
    tok = lambda i: (i, 0, 0)
    return pl.pallas_call(
        functools.partial(_ctxkv_kernel, ctx_row),
        grid=(b,),
        in_specs=[pl.BlockSpec((None, lc, d), tok),
                  pl.BlockSpec(mod.shape, const),
                  pl.BlockSpec((1, d), const),
                  pl.BlockSpec(w_kv.shape, const),
                  pl.BlockSpec((1, kw), const),
                  pl.BlockSpec(bd.shape, const)],
        out_specs=[pl.BlockSpec((None, lc, kw), tok), pl.BlockSpec((None, lc, kw), tok)],
        out_shape=[jax.ShapeDtypeStruct((b, lc, kw), BF16)] * 2,
        compiler_params=_cparams("arbitrary"),
        name="ctxkv",
    )(ctx, mod, norm1, w_kv, gk, bd)


def _attn_kernel(sink_ref, q_ref, kp_ref, kc_ref, kn_ref, vp_ref, vc_ref, vn_ref, kx_ref, vx_ref,
                 ga_ref, o_ref, acc_ref):
    i = pl.program_id(1)
    nb = pl.num_programs(1)
    tq = q_ref.shape[0]
    lc = kx_ref.shape[0]
    nk = 3 * tq + lc
    r = lax.broadcasted_iota(jnp.int32, (tq, nk), 0)
    j = lax.broadcasted_iota(jnp.int32, (tq, nk), 1)
    first = jnp.where(i > 0, 0, tq)
    last = jnp.where(i < nb - 1, 0, tq)
    in_next = (j >= 2 * tq) & (j < 3 * tq)
    valid = ((j >= r + first) & (j < tq)) | ((j >= tq) & ~in_next) | (in_next & (j - 2 * tq <= r - last))
    lo = lax.broadcasted_iota(jnp.int32, (nk, LANES), 1) < HEAD_DIM
    zero = jnp.zeros((nk, LANES), BF16)
    for h in range(N_KV_HEADS):
        hs = slice(h * LANES, (h + 1) * LANES)
        kcat = jnp.concatenate([kp_ref[:, hs], kc_ref[:, hs], kn_ref[:, hs], kx_ref[:, hs]], axis=0)
        vcat = jnp.concatenate([vp_ref[:, hs], vc_ref[:, hs], vn_ref[:, hs], vx_ref[:, hs]], axis=0)
        halves = ((jnp.where(lo, kcat, zero), jnp.where(lo, vcat, zero)),
                  (jnp.where(lo, zero, kcat), jnp.where(lo, zero, vcat)))
        for p in range(GROUP // 2):
            cs = slice((h * (GROUP // 2) + p) * LANES, (h * (GROUP // 2) + p + 1) * LANES)
            q2 = q_ref[:, cs]
            out = jnp.zeros((tq, LANES), F32)
            for half, (kk, vv) in enumerate(halves):
                sk = sink_ref[h * GROUP + 2 * p + half]
                s = lax.dot_general(q2, kk, (((1,), (1,)), ((), ())), preferred_element_type=F32)
                s = jnp.where(valid, s, NEG_INF)
                m = jnp.maximum(jnp.max(s, axis=-1, keepdims=True), sk)
                e = jnp.exp(s - m)
                den = jnp.sum(e, axis=-1, keepdims=True) + jnp.exp(sk - m)
                out = out + _dot(e.astype(BF16), vv) / den
            acc_ref[:, cs] = out
    a = acc_ref[...]
    o_ref[...] = (a * _rms(a) * ga_ref[...]).astype(BF16)


def _attention(sink, q, k, v, kx, vx, ga):
    b, l, aw = q.shape
    kw = k.shape[-1]
    lc = kx.shape[1]
    tq = WINDOW
    nb = l // tq
    cur = lambda bi, i: (bi, i, 0)
    prev = lambda bi, i: (bi, jnp.maximum(i - 1, 0), 0)
    nxt = lambda bi, i: (bi, jnp.minimum(i + 1, nb - 1), 0)
    ctx = lambda bi, i: (bi, 0, 0)
    kv = lambda m: pl.BlockSpec((None, tq, kw), m)
    return pl.pallas_call(
        _attn_kernel,
        grid=(b, nb),
        in_specs=[pl.BlockSpec(memory_space=pltpu.SMEM),
                  pl.BlockSpec((None, tq, aw), cur),
                  kv(prev), kv(cur), kv(nxt), kv(prev), kv(cur), kv(nxt),
                  pl.BlockSpec((None, lc, kw), ctx), pl.BlockSpec((None, lc, kw), ctx),
                  pl.BlockSpec((1, aw), lambda bi, i: (0, 0))],
        out_specs=pl.BlockSpec((None, tq, aw), cur),
        out_shape=jax.ShapeDtypeStruct((b, l, aw), BF16),
        scratch_shapes=[pltpu.VMEM((tq, aw), F32)],
        compiler_params=_cparams("arbitrary", "arbitrary"),
        name="attn",
    )(sink, q, k, k, k, v, v, v, kx, vx, ga)


def _hconv_kernel(u0_ref, u1_ref, u2_ref, w0_ref, w1_ref, w2_ref, b0_ref, b1_ref, b2_ref,
                  x0_ref, z_ref, zn_ref):
    l, cb = u0_ref.shape
    row = lax.broadcasted_iota(jnp.int32, (l, cb), 0)

    def conv(u_ref, w_ref, b_ref):
        u = u_ref[...]
        before = jnp.where(row == 0, 0.0, pltpu.roll(u, 1, 0))
        after = jnp.where(row == l - 1, 0.0, pltpu.roll(u, l - 1, 0))
        return b_ref[...] + before * w_ref[0:1, :] + u * w_ref[1:2, :] + after * w_ref[2:3, :]

    x0_ref[...] = conv(u0_ref, w0_ref, b0_ref)
    z = conv(u1_ref, w1_ref, b1_ref) * conv(u2_ref, w2_ref, b2_ref)
    z_ref[...] = z.astype(BF16)
    sign = (1 - 2 * (row % 2)).astype(F32)
    zn_ref[...] = jnp.sum(z * sign, axis=0, keepdims=True)


def _hconv(u, conv_w, conv_b, cb=LANES):
    b, l, w3 = u.shape
    c = w3 // 3
    n = c // cb
    us = [pl.BlockSpec((None, l, cb), lambda bi, j, g=g: (bi, 0, g * n + j)) for g in range(3)]
    ws = [pl.BlockSpec((3, cb), lambda bi, j, g=g: (0, g * n + j)) for g in range(3)]
    bs = [pl.BlockSpec((1, cb), lambda bi, j, g=g: (0, g * n + j)) for g in range(3)]
    out = lambda bi, j: (bi, 0, j)
    return pl.pallas_call(
        _hconv_kernel,
        grid=(b, n),
        in_specs=us + ws + bs,
        out_specs=[pl.BlockSpec((None, l, cb), out), pl.BlockSpec((None, l, cb), out),
                   pl.BlockSpec((None, 1, cb), out)],
        out_shape=[jax.ShapeDtypeStruct((b, l, c), F32), jax.ShapeDtypeStruct((b, l, c), BF16),
                   jax.ShapeDtypeStruct((b, 1, c), F32)],
        compiler_params=_cparams("arbitrary", "arbitrary"),
        name="hconv",
    )(u, u, u, conv_w, conv_w, conv_w, conv_b, conv_b, conv_b)


def _filter_kernel(z_ref, w1_ref, b1_ref, w2_ref, b2_ref, w3_ref, b3_ref, fr_ref, w4f_ref, w4b_ref,
                   dl_ref, kp_ref, km_ref, kn_ref, h_scr):
    l, cf = kp_ref.shape

    @pl.when(pl.program_id(0) == 0)
    def _():
        fr = fr_ref[...]
        h = jnp.sin(fr * (_dot3(z_ref[...], w1_ref[...]) + b1_ref[...]))
        h = jnp.sin(fr * (_dot3(h, w2_ref[...]) + b2_ref[...]))
        h_scr[...] = jnp.sin(fr * (_dot3(h, w3_ref[...]) + b3_ref[...]))

    h = h_scr[...]
    row = lax.broadcasted_iota(jnp.int32, (l, cf), 0)
    t = row.astype(F32) / (l - 1)
    decay = jnp.exp(-t * dl_ref[...])
    kf = _dot3(h, w4f_ref[...]) * decay
    kb = jnp.where(row == 0, 0.0, _dot3(h, w4b_ref[...]) * decay)
    nrm = lax.rsqrt(jnp.sum(kf * kf + kb * kb, axis=0, keepdims=True) + EPS)
    kp = (kf + kb) * nrm
    kp_ref[...] = kp.astype(BF16)
    km_ref[...] = ((kf - kb) * nrm).astype(BF16)
    sign = (1 - 2 * (row % 2)).astype(F32)
    kn_ref[...] = jnp.sum(kp * sign, axis=0, keepdims=True)


def _filter(zf, w1, b1, w2, b2, w3, b3, fr, w4, deltas, cf=LANES):
    l, zw = zf.shape
    ffn = w2.shape[0]
    c = w4.shape[1] // 2
    n = c // cf
    const = lambda j: (0, 0)
    vec = pl.BlockSpec((1, ffn), const)
    mat = pl.BlockSpec((ffn, ffn), const)
    return pl.pallas_call(
        _filter_kernel,
        grid=(n,),
        in_specs=[pl.BlockSpec((l, zw), const), pl.BlockSpec((zw, ffn), const), vec, mat, vec, mat, vec, vec,
                  pl.BlockSpec((ffn, cf), lambda j: (0, j)),
                  pl.BlockSpec((ffn, cf), lambda j: (0, n + j)),
                  pl.BlockSpec((1, cf), lambda j: (0, j))],
        out_specs=[pl.BlockSpec((l, cf), lambda j: (0, j)), pl.BlockSpec((l, cf), lambda j: (0, j)),
                   pl.BlockSpec((1, cf), lambda j: (0, j))],
        out_shape=[jax.ShapeDtypeStruct((l, c), BF16), jax.ShapeDtypeStruct((l, c), BF16),
                   jax.ShapeDtypeStruct((1, c), F32)],
        scratch_shapes=[pltpu.VMEM((l, ffn), F32)],
        compiler_params=_cparams("arbitrary"),
        name="filt",
    )(zf, w1, b1, w2, b2, w3, b3, fr, w4, w4, deltas)


def _kspec_kernel(n_fft, c_ref, s_ref, kp_ref, km_ref, kr_ref, ks_ref):
    tf = c_ref.shape[0]
    f = pl.program_id(0) * tf + lax.broadcasted_iota(jnp.int32, (tf, 1), 0)
    w = jnp.where(f == 0, 1.0 / n_fft, 2.0 / n_fft)
    kr_ref[...] = _dot(c_ref[...], kp_ref[...]) * w
    ks_ref[...] = _dot(s_ref[...], km_ref[...]) * w


def _kspec(cm, sm, kp, km, tf):
    l, c = kp.shape
    const = lambda i: (0, 0)
    tile = lambda i: (i, 0)
    return pl.pallas_call(
        functools.partial(_kspec_kernel, 2 * l),
        grid=(l // tf,),
        in_specs=[pl.BlockSpec((tf, l), tile), pl.BlockSpec((tf, l), tile),
                  pl.BlockSpec((l, c), const), pl.BlockSpec((l, c), const)],
        out_specs=[pl.BlockSpec((tf, c), tile), pl.BlockSpec((tf, c), tile)],
        out_shape=[jax.ShapeDtypeStruct((l, c), F32)] * 2,
        compiler_params=_cparams("arbitrary"),
        name="kspec",
    )(cm, sm, kp, km)


def _hfwd_kernel(c_ref, s_ref, z_ref, kr_ref, ks_ref, a_ref, b_ref):
    z = z_ref[...]
    zr = _dot(c_ref[...], z)
    zs = _dot(s_ref[...], z)
    kr = kr_ref[...]
    ks = ks_ref[...]
    a_ref[...] = (zr * kr - zs * ks).astype(BF16)
    b_ref[...] = (zr * ks + zs * kr).astype(BF16)


def _hfwd(cm, sm, z, kr, ks, tf):
    b, l, c = z.shape
    tile = lambda i, bi: (i, 0)
    out = lambda i, bi: (bi, i, 0)
    return pl.pallas_call(
        _hfwd_kernel,
        grid=(l // tf, b),
        in_specs=[pl.BlockSpec((tf, l), tile), pl.BlockSpec((tf, l), tile),
                  pl.BlockSpec((None, l, c), lambda i, bi: (bi, 0, 0)),
                  pl.BlockSpec((tf, c), tile), pl.BlockSpec((tf, c), tile)],
        out_specs=[pl.BlockSpec((None, tf, c), out), pl.BlockSpec((None, tf, c), out)],
        out_shape=[jax.ShapeDtypeStruct((b, l, c), BF16)] * 2,
        compiler_params=_cparams("arbitrary", "arbitrary"),
        name="hfwd",
    )(cm, sm, z, kr, ks)


def _hinv_kernel(c_ref, s_ref, a_ref, b_ref, x0_ref, z_ref, zn_ref, kn_ref, bias_ref, g_ref, o_ref):
    tt = c_ref.shape[0]
    n_fft = 2 * c_ref.shape[1]
    y = _dot(c_ref[...], a_ref[...]) + _dot(s_ref[...], b_ref[...])
    t = pl.program_id(0) * tt + lax.broadcasted_iota(jnp.int32, (tt, 1), 0)
    sign = (1 - 2 * (t % 2)).astype(F32)
    y = y + sign * (zn_ref[...] * kn_ref[...] * (1.0 / n_fft))
    hy = x0_ref[...] * (y + z_ref[...].astype(F32) * bias_ref[...])
    o_ref[...] = (hy * _rms(hy) * g_ref[...]).astype(BF16)


def _hinv(cm, sm, a, bm, x0, z, zn, kn, bias, gain, tt):
    b, l, c = z.shape
    tile = lambda i, bi: (i, 0)
    full = lambda i, bi: (bi, 0, 0)
    tok = lambda i, bi: (bi, i, 0)
    const = lambda i, bi: (0, 0)
    return pl.pallas_call(
        _hinv_kernel,
        grid=(l // tt, b),
        in_specs=[pl.BlockSpec((tt, l), tile), pl.BlockSpec((tt, l), tile),
                  pl.BlockSpec((None, l, c), full), pl.BlockSpec((None, l, c), full),
                  pl.BlockSpec((None, tt, c), tok), pl.BlockSpec((None, tt, c), tok),
                  pl.BlockSpec((None, 1, c), full),
                  pl.BlockSpec((1, c), const), pl.BlockSpec((1, c), const), pl.BlockSpec((1, c), const)],
        out_specs=pl.BlockSpec((None, tt, c), tok),
        out_shape=jax.ShapeDtypeStruct((b, l, c), BF16),
        compiler_params=_cparams("arbitrary", "arbitrary"),
        name="hinv",
    )(cm, sm, a, bm, x0, z, zn, kn, bias, gain)


def _mixout_kernel(an_ref, yn_ref, x_ref, mod_ref, wo_ref, n2_ref, wr_ref, br_ref, tri_ref,
                   xn_ref, h2_ref, route_ref, cnt_ref, carry):
    bi = pl.program_id(0)
    d = x_ref.shape[-1]
    half = an_ref.shape[-1]
    tl = x_ref.shape[0]

    @pl.when((bi == 0) & (pl.program_id(1) == 0))
    def _():
        carry[...] = jnp.zeros_like(carry)

    mix = _dot(an_ref[...], wo_ref[0:half, :]) + _dot(yn_ref[...], wo_ref[half:, :])
    g1 = mod_ref[pl.ds(bi, 1), 2 * d:3 * d]
    sh2 = mod_ref[pl.ds(bi, 1), 3 * d:4 * d]
    sc2 = mod_ref[pl.ds(bi, 1), 4 * d:5 * d]
    xn = x_ref[...] + g1 * mix
    xn_ref[...] = xn
    h2 = (xn * _rms(xn)) * n2_ref[...] * (1 + sc2) + sh2
    for j in range(d // LANES):
        h2_ref[:, j, :] = h2[:, j * LANES:(j + 1) * LANES]

    logits = _dot3(h2, wr_ref[...]) + br_ref[...]
    lane = lax.broadcasted_iota(jnp.int32, (tl, LANES), 1).astype(F32)
    vals, idxs, sels = [], [], []
    cur = logits
    for _ in range(TOP_K):
        m = jnp.max(cur, axis=-1, keepdims=True)
        idx = jnp.min(jnp.where(cur == m, lane, float(LANES)), axis=-1, keepdims=True)
        sel = lane == idx
        vals.append(m)
        idxs.append(idx)
        sels.append(sel)
        cur = jnp.where(sel, -jnp.inf, cur)
    es = [jnp.exp(v - vals[0]) for v in vals]
    den = es[0] + es[1] + es[2] + es[3]
    hot = sum(s.astype(F32) for s in sels)
    before = _dot(tri_ref[...], hot.astype(BF16)) + carry[...]
    carry[...] = carry[...] + jnp.sum(hot, axis=0, keepdims=True)
    cnt_ref[...] = carry[...]
    route = jnp.zeros((tl, LANES), F32)
    for k in range(TOP_K):
        pos = jnp.sum(jnp.where(sels[k], before, 0.0), axis=-1, keepdims=True)
        route = jnp.where(lane == k, idxs[k], route)
        route = jnp.where(lane == TOP_K + k, es[k] / den, route)
        route = jnp.where(lane == 2 * TOP_K + k, pos, route)
    route_ref[...] = route


def _mixout(an, yn, x, mod, w_out, norm2, wr, br, tri, tl):
    b, l, d = x.shape
    half = an.shape[-1]
    nt = l // tl
    tok = lambda bi, i: (bi, i, 0)
    const = lambda bi, i: (0, 0)
    flat = lambda bi, i: (bi * nt + i, 0)
    return pl.pallas_call(
        _mixout_kernel,
        grid=(b, nt),
        in_specs=[pl.BlockSpec((None, tl, half), tok), pl.BlockSpec((None, tl, half), tok),
                  pl.BlockSpec((None, tl, d), tok),
                  pl.BlockSpec(mod.shape, const), pl.BlockSpec(w_out.shape, const),
                  pl.BlockSpec((1, d), const), pl.BlockSpec(wr.shape, const),
                  pl.BlockSpec((1, LANES), const), pl.BlockSpec((tl, tl), const)],
        out_specs=[pl.BlockSpec((None, tl, d), tok),
                   pl.BlockSpec((tl, d // LANES, LANES), lambda bi, i: (bi * nt + i, 0, 0)),
                   pl.BlockSpec((tl, LANES), flat),
                   pl.BlockSpec((1, LANES), const)],
        out_shape=[jax.ShapeDtypeStruct((b, l, d), F32),
                   jax.ShapeDtypeStruct((b * l, d // LANES, LANES), F32),
                   jax.ShapeDtypeStruct((b * l, LANES), F32),
                   jax.ShapeDtypeStruct((1, LANES), F32)],
        scratch_shapes=[pltpu.VMEM((1, LANES), F32)],
        compiler_params=_cparams("arbitrary", "arbitrary"),
        name="mixout",
    )(an, yn, x, mod, w_out, norm2, wr, br, tri)


def _gather_rows(idx_ref, n, src_hbm, dst, sem):
    def body(r, carry):
        pltpu.make_async_copy(src_hbm.at[idx_ref[0, r]], dst.at[r], sem).start()
        return carry
    lax.fori_loop(0, n, body, 0, unroll=8)


def _wait_rows(src_hbm, dst, sem):
    pltpu.make_async_copy(src_hbm.at[pl.ds(0, dst.shape[0])], dst, sem).wait()


def _experts_kernel(be_ref, nvb_ref, tok_ref, tokn_ref, h2_hbm, wgu_ref, bgu_ref, wdn_ref, bdn_ref,
                    ys_ref, xbuf, sem):
    i = pl.program_id(0)
    rows = xbuf.shape[1]
    nchunk = xbuf.shape[2]
    ff = wdn_ref.shape[0]
    slot = i % 2
    nvb = nvb_ref[0]

    @pl.when(i == 0)
    def _():
        _gather_rows(tok_ref, rows, h2_hbm, xbuf.at[0], sem.at[0])

    @pl.when(i + 1 < nvb)
    def _():
        _gather_rows(tokn_ref, rows, h2_hbm, xbuf.at[1 - slot], sem.at[1 - slot])

    @pl.when(i < nvb)
    def _():
        _wait_rows(h2_hbm, xbuf.at[slot], sem.at[slot])
        xs = jnp.concatenate([xbuf[slot, :, j, :] for j in range(nchunk)], axis=1).astype(BF16)
        gu = _dot(xs, wgu_ref[...]) + bgu_ref[...]
        gate = jnp.minimum(gu[:, :ff], SWIGLU_LIMIT)
        up = jnp.clip(gu[:, ff:], -SWIGLU_LIMIT, SWIGLU_LIMIT)
        glu = gate * jax.nn.sigmoid(SWIGLU_ALPHA * gate)
        y = _dot(((up + 1) * glu).astype(BF16), wdn_ref[...]) + bdn_ref[...]
        for j in range(nchunk):
            ys_ref[:, j, :] = y[:, j * LANES:(j + 1) * LANES]

    @pl.when(i >= nvb)
    def _():
        ys_ref[...] = jnp.zeros_like(ys_ref)


def _experts(block_e, nvb, slot_tok, h2, wgu, bgu, wdn, bdn, rows):
    n_blk = block_e.shape[0]
    _, nchunk, _ = h2.shape
    d = nchunk * LANES
    ff2 = wgu.shape[-1]
    ff = wdn.shape[1]
    grid_spec = pltpu.PrefetchScalarGridSpec(
        num_scalar_prefetch=2,
        grid=(n_blk,),
        in_specs=[pl.BlockSpec((None, 1, rows), lambda i, be, nv: (i, 0, 0), memory_space=pltpu.SMEM),
                  pl.BlockSpec((None, 1, rows), lambda i, be, nv: (jnp.minimum(i + 1, n_blk - 1), 0, 0),
                               memory_space=pltpu.SMEM),
                  pl.BlockSpec(memory_space=pl.ANY),
                  pl.BlockSpec((None, d, ff2), lambda i, be, nv: (be[i], 0, 0)),
                  pl.BlockSpec((None, 1, ff2), lambda i, be, nv: (be[i], 0, 0)),
                  pl.BlockSpec((None, ff, d), lambda i, be, nv: (be[i], 0, 0)),
                  pl.BlockSpec((None, 1, d), lambda i, be, nv: (be[i], 0, 0))],
        out_specs=pl.BlockSpec((rows, nchunk, LANES), lambda i, be, nv: (i, 0, 0)),
        scratch_shapes=[pltpu.VMEM((2, rows, nchunk, LANES), F32), pltpu.SemaphoreType.DMA((2,))],
    )
    return pl.pallas_call(
        _experts_kernel,
        grid_spec=grid_spec,
        out_shape=jax.ShapeDtypeStruct((n_blk * rows, nchunk, LANES), F32),
        compiler_params=_cparams("arbitrary"),
        name="experts",
    )(block_e, nvb, slot_tok, slot_tok, h2, wgu, bgu, wdn, bdn)


def _combine_kernel(dest_ref, destn_ref, ys_hbm, xn_ref, route_ref, g2_ref, o_ref, ybuf, sem):
    i = pl.program_id(0)
    n = pl.num_programs(0)
    tc = xn_ref.shape[0]
    nchunk = ybuf.shape[2]
    slot = i % 2

    @pl.when(i == 0)
    def _():
        _gather_rows(dest_ref, TOP_K * tc, ys_hbm, ybuf.at[0], sem.at[0])

    @pl.when(i + 1 < n)
    def _():
        _gather_rows(destn_ref, TOP_K * tc, ys_hbm, ybuf.at[1 - slot], sem.at[1 - slot])

    _wait_rows(ys_hbm, ybuf.at[slot], sem.at[slot])
    route = route_ref[...]
    gates = [route[:, TOP_K + k:TOP_K + k + 1] for k in range(TOP_K)]
    g2 = g2_ref[...]
    for j in range(nchunk):
        cs = slice(j * LANES, (j + 1) * LANES)
        acc = gates[0] * ybuf[slot, pl.ds(0, tc), j, :]
        for k in range(1, TOP_K):
            acc = acc + gates[k] * ybuf[slot, pl.ds(k * tc, tc), j, :]
        o_ref[:, cs] = xn_ref[:, cs] + g2[:, cs] * acc


def _combine(dest_kmaj, ys, xn, route, g2, tc):
    t, d = xn.shape
    nchunk = d // LANES
    n = t // tc
    per_batch = t // g2.shape[0] // tc
    return pl.pallas_call(
        _combine_kernel,
        grid=(n,),
        in_specs=[pl.BlockSpec((None, 1, TOP_K * tc), lambda i: (i, 0, 0), memory_space=pltpu.SMEM),
                  pl.BlockSpec((None, 1, TOP_K * tc), lambda i: (jnp.minimum(i + 1, n - 1), 0, 0),
                               memory_space=pltpu.SMEM),
                  pl.BlockSpec(memory_space=pl.ANY),
                  pl.BlockSpec((tc, d), lambda i: (i, 0)),
                  pl.BlockSpec((tc, LANES), lambda i: (i, 0)),
                  pl.BlockSpec((None, 1, d), lambda i: (i // per_batch, 0, 0))],
        out_specs=pl.BlockSpec((tc, d), lambda i: (i, 0)),
        out_shape=jax.ShapeDtypeStruct((t, d), F32),
        scratch_shapes=[pltpu.VMEM((2, TOP_K * tc, nchunk, LANES), F32), pltpu.SemaphoreType.DMA((2,))],
        compiler_params=_cparams("arbitrary"),
        name="combine",
    )(dest_kmaj, dest_kmaj, ys, xn, route, g2)


def _rope_tables(l):
    n_freq = HEAD_DIM // 4
    inv_freq = ROPE_BASE ** (-np.arange(n_freq, dtype=np.float64) / n_freq)
    tpos = np.arange(l)
    lane = np.arange(LANES) % HEAD_DIM
    pos = np.where(lane[None, :] < HEAD_DIM // 2, (tpos // GRID_W)[:, None], (tpos % GRID_W)[:, None])
    ang = pos * inv_freq[lane % n_freq][None, :]
    sign = np.where(lane % (2 * n_freq) < n_freq, -1.0, 1.0)[None, :]
    return jnp.asarray(np.cos(ang), F32), jnp.asarray(np.sin(ang) * sign, F32)


def _filter_features(l, width):
    t = np.linspace(0.0, 1.0, l)[:, None]
    w = 2.0 * math.pi * np.arange(l)[:, None] / l
    bands = np.linspace(1e-4, POS_BANDS - 1, POS_BANDS)[None, :]
    z = np.concatenate([t, np.cos(bands * w), -np.sin(bands * w)], axis=-1)
    return jnp.asarray(np.pad(z, ((0, 0), (0, width - z.shape[1]))), F32)


def _dft_matrices(l):
    idx = (jnp.arange(l, dtype=jnp.int32)[:, None] * jnp.arange(l, dtype=jnp.int32)[None, :]) % (2 * l)
    ang = idx.astype(F32) * (math.pi / l)
    return jnp.cos(ang).astype(BF16), jnp.sin(ang).astype(BF16)


def kernel(x, c, ctx, c_ctx, w_mod, b_mod, norm1, norm2, w_in, q_norm, k_norm, sink, conv_w, conv_b,
           filt_w1, filt_b1, filt_w2, filt_b2, filt_w3, filt_b3, filt_w4, filt_freq, filt_bias,
           attn_out_norm, hyena_out_norm, w_out, w_router, b_router, w_gu, b_gu, w_down, b_down):
    assert w_mod.shape[0] == 1, "single-layer configuration"
    b, l, d = x.shape
    t = b * l
    aw = N_Q_HEADS * HEAD_DIM
    kvw = N_KV_HEADS * HEAD_DIM
    hw = conv_w.shape[-1] // 3
    tl = min(512, l)

    ctx_row = b
    pad_rows = -(b + 1) % SUBLANES
    c_all = jnp.concatenate([c, c_ctx[None], jnp.zeros((pad_rows, d), F32)], axis=0)
    mod = _adaln(c_all, w_mod[0], b_mod[0])

    w = w_in[0]
    wq, wk, wv, wu = w[:, :aw], w[:, aw:aw + kvw], w[:, aw + kvw:aw + 2 * kvw], w[:, aw + 2 * kvw:]
    dup = lambda m: jnp.concatenate([m[:, h * HEAD_DIM:(h + 1) * HEAD_DIM]
                                     for h in range(N_KV_HEADS) for _ in range(2)], axis=1)
    w_all = jnp.concatenate([wq, dup(wk), dup(wv), wu], axis=1).astype(BF16)
    w_kv = jnp.concatenate([dup(wk), dup(wv)], axis=1).astype(BF16)
    gq = jnp.tile(q_norm[0], N_Q_HEADS)[None]
    gk = jnp.tile(k_norm[0], 2 * N_KV_HEADS)[None]
    bd = jnp.asarray(np.kron(np.eye(N_Q_HEADS), np.full((HEAD_DIM, HEAD_DIM), 1.0 / HEAD_DIM)), BF16)
    cos_t, sin_t = _rope_tables(l)

    q, k, v, u = _inproj(x, mod, norm1, w_all, gq, gk, bd, cos_t, sin_t, tl)
    kx, vx = _ctxkv(ctx, mod, ctx_row, norm1, w_kv, gk, bd[:2 * kvw, :2 * kvw])
    an = _attention(sink[0], q, k, v, kx, vx, attn_out_norm)

    x0, z, zn = _hconv(u, conv_w[0], conv_b)
    ffn = filt_w2.shape[-1]
    zf = _filter_features(l, ffn)
    w1 = jnp.pad(filt_w1[0], ((0, ffn - POS_EMB_DIM), (0, 0)))
    deltas = jnp.asarray(np.linspace(MIN_DECAY, MAX_DECAY, hw)[None, :], F32)
    kp, km, kn = _filter(zf, w1, filt_b1, filt_w2[0], filt_b2, filt_w3[0], filt_b3, filt_freq,
                         filt_w4[0], deltas)
    cm, sm = _dft_matrices(l)
    kr, ks = _kspec(cm, sm, kp, km, tl)
    fa, fb = _hfwd(cm, sm, z, kr, ks, tl)
    yn = _hinv(cm, sm, fa, fb, x0, z, zn, kn, filt_bias, hyena_out_norm, tl)

    wr = jnp.pad(w_router[0], ((0, 0), (0, LANES - N_EXPERTS)))
    br = jnp.concatenate([b_router[0], jnp.full((LANES - N_EXPERTS,), NEG_INF, F32)])[None]
    tri = jnp.asarray(np.tril(np.ones((tl, tl)), -1), BF16)
    xn, h2, route, cnt = _mixout(an, yn, x, mod, w_out[0].astype(BF16), norm2, wr, br, tri, tl)

    rows = EXPERT_ROWS
    a_tot = t * TOP_K
    n_blk = -(-a_tot // rows) + N_EXPERTS
    idx = route[:, 0:TOP_K].astype(jnp.int32)
    pos = route[:, 2 * TOP_K:3 * TOP_K].astype(jnp.int32)
    counts = cnt[0, :N_EXPERTS].astype(jnp.int32)
    pcounts = (counts + rows - 1) // rows * rows
    pends = jnp.cumsum(pcounts)
    pstarts = pends - pcounts
    dest = pstarts[idx] + pos
    slot_tok = jnp.zeros((n_blk * rows,), jnp.int32).at[dest.reshape(-1)].set(
        jnp.arange(a_tot, dtype=jnp.int32) // TOP_K)
    block_e = jnp.minimum(jnp.searchsorted(pends, jnp.arange(n_blk, dtype=jnp.int32) * rows, side='right'),
                          N_EXPERTS - 1).astype(jnp.int32)
    nvb = (pends[-1:] // rows).astype(jnp.int32)

    ys = _experts(block_e, nvb, slot_tok.reshape(n_blk, 1, rows), h2,
                  w_gu[0].astype(BF16), b_gu[0][:, None, :], w_down[0].astype(BF16), b_down[0][:, None, :], rows)

    tc = min(COMBINE_ROWS, l)
    dest_kmaj = dest.reshape(t // tc, tc, TOP_K).transpose(0, 2, 1).reshape(t // tc, 1, TOP_K * tc)
    g2 = mod[:b, None, 5 * d:6 * d]
    out = _combine(dest_kmaj, ys, xn.reshape(t, d), route, g2, tc)
    return out.reshape(b, l, d)
```

```python
import functools
import math

import numpy as np
import jax
import jax.numpy as jnp
from jax import lax
from jax.experimental import pallas as pl
from jax.experimental.pallas import tpu as pltpu

F32 = jnp.float32
BF16 = jnp.bfloat16

LANES = 128
SUBLANES = 8
VMEM_LIMIT = 56 * 1024 * 1024

HEAD_DIM = 64
N_Q_HEADS = 8
N_KV_HEADS = 2
GROUP = N_Q_HEADS // N_KV_HEADS
GRID_W = 64
WINDOW = 128
ROPE_BASE = 10000.0
ATTN_SCALE = HEAD_DIM ** -0.5
POS_EMB_DIM = 33
POS_BANDS = (POS_EMB_DIM - 1) // 2
DECAY_TARGET = 1e-2
MAX_DECAY = -math.log(DECAY_TARGET) / 0.3
MIN_DECAY = -math.log(DECAY_TARGET) / 1.5
N_EXPERTS = 32
TOP_K = 4
SWIGLU_LIMIT = 7.0
SWIGLU_ALPHA = 1.702
EPS = 1e-6
NEG_INF = -1e30

EXPERT_ROWS = 256
COMBINE_ROWS = 256


def _cparams(*sem):
    return pltpu.CompilerParams(dimension_semantics=sem, vmem_limit_bytes=VMEM_LIMIT)


def _split(a):
    hi = a.astype(BF16)
    lo = (a - hi.astype(F32)).astype(BF16)
    return hi, lo


def _dot(a, b):
    return jnp.dot(a, b, preferred_element_type=F32)


def _dot3(a, b):
    ah, al = _split(a)
    bh, bl = _split(b)
    return _dot(ah, bh) + _dot(al, bh) + _dot(ah, bl)


def _rms(x):
    return lax.rsqrt(jnp.mean(x * x, axis=-1, keepdims=True) + EPS)


def _adaln_kernel(c_ref, w_ref, b_ref, o_ref):
    c = c_ref[...]
    o_ref[...] = _dot3(c * jax.nn.sigmoid(c), w_ref[...]) + b_ref[...]


def _adaln(c_all, w_mod, b_mod):
    rows, d = c_all.shape
    n = w_mod.shape[1]
    tn = 1024
    return pl.pallas_call(
        _adaln_kernel,
        grid=(n // tn,),
        in_specs=[pl.BlockSpec((rows, d), lambda j: (0, 0)),
                  pl.BlockSpec((d, tn), lambda j: (0, j)),
                  pl.BlockSpec((1, tn), lambda j: (0, j))],
        out_specs=pl.BlockSpec((rows, tn), lambda j: (0, j)),
        out_shape=jax.ShapeDtypeStruct((rows, n), F32),
        compiler_params=_cparams("arbitrary"),
        name="adaln",
    )(c_all, w_mod, b_mod[None])


def _head_rms(x, bd):
    hi, lo = _split(x * x)
    return x * lax.rsqrt(_dot(hi, bd) + _dot(lo, bd) + EPS)


def _rope128(x, cos, sin):
    lane = lax.broadcasted_iota(jnp.int32, x.shape, 1)
    partner = jnp.where(lane % 32 < 16, pltpu.roll(x, LANES - 16, 1), pltpu.roll(x, 16, 1))
    return x * cos + partner * sin


def _modulated(x, mod_ref, row, norm_ref, d):
    sh = mod_ref[pl.ds(row, 1), 0:d]
    sc = mod_ref[pl.ds(row, 1), d:2 * d]
    return (x * _rms(x)) * norm_ref[...] * (1 + sc) + sh


def _inproj_kernel(x_ref, mod_ref, n1_ref, w_ref, gq_ref, gk_ref, bd_ref, cos_ref, sin_ref,
                   q_ref, k_ref, v_ref, u_ref):
    d = x_ref.shape[-1]
    aw = q_ref.shape[-1]
    kw = k_ref.shape[-1]
    hb = _modulated(x_ref[...], mod_ref, pl.program_id(1), n1_ref, d).astype(BF16)
    cos = cos_ref[...]
    sin = sin_ref[...]
    bd = bd_ref[...]
    q = _head_rms(_dot(hb, w_ref[:, 0:aw]), bd) * gq_ref[...]
    for c in range(aw // LANES):
        sl = slice(c * LANES, (c + 1) * LANES)
        q_ref[:, sl] = (_rope128(q[:, sl], cos, sin) * ATTN_SCALE).astype(BF16)
    k = _head_rms(_dot(hb, w_ref[:, aw:aw + kw]), bd[0:kw, 0:kw]) * gk_ref[...]
    for c in range(kw // LANES):
        sl = slice(c * LANES, (c + 1) * LANES)
        k_ref[:, sl] = _rope128(k[:, sl], cos, sin).astype(BF16)
    v_ref[...] = _dot(hb, w_ref[:, aw + kw:aw + 2 * kw]).astype(BF16)
    u_ref[...] = _dot(hb, w_ref[:, aw + 2 * kw:])


def _inproj(x, mod, norm1, w_all, gq, gk, bd, cos_t, sin_t, tl):
    b, l, d = x.shape
    aw, kw = gq.shape[1], gk.shape[1]
    uw = w_all.shape[1] - aw - 2 * kw
    const = lambda i, j: (0, 0)
    tok = lambda i, j: (j, i, 0)
    return pl.pallas_call(
        _inproj_kernel,
        grid=(l // tl, b),
        in_specs=[pl.BlockSpec((None, tl, d), tok),
                  pl.BlockSpec(mod.shape, const),
                  pl.BlockSpec((1, d), const),
                  pl.BlockSpec(w_all.shape, const),
                  pl.BlockSpec((1, aw), const),
                  pl.BlockSpec((1, kw), const),
                  pl.BlockSpec(bd.shape, const),
                  pl.BlockSpec((tl, LANES), lambda i, j: (i, 0)),
                  pl.BlockSpec((tl, LANES), lambda i, j: (i, 0))],
        out_specs=[pl.BlockSpec((None, tl, aw), tok),
                   pl.BlockSpec((None, tl, kw), tok),
                   pl.BlockSpec((None, tl, kw), tok),
                   pl.BlockSpec((None, tl, uw), tok)],
        out_shape=[jax.ShapeDtypeStruct((b, l, aw), BF16),
                   jax.ShapeDtypeStruct((b, l, kw), BF16),
                   jax.ShapeDtypeStruct((b, l, kw), BF16),
                   jax.ShapeDtypeStruct((b, l, uw), F32)],
        compiler_params=_cparams("arbitrary", "arbitrary"),
        name="inproj",
    )(x, mod, norm1, w_all, gq, gk, bd, cos_t, sin_t)


def _ctxkv_kernel(row, x_ref, mod_ref, n1_ref, w_ref, gk_ref, bd_ref, k_ref, v_ref):
    d = x_ref.shape[-1]
    kw = k_ref.shape[-1]
    hb = _modulated(x_ref[...], mod_ref, row, n1_ref, d).astype(BF16)
    k = _head_rms(_dot(hb, w_ref[:, 0:kw]), bd_ref[...]) * gk_ref[...]
    k_ref[...] = k.astype(BF16)
    v_ref[...] = _dot(hb, w_ref[:, kw:]).astype(BF16)


def _ctxkv(ctx, mod, ctx_row, norm1, w_kv, gk, bd):
    b, lc, d = ctx.shape
    kw = gk.shape[1]
    const = lambda i: (0, 0)
    tok = lambda i: (i, 0, 0)
    return pl.pallas_call(
        functools.partial(_ctxkv_kernel, ctx_row),
        grid=(b,),
        in_specs=[pl.BlockSpec((None, lc, d), tok),
                  pl.BlockSpec(mod.shape, const),
                  pl.BlockSpec((1, d), const),
                  pl.BlockSpec(w_kv.shape, const),
                  pl.BlockSpec((1, kw), const),
                  pl.BlockSpec(bd.shape, const)],
        out_specs=[pl.BlockSpec((None, lc, kw), tok), pl.BlockSpec((None, lc, kw), tok)],
        out_shape=[jax.ShapeDtypeStruct((b, lc, kw), BF16)] * 2,
        compiler_params=_cparams("arbitrary"),
        name="ctxkv",
    )(ctx, mod, norm1, w_kv, gk, bd)


def _attn_kernel(sink_ref, q_ref, kp_ref, kc_ref, kn_ref, vp_ref, vc_ref, vn_ref, kx_ref, vx_ref,
                 ga_ref, o_ref, acc_ref):
    i = pl.program_id(1)
    nb = pl.num_programs(1)
    tq = q_ref.shape[0]
    lc = kx_ref.shape[0]
    nk = 3 * tq + lc
    pairs = GROUP // 2
    rows = pairs * tq
    r = lax.broadcasted_iota(jnp.int32, (rows, tq), 0) % tq
    j = lax.broadcasted_iota(jnp.int32, (rows, tq), 1)
    ok_prev = j >= r + jnp.where(i > 0, 0, tq)
    ok_next = j <= r - jnp.where(i < nb - 1, 0, tq)
    lo = lax.broadcasted_iota(jnp.int32, (nk, LANES), 1) < HEAD_DIM
    zero = jnp.zeros((nk, LANES), BF16)
    top = lax.broadcasted_iota(jnp.int32, (rows, 1), 0) < tq

    ks, vs, qs, sinks = [], [], [], []
    for h in range(N_KV_HEADS):
        hs = slice(h * LANES, (h + 1) * LANES)
        kcat = jnp.concatenate([kp_ref[:, hs], kc_ref[:, hs], kn_ref[:, hs], kx_ref[:, hs]], axis=0)
        vcat = jnp.concatenate([vp_ref[:, hs], vc_ref[:, hs], vn_ref[:, hs], vx_ref[:, hs]], axis=0)
        q2 = jnp.concatenate([q_ref[:, (h * pairs + p) * LANES:(h * pairs + p + 1) * LANES]
                              for p in range(pairs)], axis=0)
        for half in range(2):
            ks.append(jnp.where(lo, kcat, zero) if half == 0 else jnp.where(lo, zero, kcat))
            vs.append(jnp.where(lo, vcat, zero) if half == 0 else jnp.where(lo, zero, vcat))
            qs.append(q2)
            sinks.append(jnp.where(top, sink_ref[h * GROUP + half], sink_ref[h * GROUP + 2 + half]))
    ss = [lax.dot_general(qq, kk, (((1,), (1,)), ((), ())), preferred_element_type=F32)
          for qq, kk in zip(qs, ks)]
    ss = [jnp.concatenate([jnp.where(ok_prev, s[:, 0:tq], NEG_INF), s[:, tq:2 * tq],
                           jnp.where(ok_next, s[:, 2 * tq:3 * tq], NEG_INF), s[:, 3 * tq:]], axis=1)
          for s in ss]
    ms = [jnp.maximum(jnp.max(s, axis=-1, keepdims=True), sk) for s, sk in zip(ss, sinks)]
    es = [jnp.exp(s - m) for s, m in zip(ss, ms)]
    dens = [jnp.sum(e, axis=-1, keepdims=True) + jnp.exp(sk - m) for e, sk, m in zip(es, sinks, ms)]
    outs = [_dot(e.astype(BF16), vv) / den for e, vv, den in zip(es, vs, dens)]
    for h in range(N_KV_HEADS):
        both = outs[2 * h] + outs[2 * h + 1]
        for p in range(pairs):
            acc_ref[:, (h * pairs + p) * LANES:(h * pairs + p + 1) * LANES] = both[p * tq:(p + 1) * tq]
    a = acc_ref[...]
    o_ref[...] = (a * _rms(a) * ga_ref[...]).astype(BF16)


def _attention(sink, q, k, v, kx, vx, ga):
    b, l, aw = q.shape
    kw = k.shape[-1]
    lc = kx.shape[1]
    tq = WINDOW
    nb = l // tq
    cur = lambda bi, i: (bi, i, 0)
    prev = lambda bi, i: (bi, jnp.maximum(i - 1, 0), 0)
    nxt = lambda bi, i: (bi, jnp.minimum(i + 1, nb - 1), 0)
    ctx = lambda bi, i: (bi, 0, 0)
    kv = lambda m: pl.BlockSpec((None, tq, kw), m)
    return pl.pallas_call(
        _attn_kernel,
        grid=(b, nb),
        in_specs=[pl.BlockSpec(memory_space=pltpu.SMEM),
                  pl.BlockSpec((None, tq, aw), cur),
                  kv(prev), kv(cur), kv(nxt), kv(prev), kv(cur), kv(nxt),
                  pl.BlockSpec((None, lc, kw), ctx), pl.BlockSpec((None, lc, kw), ctx),
                  pl.BlockSpec((1, aw), lambda bi, i: (0, 0))],
        out_specs=pl.BlockSpec((None, tq, aw), cur),
        out_shape=jax.ShapeDtypeStruct((b, l, aw), BF16),
        scratch_shapes=[pltpu.VMEM((tq, aw), F32)],
        compiler_params=_cparams("arbitrary", "arbitrary"),
        name="attn",
    )(sink, q, k, k, k, v, v, v, kx, vx, ga)


def _hconv_kernel(u0_ref, u1_ref, u2_ref, w0_ref, w1_ref, w2_ref, b0_ref, b1_ref, b2_ref,
                  x0_ref, z_ref, zn_ref):
    l, cb = u0_ref.shape
    row = lax.broadcasted_iota(jnp.int32, (l, cb), 0)

    def conv(u_ref, w_ref, b_ref):
        u = u_ref[...]
        before = jnp.where(row == 0, 0.0, pltpu.roll(u, 1, 0))
        after = jnp.where(row == l - 1, 0.0, pltpu.roll(u, l - 1, 0))
        return b_ref[...] + before * w_ref[0:1, :] + u * w_ref[1:2, :] + after * w_ref[2:3, :]

    x0_ref[...] = conv(u0_ref, w0_ref, b0_ref)
    z = conv(u1_ref, w1_ref, b1_ref) * conv(u2_ref, w2_ref, b2_ref)
    z_ref[...] = z.astype(BF16)
    sign = (1 - 2 * (row % 2)).astype(F32)
    zn_ref[...] = jnp.sum(z * sign, axis=0, keepdims=True)


def _hconv(u, conv_w, conv_b, cb=LANES):
    b, l, w3 = u.shape
    c = w3 // 3
    n = c // cb
    us = [pl.BlockSpec((None, l, cb), lambda bi, j, g=g: (bi, 0, g * n + j)) for g in range(3)]
    ws = [pl.BlockSpec((3, cb), lambda bi, j, g=g: (0, g * n + j)) for g in range(3)]
    bs = [pl.BlockSpec((1, cb), lambda bi, j, g=g: (0, g * n + j)) for g in range(3)]
    out = lambda bi, j: (bi, 0, j)
    return pl.pallas_call(
        _hconv_kernel,
        grid=(b, n),
        in_specs=us + ws + bs,
        out_specs=[pl.BlockSpec((None, l, cb), out), pl.BlockSpec((None, l, cb), out),
                   pl.BlockSpec((None, 1, cb), out)],
        out_shape=[jax.ShapeDtypeStruct((b, l, c), F32), jax.ShapeDtypeStruct((b, l, c), BF16),
                   jax.ShapeDtypeStruct((b, 1, c), F32)],
        compiler_params=_cparams("arbitrary", "arbitrary"),
        name="hconv",
    )(u, u, u, conv_w, conv_w, conv_w, conv_b, conv_b, conv_b)


def _filter_kernel(z_ref, w1_ref, b1_ref, w2_ref, b2_ref, w3_ref, b3_ref, fr_ref, w4f_ref, w4b_ref,
                   dl_ref, kp_ref, km_ref, kn_ref, h_scr):
    l, cf = kp_ref.shape

    @pl.when(pl.program_id(0) == 0)
    def _():
        fr = fr_ref[...]
        h = jnp.sin(fr * (_dot3(z_ref[...], w1_ref[...]) + b1_ref[...]))
        h = jnp.sin(fr * (_dot3(h, w2_ref[...]) + b2_ref[...]))
        h_scr[...] = jnp.sin(fr * (_dot3(h, w3_ref[...]) + b3_ref[...]))

    h = h_scr[...]
    row = lax.broadcasted_iota(jnp.int32, (l, cf), 0)
    t = row.astype(F32) / (l - 1)
    decay = jnp.exp(-t * dl_ref[...])
    kf = _dot3(h, w4f_ref[...]) * decay
    kb = jnp.where(row == 0, 0.0, _dot3(h, w4b_ref[...]) * decay)
    nrm = lax.rsqrt(jnp.sum(kf * kf + kb * kb, axis=0, keepdims=True) + EPS)
    kp = (kf + kb) * nrm
    kp_ref[...] = kp.astype(BF16)
    km_ref[...] = ((kf - kb) * nrm).astype(BF16)
    sign = (1 - 2 * (row % 2)).astype(F32)
    kn_ref[...] = jnp.sum(kp * sign, axis=0, keepdims=True)


def _filter(zf, w1, b1, w2, b2, w3, b3, fr, w4, deltas, cf=LANES):
    l, zw = zf.shape
    ffn = w2.shape[0]
    c = w4.shape[1] // 2
    n = c // cf
    const = lambda j: (0, 0)
    vec = pl.BlockSpec((1, ffn), const)
    mat = pl.BlockSpec((ffn, ffn), const)
    return pl.pallas_call(
        _filter_kernel,
        grid=(n,),
        in_specs=[pl.BlockSpec((l, zw), const), pl.BlockSpec((zw, ffn), const), vec, mat, vec, mat, vec, vec,
                  pl.BlockSpec((ffn, cf), lambda j: (0, j)),
                  pl.BlockSpec((ffn, cf), lambda j: (0, n + j)),
                  pl.BlockSpec((1, cf), lambda j: (0, j))],
        out_specs=[pl.BlockSpec((l, cf), lambda j: (0, j)), pl.BlockSpec((l, cf), lambda j: (0, j)),
                   pl.BlockSpec((1, cf), lambda j: (0, j))],
        out_shape=[jax.ShapeDtypeStruct((l, c), BF16), jax.ShapeDtypeStruct((l, c), BF16),
                   jax.ShapeDtypeStruct((1, c), F32)],
        scratch_shapes=[pltpu.VMEM((l, ffn), F32)],
        compiler_params=_cparams("arbitrary"),
        name="filt",
    )(zf, w1, b1, w2, b2, w3, b3, fr, w4, w4, deltas)


DFT_FINE = 64


def _dftgen_kernel(ca_ref, sa_ref, cb_ref, sb_ref, c_ref, s_ref):
    cb = cb_ref[...]
    sb = sb_ref[...]
    for a in range(ca_ref.shape[0]):
        ca = ca_ref[a:a + 1, :]
        sa = sa_ref[a:a + 1, :]
        rs = slice(a * DFT_FINE, (a + 1) * DFT_FINE)
        c_ref[rs, :] = (ca * cb - sa * sb).astype(BF16)
        s_ref[rs, :] = (sa * cb + ca * sb).astype(BF16)


def _dftgen(l, tf):
    t = np.arange(l, dtype=np.int64)[None, :]
    coarse = (np.arange(l // DFT_FINE, dtype=np.int64)[:, None] * DFT_FINE * t) % (2 * l)
    fine = (np.arange(DFT_FINE, dtype=np.int64)[:, None] * t) % (2 * l)
    tabs = [jnp.asarray(fn(ang * (math.pi / l)), F32) for ang in (coarse, fine) for fn in (np.cos, np.sin)]
    na = tf // DFT_FINE
    return pl.pallas_call(
        _dftgen_kernel,
        grid=(l // tf,),
        in_specs=[pl.BlockSpec((na, l), lambda i: (i, 0)), pl.BlockSpec((na, l), lambda i: (i, 0)),
                  pl.BlockSpec((DFT_FINE, l), lambda i: (0, 0)), pl.BlockSpec((DFT_FINE, l), lambda i: (0, 0))],
        out_specs=[pl.BlockSpec((tf, l), lambda i: (i, 0)), pl.BlockSpec((tf, l), lambda i: (i, 0))],
        out_shape=[jax.ShapeDtypeStruct((l, l), BF16)] * 2,
        compiler_params=_cparams("arbitrary"),
        name="dftgen",
    )(*tabs)


def _kspec_kernel(n_fft, c_ref, s_ref, kp_ref, km_ref, kr_ref, ks_ref):
    tf = c_ref.shape[0]
    f = pl.program_id(0) * tf + lax.broadcasted_iota(jnp.int32, (tf, 1), 0)
    w = jnp.where(f == 0, 1.0 / n_fft, 2.0 / n_fft)
    kr_ref[...] = _dot(c_ref[...], kp_ref[...]) * w
    ks_ref[...] = _dot(s_ref[...], km_ref[...]) * w


def _kspec(cm, sm, kp, km, tf):
    l, c = kp.shape
    const = lambda i: (0, 0)
    tile = lambda i: (i, 0)
    return pl.pallas_call(
        functools.partial(_kspec_kernel, 2 * l),
        grid=(l // tf,),
        in_specs=[pl.BlockSpec((tf, l), tile), pl.BlockSpec((tf, l), tile),
                  pl.BlockSpec((l, c), const), pl.BlockSpec((l, c), const)],
        out_specs=[pl.BlockSpec((tf, c), tile), pl.BlockSpec((tf, c), tile)],
        out_shape=[jax.ShapeDtypeStruct((l, c), F32)] * 2,
        compiler_params=_cparams("arbitrary"),
        name="kspec",
    )(cm, sm, kp, km)


def _hfwd_kernel(c_ref, s_ref, z_ref, kr_ref, ks_ref, a_ref, b_ref):
    z = z_ref[...]
    zr = _dot(c_ref[...], z)
    zs = _dot(s_ref[...], z)
    kr = kr_ref[...]
    ks = ks_ref[...]
    a_ref[...] = (zr * kr - zs * ks).astype(BF16)
    b_ref[...] = (zr * ks + zs * kr).astype(BF16)


def _hfwd(cm, sm, z, kr, ks, tf):
    b, l, c = z.shape
    tile = lambda i, bi: (i, 0)
    out = lambda i, bi: (bi, i, 0)
    return pl.pallas_call(
        _hfwd_kernel,
        grid=(l // tf, b),
        in_specs=[pl.BlockSpec((tf, l), tile), pl.BlockSpec((tf, l), tile),
                  pl.BlockSpec((None, l, c), lambda i, bi: (bi, 0, 0)),
                  pl.BlockSpec((tf, c), tile), pl.BlockSpec((tf, c), tile)],
        out_specs=[pl.BlockSpec((None, tf, c), out), pl.BlockSpec((None, tf, c), out)],
        out_shape=[jax.ShapeDtypeStruct((b, l, c), BF16)] * 2,
        compiler_params=_cparams("arbitrary", "arbitrary"),
        name="hfwd",
    )(cm, sm, z, kr, ks)


def _hinv_kernel(c_ref, s_ref, a_ref, b_ref, x0_ref, z_ref, zn_ref, kn_ref, bias_ref, g_ref, o_ref):
    tt = c_ref.shape[0]
    n_fft = 2 * c_ref.shape[1]
    y = _dot(c_ref[...], a_ref[...]) + _dot(s_ref[...], b_ref[...])
    t = pl.program_id(0) * tt + lax.broadcasted_iota(jnp.int32, (tt, 1), 0)
    sign = (1 - 2 * (t % 2)).astype(F32)
    y = y + sign * (zn_ref[...] * kn_ref[...] * (1.0 / n_fft))
    hy = x0_ref[...] * (y + z_ref[...].astype(F32) * bias_ref[...])
    o_ref[...] = (hy * _rms(hy) * g_ref[...]).astype(BF16)


def _hinv(cm, sm, a, bm, x0, z, zn, kn, bias, gain, tt):
    b, l, c = z.shape
    tile = lambda i, bi: (i, 0)
    full = lambda i, bi: (bi, 0, 0)
    tok = lambda i, bi: (bi, i, 0)
    const = lambda i, bi: (0, 0)
    return pl.pallas_call(
        _hinv_kernel,
        grid=(l // tt, b),
        in_specs=[pl.BlockSpec((tt, l), tile), pl.BlockSpec((tt, l), tile),
                  pl.BlockSpec((None, l, c), full), pl.BlockSpec((None, l, c), full),
                  pl.BlockSpec((None, tt, c), tok), pl.BlockSpec((None, tt, c), tok),
                  pl.BlockSpec((None, 1, c), full),
                  pl.BlockSpec((1, c), const), pl.BlockSpec((1, c), const), pl.BlockSpec((1, c), const)],
        out_specs=pl.BlockSpec((None, tt, c), tok),
        out_shape=jax.ShapeDtypeStruct((b, l, c), BF16),
        compiler_params=_cparams("arbitrary", "arbitrary"),
        name="hinv",
    )(cm, sm, a, bm, x0, z, zn, kn, bias, gain)


def _mixout_kernel(an_ref, yn_ref, x_ref, mod_ref, wo_ref, n2_ref, wr_ref, br_ref, tri_ref,
                   xn_ref, h2_ref, route_ref, cnt_ref, carry):
    bi = pl.program_id(0)
    d = x_ref.shape[-1]
    half = an_ref.shape[-1]
    tl = x_ref.shape[0]

    @pl.when((bi == 0) & (pl.program_id(1) == 0))
    def _():
        carry[...] = jnp.zeros_like(carry)

    mix = _dot(an_ref[...], wo_ref[0:half, :]) + _dot(yn_ref[...], wo_ref[half:, :])
    g1 = mod_ref[pl.ds(bi, 1), 2 * d:3 * d]
    sh2 = mod_ref[pl.ds(bi, 1), 3 * d:4 * d]
    sc2 = mod_ref[pl.ds(bi, 1), 4 * d:5 * d]
    xn = x_ref[...] + g1 * mix
    xn_ref[...] = xn
    h2 = (xn * _rms(xn)) * n2_ref[...] * (1 + sc2) + sh2
    h2_ref[...] = h2

    logits = _dot3(h2, wr_ref[...]) + br_ref[...]
    lane = lax.broadcasted_iota(jnp.int32, (tl, LANES), 1).astype(F32)
    vals, idxs, sels = [], [], []
    cur = logits
    for _ in range(TOP_K):
        m = jnp.max(cur, axis=-1, keepdims=True)
        idx = jnp.min(jnp.where(cur == m, lane, float(LANES)), axis=-1, keepdims=True)
        sel = lane == idx
        vals.append(m)
        idxs.append(idx)
        sels.append(sel)
        cur = jnp.where(sel, -jnp.inf, cur)
    es = [jnp.exp(v - vals[0]) for v in vals]
    den = es[0] + es[1] + es[2] + es[3]
    hot = sum(s.astype(F32) for s in sels)
    before = _dot(tri_ref[...], hot.astype(BF16)) + carry[...]
    carry[...] = carry[...] + jnp.sum(hot, axis=0, keepdims=True)
    cnt_ref[...] = carry[...]
    route = jnp.zeros((tl, LANES), F32)
    for k in range(TOP_K):
        pos = jnp.sum(jnp.where(sels[k], before, 0.0), axis=-1, keepdims=True)
        route = jnp.where(lane == k, idxs[k], route)
        route = jnp.where(lane == TOP_K + k, es[k] / den, route)
        route = jnp.where(lane == 2 * TOP_K + k, pos, route)
    route_ref[...] = route


def _mixout(an, yn, x, mod, w_out, norm2, wr, br, tri, tl):
    b, l, d = x.shape
    half = an.shape[-1]
    nt = l // tl
    tok = lambda bi, i: (bi, i, 0)
    const = lambda bi, i: (0, 0)
    flat = lambda bi, i: (bi * nt + i, 0)
    return pl.pallas_call(
        _mixout_kernel,
        grid=(b, nt),
        in_specs=[pl.BlockSpec((None, tl, half), tok), pl.BlockSpec((None, tl, half), tok),
                  pl.BlockSpec((None, tl, d), tok),
                  pl.BlockSpec(mod.shape, const), pl.BlockSpec(w_out.shape, const),
                  pl.BlockSpec((1, d), const), pl.BlockSpec(wr.shape, const),
                  pl.BlockSpec((1, LANES), const), pl.BlockSpec((tl, tl), const)],
        out_specs=[pl.BlockSpec((None, tl, d), tok),
                   pl.BlockSpec((tl, d), flat),
                   pl.BlockSpec((tl, LANES), flat),
                   pl.BlockSpec((1, LANES), const)],
        out_shape=[jax.ShapeDtypeStruct((b, l, d), F32),
                   jax.ShapeDtypeStruct((b * l, d), F32),
                   jax.ShapeDtypeStruct((b * l, LANES), F32),
                   jax.ShapeDtypeStruct((1, LANES), F32)],
        scratch_shapes=[pltpu.VMEM((1, LANES), F32)],
        compiler_params=_cparams("arbitrary", "arbitrary"),
        name="mixout",
    )(an, yn, x, mod, w_out, norm2, wr, br, tri)


def _row_copy(idx_ref, r, src_hbm, dst, sem):
    return pltpu.make_async_copy(src_hbm.at[pl.ds(idx_ref[0, r], 1)], dst.at[pl.ds(r, 1)], sem)


def _gather_rows(idx_ref, n, src_hbm, dst, sem):
    def body(r, carry):
        _row_copy(idx_ref, 2 * r, src_hbm, dst, sem).start(priority=0)
        _row_copy(idx_ref, 2 * r + 1, src_hbm, dst, sem).start(priority=1)
        return carry
    lax.fori_loop(0, n // 2, body, 0, unroll=4)


def _wait_rows(src_hbm, dst, sem):
    pltpu.make_async_copy(src_hbm.at[pl.ds(0, dst.shape[0])], dst, sem).wait()


EXPERT_STAGES = 4


def _experts_kernel(be_ref, tok_ref, tokn_ref, h2_hbm, wgu_ref, bgu_ref, wdn_ref, bdn_ref,
                    ys_ref, xbuf, wgu_bf, wdn_bf, sem):
    i = pl.program_id(0)
    n = pl.num_programs(0)
    rows = xbuf.shape[1]
    ff = wdn_ref.shape[0]
    fc = ff // EXPERT_STAGES
    rc = -(-rows // (EXPERT_STAGES - 1))
    slot = i % 2
    other = 1 - slot

    @pl.when(i == 0)
    def _():
        _gather_rows(tok_ref, rows, h2_hbm, xbuf.at[0], sem.at[0])

    @pl.when((i == 0) | (be_ref[i] != be_ref[jnp.maximum(i - 1, 0)]))
    def _():
        wgu_bf[...] = wgu_ref[...].astype(BF16)
        wdn_bf[...] = wdn_ref[...].astype(BF16)

    _wait_rows(h2_hbm, xbuf.at[slot], sem.at[slot])
    acc = None
    for c in range(EXPERT_STAGES):
        xs = xbuf[slot].astype(BF16)
        for r in range(c * rc, min((c + 1) * rc, rows)):
            _row_copy(tokn_ref, r, h2_hbm, xbuf.at[other], sem.at[other]).start(priority=r % 2)
        cg = slice(c * fc, (c + 1) * fc)
        cu = slice(ff + c * fc, ff + (c + 1) * fc)
        gate = jnp.minimum(_dot(xs, wgu_bf[:, cg]) + bgu_ref[:, cg], SWIGLU_LIMIT)
        up = jnp.clip(_dot(xs, wgu_bf[:, cu]) + bgu_ref[:, cu], -SWIGLU_LIMIT, SWIGLU_LIMIT)
        act = ((up + 1) * (gate * jax.nn.sigmoid(SWIGLU_ALPHA * gate))).astype(BF16)
        part = _dot(act, wdn_bf[cg, :])
        acc = part if acc is None else acc + part
    ys_ref[...] = acc + bdn_ref[...]

    @pl.when(i == n - 1)
    def _():
        _wait_rows(h2_hbm, xbuf.at[other], sem.at[other])


def _experts(block_e, slot_tok, h2, wgu, bgu, wdn, bdn, rows):
    n_blk = block_e.shape[0]
    d = h2.shape[-1]
    ff2 = wgu.shape[-1]
    ff = wdn.shape[1]
    grid_spec = pltpu.PrefetchScalarGridSpec(
        num_scalar_prefetch=1,
        grid=(n_blk,),
        in_specs=[pl.BlockSpec((None, 1, rows), lambda i, be: (i, 0, 0), memory_space=pltpu.SMEM),
                  pl.BlockSpec((None, 1, rows), lambda i, be: (jnp.minimum(i + 1, n_blk - 1), 0, 0),
                               memory_space=pltpu.SMEM),
                  pl.BlockSpec(memory_space=pl.ANY),
                  pl.BlockSpec((None, d, ff2), lambda i, be: (be[i], 0, 0)),
                  pl.BlockSpec((None, 1, ff2), lambda i, be: (be[i], 0, 0)),
                  pl.BlockSpec((None, ff, d), lambda i, be: (be[i], 0, 0)),
                  pl.BlockSpec((None, 1, d), lambda i, be: (be[i], 0, 0))],
        out_specs=pl.BlockSpec((rows, d), lambda i, be: (i, 0)),
        scratch_shapes=[pltpu.VMEM((2, rows, d), F32), pltpu.VMEM((d, ff2), BF16), pltpu.VMEM((ff, d), BF16),
                        pltpu.SemaphoreType.DMA((2,))],
    )
    return pl.pallas_call(
        _experts_kernel,
        grid_spec=grid_spec,
        out_shape=jax.ShapeDtypeStruct((n_blk * rows, d), F32),
        compiler_params=_cparams("arbitrary"),
        name="experts",
    )(block_e, slot_tok, slot_tok, h2, wgu, bgu, wdn, bdn)


def _combine_kernel(dest_ref, destn_ref, ys_hbm, xn_ref, route_ref, g2_ref, o_ref, ybuf, sem):
    i = pl.program_id(0)
    n = pl.num_programs(0)
    tc = xn_ref.shape[0]
    slot = i % 2

    @pl.when(i == 0)
    def _():
        _gather_rows(dest_ref, TOP_K * tc, ys_hbm, ybuf.at[0], sem.at[0])

    @pl.when(i + 1 < n)
    def _():
        _gather_rows(destn_ref, TOP_K * tc, ys_hbm, ybuf.at[1 - slot], sem.at[1 - slot])

    _wait_rows(ys_hbm, ybuf.at[slot], sem.at[slot])
    route = route_ref[...]
    gates = [route[:, TOP_K + k:TOP_K + k + 1] for k in range(TOP_K)]
    acc = gates[0] * ybuf[slot, pl.ds(0, tc), :]
    for k in range(1, TOP_K):
        acc = acc + gates[k] * ybuf[slot, pl.ds(k * tc, tc), :]
    o_ref[...] = xn_ref[...] + g2_ref[...] * acc


def _combine(dest_kmaj, ys, xn, route, g2, tc):
    t, d = xn.shape
    n = t // tc
    per_batch = t // g2.shape[0] // tc
    return pl.pallas_call(
        _combine_kernel,
        grid=(n,),
        in_specs=[pl.BlockSpec((None, 1, TOP_K * tc), lambda i: (i, 0, 0), memory_space=pltpu.SMEM),
                  pl.BlockSpec((None, 1, TOP_K * tc), lambda i: (jnp.minimum(i + 1, n - 1), 0, 0),
                               memory_space=pltpu.SMEM),
                  pl.BlockSpec(memory_space=pl.ANY),
                  pl.BlockSpec((tc, d), lambda i: (i, 0)),
                  pl.BlockSpec((tc, LANES), lambda i: (i, 0)),
                  pl.BlockSpec((None, 1, d), lambda i: (i // per_batch, 0, 0))],
        out_specs=pl.BlockSpec((tc, d), lambda i: (i, 0)),
        out_shape=jax.ShapeDtypeStruct((t, d), F32),
        scratch_shapes=[pltpu.VMEM((2, TOP_K * tc, d), F32), pltpu.SemaphoreType.DMA((2,))],
        compiler_params=_cparams("arbitrary"),
        name="combine",
    )(dest_kmaj, dest_kmaj, ys, xn, route, g2)


def _rope_tables(l):
    n_freq = HEAD_DIM // 4
    inv_freq = ROPE_BASE ** (-np.arange(n_freq, dtype=np.float64) / n_freq)
    tpos = np.arange(l)
    lane = np.arange(LANES) % HEAD_DIM
    pos = np.where(lane[None, :] < HEAD_DIM // 2, (tpos // GRID_W)[:, None], (tpos % GRID_W)[:, None])
    ang = pos * inv_freq[lane % n_freq][None, :]
    sign = np.where(lane % (2 * n_freq) < n_freq, -1.0, 1.0)[None, :]
    return jnp.asarray(np.cos(ang), F32), jnp.asarray(np.sin(ang) * sign, F32)


def _filter_features(l, width):
    t = np.linspace(0.0, 1.0, l)[:, None]
    w = 2.0 * math.pi * np.arange(l)[:, None] / l
    bands = np.linspace(1e-4, POS_BANDS - 1, POS_BANDS)[None, :]
    z = np.concatenate([t, np.cos(bands * w), -np.sin(bands * w)], axis=-1)
    return jnp.asarray(np.pad(z, ((0, 0), (0, width - z.shape[1]))), F32)


def kernel(x, c, ctx, c_ctx, w_mod, b_mod, norm1, norm2, w_in, q_norm, k_norm, sink, conv_w, conv_b,
           filt_w1, filt_b1, filt_w2, filt_b2, filt_w3, filt_b3, filt_w4, filt_freq, filt_bias,
           attn_out_norm, hyena_out_norm, w_out, w_router, b_router, w_gu, b_gu, w_down, b_down):
    assert w_mod.shape[0] == 1, "single-layer configuration"
    b, l, d = x.shape
    t = b * l
    aw = N_Q_HEADS * HEAD_DIM
    kvw = N_KV_HEADS * HEAD_DIM
    hw = conv_w.shape[-1] // 3
    tl = min(512, l)

    ctx_row = b
    pad_rows = -(b + 1) % SUBLANES
    c_all = jnp.concatenate([c, c_ctx[None], jnp.zeros((pad_rows, d), F32)], axis=0)
    mod = _adaln(c_all, w_mod[0], b_mod[0])

    w = w_in[0]
    wq, wk, wv, wu = w[:, :aw], w[:, aw:aw + kvw], w[:, aw + kvw:aw + 2 * kvw], w[:, aw + 2 * kvw:]
    dup = lambda m: jnp.concatenate([m[:, h * HEAD_DIM:(h + 1) * HEAD_DIM]
                                     for h in range(N_KV_HEADS) for _ in range(2)], axis=1)
    w_all = jnp.concatenate([wq, dup(wk), dup(wv), wu], axis=1).astype(BF16)
    w_kv = jnp.concatenate([dup(wk), dup(wv)], axis=1).astype(BF16)
    gq = jnp.tile(q_norm[0], N_Q_HEADS)[None]
    gk = jnp.tile(k_norm[0], 2 * N_KV_HEADS)[None]
    bd = jnp.asarray(np.kron(np.eye(N_Q_HEADS), np.full((HEAD_DIM, HEAD_DIM), 1.0 / HEAD_DIM)), BF16)
    cos_t, sin_t = _rope_tables(l)

    q, k, v, u = _inproj(x, mod, norm1, w_all, gq, gk, bd, cos_t, sin_t, tl)
    kx, vx = _ctxkv(ctx, mod, ctx_row, norm1, w_kv, gk, bd[:2 * kvw, :2 * kvw])
    an = _attention(sink[0], q, k, v, kx, vx, attn_out_norm)

    x0, z, zn = _hconv(u, conv_w[0], conv_b)
    ffn = filt_w2.shape[-1]
    zf = _filter_features(l, ffn)
    w1 = jnp.pad(filt_w1[0], ((0, ffn - POS_EMB_DIM), (0, 0)))
    deltas = jnp.asarray(np.linspace(MIN_DECAY, MAX_DECAY, hw)[None, :], F32)
    kp, km, kn = _filter(zf, w1, filt_b1, filt_w2[0], filt_b2, filt_w3[0], filt_b3, filt_freq,
                         filt_w4[0], deltas)
    cm, sm = _dftgen(l, tl)
    kr, ks = _kspec(cm, sm, kp, km, tl)
    fa, fb = _hfwd(cm, sm, z, kr, ks, tl)
    yn = _hinv(cm, sm, fa, fb, x0, z, zn, kn, filt_bias, hyena_out_norm, tl)

    wr = jnp.pad(w_router[0], ((0, 0), (0, LANES - N_EXPERTS)))
    br = jnp.concatenate([b_router[0], jnp.full((LANES - N_EXPERTS,), NEG_INF, F32)])[None]
    tri = jnp.asarray(np.tril(np.ones((tl, tl)), -1), BF16)
    xn, h2, route, cnt = _mixout(an, yn, x, mod, w_out[0].astype(BF16), norm2, wr, br, tri, tl)

    rows = EXPERT_ROWS
    a_tot = t * TOP_K
    n_blk = -(-a_tot // rows) + N_EXPERTS
    idx = route[:, 0:TOP_K].astype(jnp.int32)
    pos = route[:, 2 * TOP_K:3 * TOP_K].astype(jnp.int32)
    counts = cnt[0, :N_EXPERTS].astype(jnp.int32)
    pcounts = (counts + rows - 1) // rows * rows
    pends = jnp.cumsum(pcounts)
    pstarts = pends - pcounts
    dest = pstarts[idx] + pos
    block_start = jnp.arange(n_blk, dtype=jnp.int32) * rows
    block_e = jnp.minimum(jnp.sum(pends[None, :] <= block_start[:, None], axis=1), N_EXPERTS - 1).astype(jnp.int32)
    order = jnp.argsort(idx.reshape(-1), stable=True).astype(jnp.int32)
    slot_e = jnp.repeat(block_e, rows)
    offset = jnp.arange(n_blk * rows, dtype=jnp.int32) - pstarts[slot_e]
    starts = jnp.cumsum(counts) - counts
    src = jnp.clip(starts[slot_e] + jnp.minimum(offset, counts[slot_e] - 1), 0, a_tot - 1)
    slot_tok = order[src] // TOP_K

    ys = _experts(block_e, slot_tok.reshape(n_blk, 1, rows), h2,
                  w_gu[0], b_gu[0][:, None, :], w_down[0], b_down[0][:, None, :], rows)

    tc = min(COMBINE_ROWS, l)
    dest_kmaj = dest.reshape(t // tc, tc, TOP_K).transpose(0, 2, 1).reshape(t // tc, 1, TOP_K * tc)
    g2 = mod[:b, None, 5 * d:6 * d]
    out = _combine(dest_kmaj, ys, xn.reshape(t, d), route, g2, tc)
    return out.reshape(b, l, d)
```

```python
import functools
import math

import numpy as np
import jax
import jax.numpy as jnp
from jax import lax
from jax.experimental import pallas as pl
from jax.experimental.pallas import tpu as pltpu

F32 = jnp.float32
BF16 = jnp.bfloat16

LANES = 128
SUBLANES = 8
VMEM_LIMIT = 56 * 1024 * 1024

HEAD_DIM = 64
N_Q_HEADS = 8
N_KV_HEADS = 2
GROUP = N_Q_HEADS // N_KV_HEADS
GRID_W = 64
WINDOW = 128
ROPE_BASE = 10000.0
ATTN_SCALE = HEAD_DIM ** -0.5
POS_EMB_DIM = 33
POS_BANDS = (POS_EMB_DIM - 1) // 2
DECAY_TARGET = 1e-2
MAX_DECAY = -math.log(DECAY_TARGET) / 0.3
MIN_DECAY = -math.log(DECAY_TARGET) / 1.5
N_EXPERTS = 32
TOP_K = 4
SWIGLU_LIMIT = 7.0
SWIGLU_ALPHA = 1.702
EPS = 1e-6
NEG_INF = -1e30

EXPERT_ROWS = 256
COMBINE_ROWS = 256


def _cparams(*sem):
    return pltpu.CompilerParams(dimension_semantics=sem, vmem_limit_bytes=VMEM_LIMIT)


def _split(a):
    hi = a.astype(BF16)
    lo = (a - hi.astype(F32)).astype(BF16)
    return hi, lo


def _dot(a, b):
    return jnp.dot(a, b, preferred_element_type=F32)


def _dot3(a, b):
    ah, al = _split(a)
    bh, bl = _split(b)
    return _dot(ah, bh) + _dot(al, bh) + _dot(ah, bl)


def _rms(x):
    return lax.rsqrt(jnp.mean(x * x, axis=-1, keepdims=True) + EPS)


def _adaln_kernel(c_ref, w_ref, b_ref, o_ref):
    c = c_ref[...]
    o_ref[...] = _dot3(c * jax.nn.sigmoid(c), w_ref[...]) + b_ref[...]


def _adaln(c_all, w_mod, b_mod):
    rows, d = c_all.shape
    n = w_mod.shape[1]
    tn = 1024
    return pl.pallas_call(
        _adaln_kernel,
        grid=(n // tn,),
        in_specs=[pl.BlockSpec((rows, d), lambda j: (0, 0)),
                  pl.BlockSpec((d, tn), lambda j: (0, j)),
                  pl.BlockSpec((1, tn), lambda j: (0, j))],
        out_specs=pl.BlockSpec((rows, tn), lambda j: (0, j)),
        out_shape=jax.ShapeDtypeStruct((rows, n), F32),
        compiler_params=_cparams("arbitrary"),
        name="adaln",
    )(c_all, w_mod, b_mod[None])


def _head_rms(x, bd):
    hi, lo = _split(x * x)
    return x * lax.rsqrt(_dot(hi, bd) + _dot(lo, bd) + EPS)


def _rope128(x, cos, sin):
    lane = lax.broadcasted_iota(jnp.int32, x.shape, 1)
    partner = jnp.where(lane % 32 < 16, pltpu.roll(x, LANES - 16, 1), pltpu.roll(x, 16, 1))
    return x * cos + partner * sin


def _modulated(x, mod_ref, row, norm_ref, d):
    sh = mod_ref[pl.ds(row, 1), 0:d]
    sc = mod_ref[pl.ds(row, 1), d:2 * d]
    return (x * _rms(x)) * norm_ref[...] * (1 + sc) + sh


def _inproj_kernel(x_ref, mod_ref, n1_ref, w_ref, gq_ref, gk_ref, bd_ref, cos_ref, sin_ref,
                   q_ref, k_ref, v_ref, u_ref):
    d = x_ref.shape[-1]
    aw = q_ref.shape[-1]
    kw = k_ref.shape[-1]
    hb = _modulated(x_ref[...], mod_ref, pl.program_id(1), n1_ref, d).astype(BF16)
    cos = cos_ref[...]
    sin = sin_ref[...]
    bd = bd_ref[...]
    q = _head_rms(_dot(hb, w_ref[:, 0:aw]), bd) * gq_ref[...]
    for c in range(aw // LANES):
        sl = slice(c * LANES, (c + 1) * LANES)
        q_ref[:, sl] = (_rope128(q[:, sl], cos, sin) * ATTN_SCALE).astype(BF16)
    k = _head_rms(_dot(hb, w_ref[:, aw:aw + kw]), bd[0:kw, 0:kw]) * gk_ref[...]
    for c in range(kw // LANES):
        sl = slice(c * LANES, (c + 1) * LANES)
        k_ref[:, sl] = _rope128(k[:, sl], cos, sin).astype(BF16)
    v_ref[...] = _dot(hb, w_ref[:, aw + kw:aw + 2 * kw]).astype(BF16)
    u_ref[...] = _dot(hb, w_ref[:, aw + 2 * kw:])


def _inproj(x, mod, norm1, w_all, gq, gk, bd, cos_t, sin_t, tl):
    b, l, d = x.shape
    aw, kw = gq.shape[1], gk.shape[1]
    uw = w_all.shape[1] - aw - 2 * kw
    const = lambda i, j: (0, 0)
    tok = lambda i, j: (j, i, 0)
    return pl.pallas_call(
        _inproj_kernel,
        grid=(l // tl, b),
        in_specs=[pl.BlockSpec((None, tl, d), tok),
                  pl.BlockSpec(mod.shape, const),
                  pl.BlockSpec((1, d), const),
                  pl.BlockSpec(w_all.shape, const),
                  pl.BlockSpec((1, aw), const),
                  pl.BlockSpec((1, kw), const),
                  pl.BlockSpec(bd.shape, const),
                  pl.BlockSpec((tl, LANES), lambda i, j: (i, 0)),
                  pl.BlockSpec((tl, LANES), lambda i, j: (i, 0))],
        out_specs=[pl.BlockSpec((None, tl, aw), tok),
                   pl.BlockSpec((None, tl, kw), tok),
                   pl.BlockSpec((None, tl, kw), tok),
                   pl.BlockSpec((None, tl, uw), tok)],
        out_shape=[jax.ShapeDtypeStruct((b, l, aw), BF16),
                   jax.ShapeDtypeStruct((b, l, kw), BF16),
                   jax.ShapeDtypeStruct((b, l, kw), BF16),
                   jax.ShapeDtypeStruct((b, l, uw), F32)],
        compiler_params=_cparams("arbitrary", "arbitrary"),
        name="inproj",
    )(x, mod, norm1, w_all, gq, gk, bd, cos_t, sin_t)


def _ctxkv_kernel(row, x_ref, mod_ref, n1_ref, w_ref, gk_ref, bd_ref, k_ref, v_ref):
    d = x_ref.shape[-1]
    kw = k_ref.shape[-1]
    hb = _modulated(x_ref[...], mod_ref, row, n1_ref, d).astype(BF16)
    k = _head_rms(_dot(hb, w_ref[:, 0:kw]), bd_ref[...]) * gk_ref[...]
    k_ref[...] = k.astype(BF16)
    v_ref[...] = _dot(hb, w_ref[:, kw:]).astype(BF16)


def _ctxkv(ctx, mod, ctx_row, norm1, w_kv, gk, bd):
    b, lc, d = ctx.shape
    kw = gk.shape[1]
    const = lambda i: (0, 0)
    tok = lambda i: (i, 0, 0)
    return pl.pallas_call(
        functools.partial(_ctxkv_kernel, ctx_row),
        grid=(b,),
        in_specs=[pl.BlockSpec((None, lc, d), tok),
                  pl.BlockSpec(mod.shape, const),
                  pl.BlockSpec((1, d), const),
                  pl.BlockSpec(w_kv.shape, const),
                  pl.BlockSpec((1, kw), const),
                  pl.BlockSpec(bd.shape, const)],
        out_specs=[pl.BlockSpec((None, lc, kw), tok), pl.BlockSpec((None, lc, kw), tok)],
        out_shape=[jax.ShapeDtypeStruct((b, lc, kw), BF16)] * 2,
        compiler_params=_cparams("arbitrary"),
        name="ctxkv",
    )(ctx, mod, norm1, w_kv, gk, bd)


def _attn_kernel(sink_ref, q_ref, kp_ref, kc_ref, kn_ref, vp_ref, vc_ref, vn_ref, kx_ref, vx_ref,
                 ga_ref, o_ref, acc_ref):
    i = pl.program_id(1)
    nb = pl.num_programs(1)
    tq = q_ref.shape[0]
    lc = kx_ref.shape[0]
    nk = 3 * tq + lc
    pairs = GROUP // 2
    rows = pairs * tq
    r = lax.broadcasted_iota(jnp.int32, (rows, tq), 0) % tq
    j = lax.broadcasted_iota(jnp.int32, (rows, tq), 1)
    ok_prev = j >= r + jnp.where(i > 0, 0, tq)
    ok_next = j <= r - jnp.where(i < nb - 1, 0, tq)
    lo = lax.broadcasted_iota(jnp.int32, (nk, LANES), 1) < HEAD_DIM
    zero = jnp.zeros((nk, LANES), BF16)
    top = lax.broadcasted_iota(jnp.int32, (rows, 1), 0) < tq

    ks, vs, qs, sinks = [], [], [], []
    for h in range(N_KV_HEADS):
        hs = slice(h * LANES, (h + 1) * LANES)
        kcat = jnp.concatenate([kp_ref[:, hs], kc_ref[:, hs], kn_ref[:, hs], kx_ref[:, hs]], axis=0)
        vcat = jnp.concatenate([vp_ref[:, hs], vc_ref[:, hs], vn_ref[:, hs], vx_ref[:, hs]], axis=0)
        q2 = jnp.concatenate([q_ref[:, (h * pairs + p) * LANES:(h * pairs + p + 1) * LANES]
                              for p in range(pairs)], axis=0)
        for half in range(2):
            ks.append(jnp.where(lo, kcat, zero) if half == 0 else jnp.where(lo, zero, kcat))
            vs.append(jnp.where(lo, vcat, zero) if half == 0 else jnp.where(lo, zero, vcat))
            qs.append(q2)
            sinks.append(jnp.where(top, sink_ref[h * GROUP + half], sink_ref[h * GROUP + 2 + half]))
    ss = [lax.dot_general(qq, kk, (((1,), (1,)), ((), ())), preferred_element_type=F32)
          for qq, kk in zip(qs, ks)]
    ss = [jnp.concatenate([jnp.where(ok_prev, s[:, 0:tq], NEG_INF), s[:, tq:2 * tq],
                           jnp.where(ok_next, s[:, 2 * tq:3 * tq], NEG_INF), s[:, 3 * tq:]], axis=1)
          for s in ss]
    ms = [jnp.maximum(jnp.max(s, axis=-1, keepdims=True), sk) for s, sk in zip(ss, sinks)]
    es = [jnp.exp(s - m) for s, m in zip(ss, ms)]
    dens = [jnp.sum(e, axis=-1, keepdims=True) + jnp.exp(sk - m) for e, sk, m in zip(es, sinks, ms)]
    outs = [_dot(e.astype(BF16), vv) / den for e, vv, den in zip(es, vs, dens)]
    for h in range(N_KV_HEADS):
        both = outs[2 * h] + outs[2 * h + 1]
        for p in range(pairs):
            acc_ref[:, (h * pairs + p) * LANES:(h * pairs + p + 1) * LANES] = both[p * tq:(p + 1) * tq]
    a = acc_ref[...]
    o_ref[...] = (a * _rms(a) * ga_ref[...]).astype(BF16)


def _attention(sink, q, k, v, kx, vx, ga):
    b, l, aw = q.shape
    kw = k.shape[-1]
    lc = kx.shape[1]
    tq = WINDOW
    nb = l // tq
    cur = lambda bi, i: (bi, i, 0)
    prev = lambda bi, i: (bi, jnp.maximum(i - 1, 0), 0)
    nxt = lambda bi, i: (bi, jnp.minimum(i + 1, nb - 1), 0)
    ctx = lambda bi, i: (bi, 0, 0)
    kv = lambda m: pl.BlockSpec((None, tq, kw), m)
    return pl.pallas_call(
        _attn_kernel,
        grid=(b, nb),
        in_specs=[pl.BlockSpec(memory_space=pltpu.SMEM),
                  pl.BlockSpec((None, tq, aw), cur),
                  kv(prev), kv(cur), kv(nxt), kv(prev), kv(cur), kv(nxt),
                  pl.BlockSpec((None, lc, kw), ctx), pl.BlockSpec((None, lc, kw), ctx),
                  pl.BlockSpec((1, aw), lambda bi, i: (0, 0))],
        out_specs=pl.BlockSpec((None, tq, aw), cur),
        out_shape=jax.ShapeDtypeStruct((b, l, aw), BF16),
        scratch_shapes=[pltpu.VMEM((tq, aw), F32)],
        compiler_params=_cparams("arbitrary", "arbitrary"),
        name="attn",
    )(sink, q, k, k, k, v, v, v, kx, vx, ga)


def _hconv_kernel(u0_ref, u1_ref, u2_ref, w0_ref, w1_ref, w2_ref, b0_ref, b1_ref, b2_ref,
                  x0_ref, z_ref, zn_ref):
    l, cb = u0_ref.shape
    row = lax.broadcasted_iota(jnp.int32, (l, cb), 0)

    def conv(u_ref, w_ref, b_ref):
        u = u_ref[...]
        before = jnp.where(row == 0, 0.0, pltpu.roll(u, 1, 0))
        after = jnp.where(row == l - 1, 0.0, pltpu.roll(u, l - 1, 0))
        return b_ref[...] + before * w_ref[0:1, :] + u * w_ref[1:2, :] + after * w_ref[2:3, :]

    x0_ref[...] = conv(u0_ref, w0_ref, b0_ref)
    z = conv(u1_ref, w1_ref, b1_ref) * conv(u2_ref, w2_ref, b2_ref)
    z_ref[...] = z.astype(BF16)
    sign = (1 - 2 * (row % 2)).astype(F32)
    zn_ref[...] = jnp.sum(z * sign, axis=0, keepdims=True)


def _hconv(u, conv_w, conv_b, cb=LANES):
    b, l, w3 = u.shape
    c = w3 // 3
    n = c // cb
    us = [pl.BlockSpec((None, l, cb), lambda bi, j, g=g: (bi, 0, g * n + j)) for g in range(3)]
    ws = [pl.BlockSpec((3, cb), lambda bi, j, g=g: (0, g * n + j)) for g in range(3)]
    bs = [pl.BlockSpec((1, cb), lambda bi, j, g=g: (0, g * n + j)) for g in range(3)]
    out = lambda bi, j: (bi, 0, j)
    return pl.pallas_call(
        _hconv_kernel,
        grid=(b, n),
        in_specs=us + ws + bs,
        out_specs=[pl.BlockSpec((None, l, cb), out), pl.BlockSpec((None, l, cb), out),
                   pl.BlockSpec((None, 1, cb), out)],
        out_shape=[jax.ShapeDtypeStruct((b, l, c), F32), jax.ShapeDtypeStruct((b, l, c), BF16),
                   jax.ShapeDtypeStruct((b, 1, c), F32)],
        compiler_params=_cparams("arbitrary", "arbitrary"),
        name="hconv",
    )(u, u, u, conv_w, conv_w, conv_w, conv_b, conv_b, conv_b)


def _filter_kernel(z_ref, w1_ref, b1_ref, w2_ref, b2_ref, w3_ref, b3_ref, fr_ref, w4f_ref, w4b_ref,
                   dl_ref, kp_ref, km_ref, kn_ref, h_scr):
    l, cf = kp_ref.shape

    @pl.when(pl.program_id(0) == 0)
    def _():
        fr = fr_ref[...]
        h = jnp.sin(fr * (_dot3(z_ref[...], w1_ref[...]) + b1_ref[...]))
        h = jnp.sin(fr * (_dot3(h, w2_ref[...]) + b2_ref[...]))
        h_scr[...] = jnp.sin(fr * (_dot3(h, w3_ref[...]) + b3_ref[...]))

    h = h_scr[...]
    row = lax.broadcasted_iota(jnp.int32, (l, cf), 0)
    t = row.astype(F32) / (l - 1)
    decay = jnp.exp(-t * dl_ref[...])
    kf = _dot3(h, w4f_ref[...]) * decay
    kb = jnp.where(row == 0, 0.0, _dot3(h, w4b_ref[...]) * decay)
    nrm = lax.rsqrt(jnp.sum(kf * kf + kb * kb, axis=0, keepdims=True) + EPS)
    kp = (kf + kb) * nrm
    kp_ref[...] = kp.astype(BF16)
    km_ref[...] = ((kf - kb) * nrm).astype(BF16)
    sign = (1 - 2 * (row % 2)).astype(F32)
    kn_ref[...] = jnp.sum(kp * sign, axis=0, keepdims=True)


def _filter(zf, w1, b1, w2, b2, w3, b3, fr, w4, deltas, cf=LANES):
    l, zw = zf.shape
    ffn = w2.shape[0]
    c = w4.shape[1] // 2
    n = c // cf
    const = lambda j: (0, 0)
    vec = pl.BlockSpec((1, ffn), const)
    mat = pl.BlockSpec((ffn, ffn), const)
    return pl.pallas_call(
        _filter_kernel,
        grid=(n,),
        in_specs=[pl.BlockSpec((l, zw), const), pl.BlockSpec((zw, ffn), const), vec, mat, vec, mat, vec, vec,
                  pl.BlockSpec((ffn, cf), lambda j: (0, j)),
                  pl.BlockSpec((ffn, cf), lambda j: (0, n + j)),
                  pl.BlockSpec((1, cf), lambda j: (0, j))],
        out_specs=[pl.BlockSpec((l, cf), lambda j: (0, j)), pl.BlockSpec((l, cf), lambda j: (0, j)),
                   pl.BlockSpec((1, cf), lambda j: (0, j))],
        out_shape=[jax.ShapeDtypeStruct((l, c), BF16), jax.ShapeDtypeStruct((l, c), BF16),
                   jax.ShapeDtypeStruct((1, c), F32)],
        scratch_shapes=[pltpu.VMEM((l, ffn), F32)],
        compiler_params=_cparams("arbitrary"),
        name="filt",
    )(zf, w1, b1, w2, b2, w3, b3, fr, w4, w4, deltas)


DFT_FINE = 64


def _dftgen_kernel(ca_ref, sa_ref, cb_ref, sb_ref, c_ref, s_ref):
    cb = cb_ref[...]
    sb = sb_ref[...]
    for a in range(ca_ref.shape[0]):
        ca = ca_ref[a:a + 1, :]
        sa = sa_ref[a:a + 1, :]
        rs = slice(a * DFT_FINE, (a + 1) * DFT_FINE)
        c_ref[rs, :] = (ca * cb - sa * sb).astype(BF16)
        s_ref[rs, :] = (sa * cb + ca * sb).astype(BF16)


def _dftgen(l, tf):
    t = np.arange(l, dtype=np.int64)[None, :]
    coarse = (np.arange(l // DFT_FINE, dtype=np.int64)[:, None] * DFT_FINE * t) % (2 * l)
    fine = (np.arange(DFT_FINE, dtype=np.int64)[:, None] * t) % (2 * l)
    tabs = [jnp.asarray(fn(ang * (math.pi / l)), F32) for ang in (coarse, fine) for fn in (np.cos, np.sin)]
    na = tf // DFT_FINE
    return pl.pallas_call(
        _dftgen_kernel,
        grid=(l // tf,),
        in_specs=[pl.BlockSpec((na, l), lambda i: (i, 0)), pl.BlockSpec((na, l), lambda i: (i, 0)),
                  pl.BlockSpec((DFT_FINE, l), lambda i: (0, 0)), pl.BlockSpec((DFT_FINE, l), lambda i: (0, 0))],
        out_specs=[pl.BlockSpec((tf, l), lambda i: (i, 0)), pl.BlockSpec((tf, l), lambda i: (i, 0))],
        out_shape=[jax.ShapeDtypeStruct((l, l), BF16)] * 2,
        compiler_params=_cparams("arbitrary"),
        name="dftgen",
    )(*tabs)


def _kspec_kernel(n_fft, c_ref, s_ref, kp_ref, km_ref, kr_ref, ks_ref):
    tf = c_ref.shape[0]
    f = pl.program_id(0) * tf + lax.broadcasted_iota(jnp.int32, (tf, 1), 0)
    w = jnp.where(f == 0, 1.0 / n_fft, 2.0 / n_fft)
    kr_ref[...] = _dot(c_ref[...], kp_ref[...]) * w
    ks_ref[...] = _dot(s_ref[...], km_ref[...]) * w


def _kspec(cm, sm, kp, km, tf):
    l, c = kp.shape
    const = lambda i: (0, 0)
    tile = lambda i: (i, 0)
    return pl.pallas_call(
        functools.partial(_kspec_kernel, 2 * l),
        grid=(l // tf,),
        in_specs=[pl.BlockSpec((tf, l), tile), pl.BlockSpec((tf, l), tile),
                  pl.BlockSpec((l, c), const), pl.BlockSpec((l, c), const)],
        out_specs=[pl.BlockSpec((tf, c), tile), pl.BlockSpec((tf, c), tile)],
        out_shape=[jax.ShapeDtypeStruct((l, c), F32)] * 2,
        compiler_params=_cparams("arbitrary"),
        name="kspec",
    )(cm, sm, kp, km)


def _hfwd_kernel(c_ref, s_ref, z_ref, kr_ref, ks_ref, a_ref, b_ref):
    z = z_ref[...]
    zr = _dot(c_ref[...], z)
    zs = _dot(s_ref[...], z)
    kr = kr_ref[...]
    ks = ks_ref[...]
    a_ref[...] = (zr * kr - zs * ks).astype(BF16)
    b_ref[...] = (zr * ks + zs * kr).astype(BF16)


def _hfwd(cm, sm, z, kr, ks, tf):
    b, l, c = z.shape
    tile = lambda i, bi: (i, 0)
    out = lambda i, bi: (bi, i, 0)
    return pl.pallas_call(
        _hfwd_kernel,
        grid=(l // tf, b),
        in_specs=[pl.BlockSpec((tf, l), tile), pl.BlockSpec((tf, l), tile),
                  pl.BlockSpec((None, l, c), lambda i, bi: (bi, 0, 0)),
                  pl.BlockSpec((tf, c), tile), pl.BlockSpec((tf, c), tile)],
        out_specs=[pl.BlockSpec((None, tf, c), out), pl.BlockSpec((None, tf, c), out)],
        out_shape=[jax.ShapeDtypeStruct((b, l, c), BF16)] * 2,
        compiler_params=_cparams("arbitrary", "arbitrary"),
        name="hfwd",
    )(cm, sm, z, kr, ks)


def _hinv_kernel(c_ref, s_ref, a_ref, b_ref, x0_ref, z_ref, zn_ref, kn_ref, bias_ref, g_ref, o_ref):
    tt = c_ref.shape[0]
    n_fft = 2 * c_ref.shape[1]
    y = _dot(c_ref[...], a_ref[...]) + _dot(s_ref[...], b_ref[...])
    t = pl.program_id(0) * tt + lax.broadcasted_iota(jnp.int32, (tt, 1), 0)
    sign = (1 - 2 * (t % 2)).astype(F32)
    y = y + sign * (zn_ref[...] * kn_ref[...] * (1.0 / n_fft))
    hy = x0_ref[...] * (y + z_ref[...].astype(F32) * bias_ref[...])
    o_ref[...] = (hy * _rms(hy) * g_ref[...]).astype(BF16)


def _hinv(cm, sm, a, bm, x0, z, zn, kn, bias, gain, tt):
    b, l, c = z.shape
    tile = lambda i, bi: (i, 0)
    full = lambda i, bi: (bi, 0, 0)
    tok = lambda i, bi: (bi, i, 0)
    const = lambda i, bi: (0, 0)
    return pl.pallas_call(
        _hinv_kernel,
        grid=(l // tt, b),
        in_specs=[pl.BlockSpec((tt, l), tile), pl.BlockSpec((tt, l), tile),
                  pl.BlockSpec((None, l, c), full), pl.BlockSpec((None, l, c), full),
                  pl.BlockSpec((None, tt, c), tok), pl.BlockSpec((None, tt, c), tok),
                  pl.BlockSpec((None, 1, c), full),
                  pl.BlockSpec((1, c), const), pl.BlockSpec((1, c), const), pl.BlockSpec((1, c), const)],
        out_specs=pl.BlockSpec((None, tt, c), tok),
        out_shape=jax.ShapeDtypeStruct((b, l, c), BF16),
        compiler_params=_cparams("arbitrary", "arbitrary"),
        name="hinv",
    )(cm, sm, a, bm, x0, z, zn, kn, bias, gain)


def _mixout_kernel(an_ref, yn_ref, x_ref, mod_ref, wo_ref, n2_ref, wr_ref, br_ref, tri_ref,
                   xn_ref, h2_ref, route_ref, cnt_ref, carry):
    bi = pl.program_id(0)
    d = x_ref.shape[-1]
    half = an_ref.shape[-1]
    tl = x_ref.shape[0]

    @pl.when((bi == 0) & (pl.program_id(1) == 0))
    def _():
        carry[...] = jnp.zeros_like(carry)

    mix = _dot(an_ref[...], wo_ref[0:half, :]) + _dot(yn_ref[...], wo_ref[half:, :])
    g1 = mod_ref[pl.ds(bi, 1), 2 * d:3 * d]
    sh2 = mod_ref[pl.ds(bi, 1), 3 * d:4 * d]
    sc2 = mod_ref[pl.ds(bi, 1), 4 * d:5 * d]
    xn = x_ref[...] + g1 * mix
    xn_ref[...] = xn
    h2 = (xn * _rms(xn)) * n2_ref[...] * (1 + sc2) + sh2
    for j in range(d // LANES):
        _token_chunk(h2_ref, 0, tl, j)[...] = h2[:, j * LANES:(j + 1) * LANES]

    logits = _dot3(h2, wr_ref[...]) + br_ref[...]
    lane = lax.broadcasted_iota(jnp.int32, (tl, LANES), 1).astype(F32)
    vals, idxs, sels = [], [], []
    cur = logits
    for _ in range(TOP_K):
        m = jnp.max(cur, axis=-1, keepdims=True)
        idx = jnp.min(jnp.where(cur == m, lane, float(LANES)), axis=-1, keepdims=True)
        sel = lane == idx
        vals.append(m)
        idxs.append(idx)
        sels.append(sel)
        cur = jnp.where(sel, -jnp.inf, cur)
    es = [jnp.exp(v - vals[0]) for v in vals]
    den = es[0] + es[1] + es[2] + es[3]
    hot = sum(s.astype(F32) for s in sels)
    before = _dot(tri_ref[...], hot.astype(BF16)) + carry[...]
    carry[...] = carry[...] + jnp.sum(hot, axis=0, keepdims=True)
    cnt_ref[...] = carry[...]
    route = jnp.zeros((tl, LANES), F32)
    for k in range(TOP_K):
        pos = jnp.sum(jnp.where(sels[k], before, 0.0), axis=-1, keepdims=True)
        route = jnp.where(lane == k, idxs[k], route)
        route = jnp.where(lane == TOP_K + k, es[k] / den, route)
        route = jnp.where(lane == 2 * TOP_K + k, pos, route)
    route_ref[...] = route


def _mixout(an, yn, x, mod, w_out, norm2, wr, br, tri, tl):
    b, l, d = x.shape
    half = an.shape[-1]
    nt = l // tl
    tok = lambda bi, i: (bi, i, 0)
    const = lambda bi, i: (0, 0)
    flat = lambda bi, i: (bi * nt + i, 0)
    return pl.pallas_call(
        _mixout_kernel,
        grid=(b, nt),
        in_specs=[pl.BlockSpec((None, tl, half), tok), pl.BlockSpec((None, tl, half), tok),
                  pl.BlockSpec((None, tl, d), tok),
                  pl.BlockSpec(mod.shape, const), pl.BlockSpec(w_out.shape, const),
                  pl.BlockSpec((1, d), const), pl.BlockSpec(wr.shape, const),
                  pl.BlockSpec((1, LANES), const), pl.BlockSpec((tl, tl), const)],
        out_specs=[pl.BlockSpec((None, tl, d), tok),
                   pl.BlockSpec((tl * SUBLANES, LANES), flat),
                   pl.BlockSpec((tl, LANES), flat),
                   pl.BlockSpec((1, LANES), const)],
        out_shape=[jax.ShapeDtypeStruct((b, l, d), F32),
                   jax.ShapeDtypeStruct((b * l * SUBLANES, LANES), F32),
                   jax.ShapeDtypeStruct((b * l, LANES), F32),
                   jax.ShapeDtypeStruct((1, LANES), F32)],
        scratch_shapes=[pltpu.VMEM((1, LANES), F32)],
        compiler_params=_cparams("arbitrary", "arbitrary"),
        name="mixout",
    )(an, yn, x, mod, w_out, norm2, wr, br, tri)


def _token_chunk(ref, first_token, n, j):
    return ref.at[pl.ds(first_token * SUBLANES + j, n, stride=SUBLANES), :]


def _token_copy(idx_ref, r, src_hbm, dst, sem):
    first_row = lambda tok: tok * SUBLANES if isinstance(tok, int) else pl.multiple_of(tok * SUBLANES, SUBLANES)
    return pltpu.make_async_copy(src_hbm.at[pl.ds(first_row(idx_ref[0, r]), SUBLANES)],
                                 dst.at[pl.ds(first_row(r), SUBLANES)], sem)


def _gather_tokens(idx_ref, n, src_hbm, dst, sem):
    def body(r, carry):
        _token_copy(idx_ref, 2 * r, src_hbm, dst, sem).start(priority=0)
        _token_copy(idx_ref, 2 * r + 1, src_hbm, dst, sem).start(priority=1)
        return carry
    lax.fori_loop(0, n // 2, body, 0, unroll=4)


def _wait_tokens(src_hbm, dst, sem):
    pltpu.make_async_copy(src_hbm.at[pl.ds(0, dst.shape[0])], dst, sem).wait()


EXPERT_STAGES = 4
EXPERT_SLOTS = 3


def _experts_kernel(be_ref, tok0_ref, tok1_ref, tok2_ref, h2_hbm, wgu_ref, bgu_ref, wdn_ref, bdn_ref,
                    ys_ref, xbuf, wgu_bf, wdn_bf, sem):
    i = pl.program_id(0)
    n = pl.num_programs(0)
    rows = xbuf.shape[1] // SUBLANES
    nchunk = wgu_ref.shape[0] // LANES
    ff = wdn_ref.shape[0]
    fc = ff // EXPERT_STAGES
    rc = -(-rows // (EXPERT_STAGES - 1))
    slot = i % EXPERT_SLOTS
    nxt = (i + 1) % EXPERT_SLOTS
    ahead = (i + 2) % EXPERT_SLOTS

    @pl.when(i == 0)
    def _():
        _gather_tokens(tok0_ref, rows, h2_hbm, xbuf.at[0], sem.at[0])
        _gather_tokens(tok1_ref, rows, h2_hbm, xbuf.at[1], sem.at[1])

    @pl.when((i == 0) | (be_ref[i] != be_ref[jnp.maximum(i - 1, 0)]))
    def _():
        wgu_bf[...] = wgu_ref[...].astype(BF16)
        wdn_bf[...] = wdn_ref[...].astype(BF16)

    _wait_tokens(h2_hbm, xbuf.at[slot], sem.at[slot])
    acc = None
    for c in range(EXPERT_STAGES):
        xs = jnp.concatenate([_token_chunk(xbuf.at[slot], 0, rows, j)[...] for j in range(nchunk)],
                             axis=1).astype(BF16)
        for r in range(c * rc, min((c + 1) * rc, rows)):
            _token_copy(tok2_ref, r, h2_hbm, xbuf.at[ahead], sem.at[ahead]).start(priority=r % 2)
        cg = slice(c * fc, (c + 1) * fc)
        cu = slice(ff + c * fc, ff + (c + 1) * fc)
        gate = jnp.minimum(_dot(xs, wgu_bf[:, cg]) + bgu_ref[:, cg], SWIGLU_LIMIT)
        up = jnp.clip(_dot(xs, wgu_bf[:, cu]) + bgu_ref[:, cu], -SWIGLU_LIMIT, SWIGLU_LIMIT)
        act = ((up + 1) * (gate * jax.nn.sigmoid(SWIGLU_ALPHA * gate))).astype(BF16)
        part = _dot(act, wdn_bf[cg, :])
        acc = part if acc is None else acc + part
    y = acc + bdn_ref[...]
    for j in range(nchunk):
        _token_chunk(ys_ref, 0, rows, j)[...] = y[:, j * LANES:(j + 1) * LANES]

    @pl.when(i == n - 1)
    def _():
        _wait_tokens(h2_hbm, xbuf.at[nxt], sem.at[nxt])
        _wait_tokens(h2_hbm, xbuf.at[ahead], sem.at[ahead])


def _experts(block_e, slot_tok, h2, wgu, bgu, wdn, bdn, rows):
    n_blk = block_e.shape[0]
    d, ff2 = wgu.shape[1:]
    ff = wdn.shape[1]
    assert n_blk >= EXPERT_SLOTS
    tok = lambda k: pl.BlockSpec((None, 1, rows), lambda i, be: (jnp.minimum(i + k, n_blk - 1), 0, 0),
                                 memory_space=pltpu.SMEM)
    grid_spec = pltpu.PrefetchScalarGridSpec(
        num_scalar_prefetch=1,
        grid=(n_blk,),
        in_specs=[tok(0), tok(1), tok(2),
                  pl.BlockSpec(memory_space=pl.ANY),
                  pl.BlockSpec((None, d, ff2), lambda i, be: (be[i], 0, 0)),
                  pl.BlockSpec((None, 1, ff2), lambda i, be: (be[i], 0, 0)),
                  pl.BlockSpec((None, ff, d), lambda i, be: (be[i], 0, 0)),
                  pl.BlockSpec((None, 1, d), lambda i, be: (be[i], 0, 0))],
        out_specs=pl.BlockSpec((rows * SUBLANES, LANES), lambda i, be: (i, 0)),
        scratch_shapes=[pltpu.VMEM((EXPERT_SLOTS, rows * SUBLANES, LANES), F32),
                        pltpu.VMEM((d, ff2), BF16), pltpu.VMEM((ff, d), BF16),
                        pltpu.SemaphoreType.DMA((EXPERT_SLOTS,))],
    )
    return pl.pallas_call(
        _experts_kernel,
        grid_spec=grid_spec,
        out_shape=jax.ShapeDtypeStruct((n_blk * rows * SUBLANES, LANES), F32),
        compiler_params=_cparams("arbitrary"),
        name="experts",
    )(block_e, slot_tok, slot_tok, slot_tok, h2, wgu, bgu, wdn, bdn)


def _combine_kernel(dest_ref, destn_ref, ys_hbm, xn_ref, route_ref, g2_ref, o_ref, ybuf, sem):
    i = pl.program_id(0)
    n = pl.num_programs(0)
    tc = xn_ref.shape[0]
    slot = i % 2

    @pl.when(i == 0)
    def _():
        _gather_tokens(dest_ref, TOP_K * tc, ys_hbm, ybuf.at[0], sem.at[0])

    @pl.when(i + 1 < n)
    def _():
        _gather_tokens(destn_ref, TOP_K * tc, ys_hbm, ybuf.at[1 - slot], sem.at[1 - slot])

    _wait_tokens(ys_hbm, ybuf.at[slot], sem.at[slot])
    route = route_ref[...]
    gates = [jnp.broadcast_to(route[:, TOP_K + k:TOP_K + k + 1], (tc, LANES)) for k in range(TOP_K)]
    for j in range(xn_ref.shape[1] // LANES):
        cs = slice(j * LANES, (j + 1) * LANES)
        acc = gates[0] * _token_chunk(ybuf.at[slot], 0, tc, j)[...]
        for k in range(1, TOP_K):
            acc = acc + gates[k] * _token_chunk(ybuf.at[slot], k * tc, tc, j)[...]
        o_ref[:, cs] = xn_ref[:, cs] + g2_ref[:, cs] * acc


def _combine(dest_kmaj, ys, xn, route, g2, tc):
    t, d = xn.shape
    n = t // tc
    per_batch = t // g2.shape[0] // tc
    return pl.pallas_call(
        _combine_kernel,
        grid=(n,),
        in_specs=[pl.BlockSpec((None, 1, TOP_K * tc), lambda i: (i, 0, 0), memory_space=pltpu.SMEM),
                  pl.BlockSpec((None, 1, TOP_K * tc), lambda i: (jnp.minimum(i + 1, n - 1), 0, 0),
                               memory_space=pltpu.SMEM),
                  pl.BlockSpec(memory_space=pl.ANY),
                  pl.BlockSpec((tc, d), lambda i: (i, 0)),
                  pl.BlockSpec((tc, LANES), lambda i: (i, 0)),
                  pl.BlockSpec((None, 1, d), lambda i: (i // per_batch, 0, 0))],
        out_specs=pl.BlockSpec((tc, d), lambda i: (i, 0)),
        out_shape=jax.ShapeDtypeStruct((t, d), F32),
        scratch_shapes=[pltpu.VMEM((2, TOP_K * tc * SUBLANES, LANES), F32), pltpu.SemaphoreType.DMA((2,))],
        compiler_params=_cparams("arbitrary"),
        name="combine",
    )(dest_kmaj, dest_kmaj, ys, xn, route, g2)


def _rope_tables(l):
    n_freq = HEAD_DIM // 4
    inv_freq = ROPE_BASE ** (-np.arange(n_freq, dtype=np.float64) / n_freq)
    tpos = np.arange(l)
    lane = np.arange(LANES) % HEAD_DIM
    pos = np.where(lane[None, :] < HEAD_DIM // 2, (tpos // GRID_W)[:, None], (tpos % GRID_W)[:, None])
    ang = pos * inv_freq[lane % n_freq][None, :]
    sign = np.where(lane % (2 * n_freq) < n_freq, -1.0, 1.0)[None, :]
    return jnp.asarray(np.cos(ang), F32), jnp.asarray(np.sin(ang) * sign, F32)


def _filter_features(l, width):
    t = np.linspace(0.0, 1.0, l)[:, None]
    w = 2.0 * math.pi * np.arange(l)[:, None] / l
    bands = np.linspace(1e-4, POS_BANDS - 1, POS_BANDS)[None, :]
    z = np.concatenate([t, np.cos(bands * w), -np.sin(bands * w)], axis=-1)
    return jnp.asarray(np.pad(z, ((0, 0), (0, width - z.shape[1]))), F32)


def kernel(x, c, ctx, c_ctx, w_mod, b_mod, norm1, norm2, w_in, q_norm, k_norm, sink, conv_w, conv_b,
           filt_w1, filt_b1, filt_w2, filt_b2, filt_w3, filt_b3, filt_w4, filt_freq, filt_bias,
           attn_out_norm, hyena_out_norm, w_out, w_router, b_router, w_gu, b_gu, w_down, b_down):
    assert w_mod.shape[0] == 1, "single-layer configuration"
    b, l, d = x.shape
    assert d == SUBLANES * LANES, "token-tiled rows assume one (8, 128) tile per token"
    t = b * l
    aw = N_Q_HEADS * HEAD_DIM
    kvw = N_KV_HEADS * HEAD_DIM
    hw = conv_w.shape[-1] // 3
    tl = min(512, l)

    ctx_row = b
    pad_rows = -(b + 1) % SUBLANES
    c_all = jnp.concatenate([c, c_ctx[None], jnp.zeros((pad_rows, d), F32)], axis=0)
    mod = _adaln(c_all, w_mod[0], b_mod[0])

    w = w_in[0]
    wq, wk, wv, wu = w[:, :aw], w[:, aw:aw + kvw], w[:, aw + kvw:aw + 2 * kvw], w[:, aw + 2 * kvw:]
    dup = lambda m: jnp.concatenate([m[:, h * HEAD_DIM:(h + 1) * HEAD_DIM]
                                     for h in range(N_KV_HEADS) for _ in range(2)], axis=1)
    w_all = jnp.concatenate([wq, dup(wk), dup(wv), wu], axis=1).astype(BF16)
    w_kv = jnp.concatenate([dup(wk), dup(wv)], axis=1).astype(BF16)
    gq = jnp.tile(q_norm[0], N_Q_HEADS)[None]
    gk = jnp.tile(k_norm[0], 2 * N_KV_HEADS)[None]
    bd = jnp.asarray(np.kron(np.eye(N_Q_HEADS), np.full((HEAD_DIM, HEAD_DIM), 1.0 / HEAD_DIM)), BF16)
    cos_t, sin_t = _rope_tables(l)

    q, k, v, u = _inproj(x, mod, norm1, w_all, gq, gk, bd, cos_t, sin_t, tl)
    kx, vx = _ctxkv(ctx, mod, ctx_row, norm1, w_kv, gk, bd[:2 * kvw, :2 * kvw])
    an = _attention(sink[0], q, k, v, kx, vx, attn_out_norm)

    x0, z, zn = _hconv(u, conv_w[0], conv_b)
    ffn = filt_w2.shape[-1]
    zf = _filter_features(l, ffn)
    w1 = jnp.pad(filt_w1[0], ((0, ffn - POS_EMB_DIM), (0, 0)))
    deltas = jnp.asarray(np.linspace(MIN_DECAY, MAX_DECAY, hw)[None, :], F32)
    kp, km, kn = _filter(zf, w1, filt_b1, filt_w2[0], filt_b2, filt_w3[0], filt_b3, filt_freq,
                         filt_w4[0], deltas)
    cm, sm = _dftgen(l, tl)
    kr, ks = _kspec(cm, sm, kp, km, tl)
    fa, fb = _hfwd(cm, sm, z, kr, ks, tl)
    yn = _hinv(cm, sm, fa, fb, x0, z, zn, kn, filt_bias, hyena_out_norm, tl)

    wr = jnp.pad(w_router[0], ((0, 0), (0, LANES - N_EXPERTS)))
    br = jnp.concatenate([b_router[0], jnp.full((LANES - N_EXPERTS,), NEG_INF, F32)])[None]
    tri = jnp.asarray(np.tril(np.ones((tl, tl)), -1), BF16)
    xn, h2, route, cnt = _mixout(an, yn, x, mod, w_out[0].astype(BF16), norm2, wr, br, tri, tl)

    rows = EXPERT_ROWS
    a_tot = t * TOP_K
    n_blk = -(-a_tot // rows) + N_EXPERTS
    idx = route[:, 0:TOP_K].astype(jnp.int32)
    pos = route[:, 2 * TOP_K:3 * TOP_K].astype(jnp.int32)
    counts = cnt[0, :N_EXPERTS].astype(jnp.int32)
    pcounts = (counts + rows - 1) // rows * rows
    pends = jnp.cumsum(pcounts)
    pstarts = pends - pcounts
    dest = pstarts[idx] + pos
    block_start = jnp.arange(n_blk, dtype=jnp.int32) * rows
    block_e = jnp.minimum(jnp.sum(pends[None, :] <= block_start[:, None], axis=1), N_EXPERTS - 1).astype(jnp.int32)
    order = jnp.argsort(idx.reshape(-1), stable=True).astype(jnp.int32)
    starts = jnp.cumsum(counts) - counts
    offset = (block_start - pstarts[block_e])[:, None] + jnp.arange(rows, dtype=jnp.int32)[None, :]
    src = starts[block_e][:, None] + jnp.minimum(offset, counts[block_e][:, None] - 1)
    slot_tok = order[jnp.clip(src, 0, a_tot - 1)] // TOP_K

    ys = _experts(block_e, slot_tok.reshape(n_blk, 1, rows), h2,
                  w_gu[0], b_gu[0][:, None, :], w_down[0], b_down[0][:, None, :], rows)

    tc = min(COMBINE_ROWS, l)
    dest_kmaj = dest.reshape(t // tc, tc, TOP_K).transpose(0, 2, 1).reshape(t // tc, 1, TOP_K * tc)
    g2 = mod[:b, None, 5 * d:6 * d]
    out = _combine(dest_kmaj, ys, xn.reshape(t, d), route, g2, tc)
    return out.reshape(b, l, d)
```

```python
import functools
import math

import numpy as np
import jax
import jax.numpy as jnp
from jax import lax
from jax.experimental import pallas as pl
from jax.experimental.pallas import tpu as pltpu

F32 = jnp.float32
BF16 = jnp.bfloat16

LANES = 128
SUBLANES = 8
VMEM_LIMIT = 56 * 1024 * 1024

HEAD_DIM = 64
N_Q_HEADS = 8
N_KV_HEADS = 2
GROUP = N_Q_HEADS // N_KV_HEADS
GRID_W = 64
WINDOW = 128
ROPE_BASE = 10000.0
ATTN_SCALE = HEAD_DIM ** -0.5
POS_EMB_DIM = 33
POS_BANDS = (POS_EMB_DIM - 1) // 2
DECAY_TARGET = 1e-2
MAX_DECAY = -math.log(DECAY_TARGET) / 0.3
MIN_DECAY = -math.log(DECAY_TARGET) / 1.5
N_EXPERTS = 32
TOP_K = 4
SWIGLU_LIMIT = 7.0
SWIGLU_ALPHA = 1.702
EPS = 1e-6
NEG_INF = -1e30

EXPERT_ROWS = 512
COMBINE_ROWS = 256


def _cparams(*sem):
    return pltpu.CompilerParams(dimension_semantics=sem, vmem_limit_bytes=VMEM_LIMIT)


def _split(a):
    hi = a.astype(BF16)
    lo = (a - hi.astype(F32)).astype(BF16)
    return hi, lo


def _dot(a, b):
    return jnp.dot(a, b, preferred_element_type=F32)


def _dot3(a, b):
    ah, al = _split(a)
    bh, bl = _split(b)
    return _dot(ah, bh) + _dot(al, bh) + _dot(ah, bl)


def _rms(x):
    return lax.rsqrt(jnp.mean(x * x, axis=-1, keepdims=True) + EPS)


def _adaln_kernel(c_ref, w_ref, b_ref, o_ref):
    c = c_ref[...]
    o_ref[...] = _dot3(c * jax.nn.sigmoid(c), w_ref[...]) + b_ref[...]


def _adaln(c_all, w_mod, b_mod):
    rows, d = c_all.shape
    n = w_mod.shape[1]
    tn = 1024
    return pl.pallas_call(
        _adaln_kernel,
        grid=(n // tn,),
        in_specs=[pl.BlockSpec((rows, d), lambda j: (0, 0)),
                  pl.BlockSpec((d, tn), lambda j: (0, j)),
                  pl.BlockSpec((1, tn), lambda j: (0, j))],
        out_specs=pl.BlockSpec((rows, tn), lambda j: (0, j)),
        out_shape=jax.ShapeDtypeStruct((rows, n), F32),
        compiler_params=_cparams("arbitrary"),
        name="adaln",
    )(c_all, w_mod, b_mod[None])


def _head_rms(x, bd):
    hi, lo = _split(x * x)
    return x * lax.rsqrt(_dot(hi, bd) + _dot(lo, bd) + EPS)


def _rope128(x, cos, sin):
    lane = lax.broadcasted_iota(jnp.int32, x.shape, 1)
    partner = jnp.where(lane % 32 < 16, pltpu.roll(x, LANES - 16, 1), pltpu.roll(x, 16, 1))
    return x * cos + partner * sin


def _modulated(x, mod_ref, row, norm_ref, d):
    sh = mod_ref[pl.ds(row, 1), 0:d]
    sc = mod_ref[pl.ds(row, 1), d:2 * d]
    return (x * _rms(x)) * norm_ref[...] * (1 + sc) + sh


def _inproj_kernel(x_ref, mod_ref, n1_ref, w_ref, gq_ref, gk_ref, bd_ref, cos_ref, sin_ref,
                   q_ref, k_ref, v_ref, u_ref):
    d = x_ref.shape[-1]
    aw = q_ref.shape[-1]
    kw = k_ref.shape[-1]
    hb = _modulated(x_ref[...], mod_ref, pl.program_id(1), n1_ref, d).astype(BF16)
    cos = cos_ref[...]
    sin = sin_ref[...]
    bd = bd_ref[...]
    q = _head_rms(_dot(hb, w_ref[:, 0:aw]), bd) * gq_ref[...]
    for c in range(aw // LANES):
        sl = slice(c * LANES, (c + 1) * LANES)
        q_ref[:, sl] = (_rope128(q[:, sl], cos, sin) * ATTN_SCALE).astype(BF16)
    k = _head_rms(_dot(hb, w_ref[:, aw:aw + kw]), bd[0:kw, 0:kw]) * gk_ref[...]
    for c in range(kw // LANES):
        sl = slice(c * LANES, (c + 1) * LANES)
        k_ref[:, sl] = _rope128(k[:, sl], cos, sin).astype(BF16)
    v_ref[...] = _dot(hb, w_ref[:, aw + kw:aw + 2 * kw]).astype(BF16)
    u_ref[...] = _dot(hb, w_ref[:, aw + 2 * kw:])


def _inproj(x, mod, norm1, w_all, gq, gk, bd, cos_t, sin_t, tl):
    b, l, d = x.shape
    aw, kw = gq.shape[1], gk.shape[1]
    uw = w_all.shape[1] - aw - 2 * kw
    const = lambda i, j: (0, 0)
    tok = lambda i, j: (j, i, 0)
    return pl.pallas_call(
        _inproj_kernel,
        grid=(l // tl, b),
        in_specs=[pl.BlockSpec((None, tl, d), tok),
                  pl.BlockSpec(mod.shape, const),
                  pl.BlockSpec((1, d), const),
                  pl.BlockSpec(w_all.shape, const),
                  pl.BlockSpec((1, aw), const),
                  pl.BlockSpec((1, kw), const),
                  pl.BlockSpec(bd.shape, const),
                  pl.BlockSpec((tl, LANES), lambda i, j: (i, 0)),
                  pl.BlockSpec((tl, LANES), lambda i, j: (i, 0))],
        out_specs=[pl.BlockSpec((None, tl, aw), tok),
                   pl.BlockSpec((None, tl, kw), tok),
                   pl.BlockSpec((None, tl, kw), tok),
                   pl.BlockSpec((None, tl, uw), tok)],
        out_shape=[jax.ShapeDtypeStruct((b, l, aw), BF16),
                   jax.ShapeDtypeStruct((b, l, kw), BF16),
                   jax.ShapeDtypeStruct((b, l, kw), BF16),
                   jax.ShapeDtypeStruct((b, l, uw), F32)],
        compiler_params=_cparams("arbitrary", "arbitrary"),
        name="inproj",
    )(x, mod, norm1, w_all, gq, gk, bd, cos_t, sin_t)


def _ctxkv_kernel(row, x_ref, mod_ref, n1_ref, w_ref, gk_ref, bd_ref, k_ref, v_ref):
    d = x_ref.shape[-1]
    kw = k_ref.shape[-1]
    hb = _modulated(x_ref[...], mod_ref, row, n1_ref, d).astype(BF16)
    k = _head_rms(_dot(hb, w_ref[:, 0:kw]), bd_ref[...]) * gk_ref[...]
    k_ref[...] = k.astype(BF16)
    v_ref[...] = _dot(hb, w_ref[:, kw:]).astype(BF16)


def _ctxkv(ctx, mod, ctx_row, norm1, w_kv, gk, bd):
    b, lc, d = ctx.shape
    kw = gk.shape[1]
    const = lambda i: (0, 0)
    tok = lambda i: (i, 0, 0)
    return pl.pallas_call(
        functools.partial(_ctxkv_kernel, ctx_row),
        grid=(b,),
        in_specs=[pl.BlockSpec((None, lc, d), tok),
                  pl.BlockSpec(mod.shape, const),
                  pl.BlockSpec((1, d), const),
                  pl.BlockSpec(w_kv.shape, const),
                  pl.BlockSpec((1, kw), const),
                  pl.BlockSpec(bd.shape, const)],
        out_specs=[pl.BlockSpec((None, lc, kw), tok), pl.BlockSpec((None, lc, kw), tok)],
        out_shape=[jax.ShapeDtypeStruct((b, lc, kw), BF16)] * 2,
        compiler_params=_cparams("arbitrary"),
        name="ctxkv",
    )(ctx, mod, norm1, w_kv, gk, bd)


def _attn_kernel(sink_ref, q_ref, kp_ref, kc_ref, kn_ref, vp_ref, vc_ref, vn_ref, kx_ref, vx_ref,
                 ga_ref, o_ref, acc_ref):
    i = pl.program_id(1)
    nb = pl.num_programs(1)
    tq = q_ref.shape[0]
    lc = kx_ref.shape[0]
    nk = 3 * tq + lc
    pairs = GROUP // 2
    rows = pairs * tq
    r = lax.broadcasted_iota(jnp.int32, (rows, tq), 0) % tq
    j = lax.broadcasted_iota(jnp.int32, (rows, tq), 1)
    ok_prev = j >= r + jnp.where(i > 0, 0, tq)
    ok_next = j <= r - jnp.where(i < nb - 1, 0, tq)
    lo = lax.broadcasted_iota(jnp.int32, (nk, LANES), 1) < HEAD_DIM
    zero = jnp.zeros((nk, LANES), BF16)
    top = lax.broadcasted_iota(jnp.int32, (rows, 1), 0) < tq

    ks, vs, qs, sinks = [], [], [], []
    for h in range(N_KV_HEADS):
        hs = slice(h * LANES, (h + 1) * LANES)
        kcat = jnp.concatenate([kp_ref[:, hs], kc_ref[:, hs], kn_ref[:, hs], kx_ref[:, hs]], axis=0)
        vcat = jnp.concatenate([vp_ref[:, hs], vc_ref[:, hs], vn_ref[:, hs], vx_ref[:, hs]], axis=0)
        q2 = jnp.concatenate([q_ref[:, (h * pairs + p) * LANES:(h * pairs + p + 1) * LANES]
                              for p in range(pairs)], axis=0)
        for half in range(2):
            ks.append(jnp.where(lo, kcat, zero) if half == 0 else jnp.where(lo, zero, kcat))
            vs.append(jnp.where(lo, vcat, zero) if half == 0 else jnp.where(lo, zero, vcat))
            qs.append(q2)
            sinks.append(jnp.where(top, sink_ref[h * GROUP + half], sink_ref[h * GROUP + 2 + half]))
    ss = [lax.dot_general(qq, kk, (((1,), (1,)), ((), ())), preferred_element_type=F32)
          for qq, kk in zip(qs, ks)]
    ss = [jnp.concatenate([jnp.where(ok_prev, s[:, 0:tq], NEG_INF), s[:, tq:2 * tq],
                           jnp.where(ok_next, s[:, 2 * tq:3 * tq], NEG_INF), s[:, 3 * tq:]], axis=1)
          for s in ss]
    ms = [jnp.maximum(jnp.max(s, axis=-1, keepdims=True), sk) for s, sk in zip(ss, sinks)]
    es = [jnp.exp(s - m) for s, m in zip(ss, ms)]
    dens = [jnp.sum(e, axis=-1, keepdims=True) + jnp.exp(sk - m) for e, sk, m in zip(es, sinks, ms)]
    outs = [_dot(e.astype(BF16), vv) / den for e, vv, den in zip(es, vs, dens)]
    for h in range(N_KV_HEADS):
        both = outs[2 * h] + outs[2 * h + 1]
        for p in range(pairs):
            acc_ref[:, (h * pairs + p) * LANES:(h * pairs + p + 1) * LANES] = both[p * tq:(p + 1) * tq]
    a = acc_ref[...]
    o_ref[...] = (a * _rms(a) * ga_ref[...]).astype(BF16)


def _attention(sink, q, k, v, kx, vx, ga):
    b, l, aw = q.shape
    kw = k.shape[-1]
    lc = kx.shape[1]
    tq = WINDOW
    nb = l // tq
    cur = lambda bi, i: (bi, i, 0)
    prev = lambda bi, i: (bi, jnp.maximum(i - 1, 0), 0)
    nxt = lambda bi, i: (bi, jnp.minimum(i + 1, nb - 1), 0)
    ctx = lambda bi, i: (bi, 0, 0)
    kv = lambda m: pl.BlockSpec((None, tq, kw), m)
    return pl.pallas_call(
        _attn_kernel,
        grid=(b, nb),
        in_specs=[pl.BlockSpec(memory_space=pltpu.SMEM),
                  pl.BlockSpec((None, tq, aw), cur),
                  kv(prev), kv(cur), kv(nxt), kv(prev), kv(cur), kv(nxt),
                  pl.BlockSpec((None, lc, kw), ctx), pl.BlockSpec((None, lc, kw), ctx),
                  pl.BlockSpec((1, aw), lambda bi, i: (0, 0))],
        out_specs=pl.BlockSpec((None, tq, aw), cur),
        out_shape=jax.ShapeDtypeStruct((b, l, aw), BF16),
        scratch_shapes=[pltpu.VMEM((tq, aw), F32)],
        compiler_params=_cparams("arbitrary", "arbitrary"),
        name="attn",
    )(sink, q, k, k, k, v, v, v, kx, vx, ga)


def _alternating(rows, cols):
    return (1 - 2 * (lax.broadcasted_iota(jnp.int32, (rows, cols), 0) % 2)).astype(F32)


def _hconv_kernel(u0_ref, u1_ref, u2_ref, w0_ref, w1_ref, w2_ref, b0_ref, b1_ref, b2_ref,
                  x0e_ref, x0o_ref, ze_ref, zo_ref, qr_ref, qs_ref, scr):
    l, cb = u0_ref.shape
    half = l // 2
    row = lax.broadcasted_iota(jnp.int32, (l, cb), 0)

    def conv(u_ref, w_ref, b_ref):
        u = u_ref[...]
        before = jnp.where(row == 0, 0.0, pltpu.roll(u, 1, 0))
        after = jnp.where(row == l - 1, 0.0, pltpu.roll(u, l - 1, 0))
        return b_ref[...] + before * w_ref[0:1, :] + u * w_ref[1:2, :] + after * w_ref[2:3, :]

    def parity_halves(v):
        scr[...] = v
        return scr[pl.ds(0, half, stride=2), :], scr[pl.ds(1, half, stride=2), :]

    x0e_ref[...], x0o_ref[...] = parity_halves(conv(u0_ref, w0_ref, b0_ref))
    ze, zo = parity_halves(conv(u1_ref, w1_ref, b1_ref) * conv(u2_ref, w2_ref, b2_ref))
    ze_ref[...] = ze.astype(BF16)
    zo_ref[...] = zo.astype(BF16)
    sign = _alternating(half, cb)
    qr_ref[...] = jnp.sum(ze * sign, axis=0, keepdims=True)
    qs_ref[...] = jnp.sum(zo * sign, axis=0, keepdims=True)


def _hconv(u, conv_w, conv_b, cb=LANES):
    b, l, w3 = u.shape
    c = w3 // 3
    n = c // cb
    us = [pl.BlockSpec((None, l, cb), lambda bi, j, g=g: (bi, 0, g * n + j)) for g in range(3)]
    ws = [pl.BlockSpec((3, cb), lambda bi, j, g=g: (0, g * n + j)) for g in range(3)]
    bs = [pl.BlockSpec((1, cb), lambda bi, j, g=g: (0, g * n + j)) for g in range(3)]
    out = lambda bi, j: (bi, 0, j)
    seq = pl.BlockSpec((None, l // 2, cb), out)
    vec = pl.BlockSpec((None, 1, cb), out)
    return pl.pallas_call(
        _hconv_kernel,
        grid=(b, n),
        in_specs=us + ws + bs,
        out_specs=[seq, seq, seq, seq, vec, vec],
        out_shape=[jax.ShapeDtypeStruct((b, l // 2, c), F32)] * 2 + [jax.ShapeDtypeStruct((b, l // 2, c), BF16)] * 2
        + [jax.ShapeDtypeStruct((b, 1, c), F32)] * 2,
        scratch_shapes=[pltpu.VMEM((l, cb), F32)],
        compiler_params=_cparams("arbitrary", "arbitrary"),
        name="hconv",
    )(u, u, u, conv_w, conv_w, conv_w, conv_b, conv_b, conv_b)


def _filter_kernel(fe_ref, fo_ref, w1_ref, b1_ref, w2_ref, b2_ref, w3_ref, b3_ref, fr_ref, w4f_ref, w4b_ref,
                   dl_ref, kpe_ref, kpo_ref, kme_ref, kmo_ref, krq_ref, ksq_ref, he_scr, ho_scr):
    half, cf = kpe_ref.shape
    l = 2 * half

    @pl.when(pl.program_id(0) == 0)
    def _():
        fr = fr_ref[...]
        for f_ref, h_scr in ((fe_ref, he_scr), (fo_ref, ho_scr)):
            h = jnp.sin(fr * (_dot3(f_ref[...], w1_ref[...]) + b1_ref[...]))
            h = jnp.sin(fr * (_dot3(h, w2_ref[...]) + b2_ref[...]))
            h_scr[...] = jnp.sin(fr * (_dot3(h, w3_ref[...]) + b3_ref[...]))

    row = lax.broadcasted_iota(jnp.int32, (half, cf), 0)

    def taps(h_scr, parity):
        t = (2 * row + parity).astype(F32) / (l - 1)
        decay = jnp.exp(-t * dl_ref[...])
        h = h_scr[...]
        return _dot3(h, w4f_ref[...]) * decay, _dot3(h, w4b_ref[...]) * decay

    kfe, kbe = taps(he_scr, 0)
    kbe = jnp.where(row == 0, 0.0, kbe)
    kfo, kbo = taps(ho_scr, 1)
    nrm = lax.rsqrt(jnp.sum(kfe * kfe + kbe * kbe + kfo * kfo + kbo * kbo, axis=0, keepdims=True) + EPS)
    kpe = (kfe + kbe) * nrm
    kmo = (kfo - kbo) * nrm
    kpe_ref[...] = kpe.astype(BF16)
    kpo_ref[...] = ((kfo + kbo) * nrm).astype(BF16)
    kme_ref[...] = ((kfe - kbe) * nrm).astype(BF16)
    kmo_ref[...] = kmo.astype(BF16)
    sign = _alternating(half, cf)
    krq_ref[...] = jnp.sum(kpe * sign, axis=0, keepdims=True)
    ksq_ref[...] = jnp.sum(kmo * sign, axis=0, keepdims=True)


def _filter(fe, fo, w1, b1, w2, b2, w3, b3, fr, w4, deltas, cf=LANES):
    half, zw = fe.shape
    ffn = w2.shape[0]
    c = w4.shape[1] // 2
    n = c // cf
    const = lambda j: (0, 0)
    col = lambda j: (0, j)
    vec = pl.BlockSpec((1, ffn), const)
    mat = pl.BlockSpec((ffn, ffn), const)
    return pl.pallas_call(
        _filter_kernel,
        grid=(n,),
        in_specs=[pl.BlockSpec((half, zw), const), pl.BlockSpec((half, zw), const),
                  pl.BlockSpec((zw, ffn), const), vec, mat, vec, mat, vec, vec,
                  pl.BlockSpec((ffn, cf), col),
                  pl.BlockSpec((ffn, cf), lambda j: (0, n + j)),
                  pl.BlockSpec((1, cf), col)],
        out_specs=[pl.BlockSpec((half, cf), col)] * 4 + [pl.BlockSpec((1, cf), col)] * 2,
        out_shape=[jax.ShapeDtypeStruct((half, c), BF16)] * 4 + [jax.ShapeDtypeStruct((1, c), F32)] * 2,
        scratch_shapes=[pltpu.VMEM((half, ffn), F32)] * 2,
        compiler_params=_cparams("arbitrary"),
        name="filt",
    )(fe, fo, w1, b1, w2, b2, w3, b3, fr, w4, w4, deltas)


DFT_FINE = 64


def _dftgen_kernel(ca_ref, sa_ref, cb_ref, sb_ref, c_ref, s_ref):
    cb = cb_ref[...]
    sb = sb_ref[...]
    for a in range(ca_ref.shape[0]):
        ca = ca_ref[a:a + 1, :]
        sa = sa_ref[a:a + 1, :]
        rs = slice(a * DFT_FINE, (a + 1) * DFT_FINE)
        c_ref[rs, :] = (ca * cb - sa * sb).astype(BF16)
        s_ref[rs, :] = (sa * cb + ca * sb).astype(BF16)


def _dftgen(coarse, fine, l, tf, name):
    tabs = [jnp.asarray(fn((k % (2 * l)) * (math.pi / l)), F32) for k in (coarse, fine) for fn in (np.cos, np.sin)]
    rows, width = coarse.shape[0] * DFT_FINE, coarse.shape[1]
    na = tf // DFT_FINE
    tile = lambda i: (i, 0)
    const = lambda i: (0, 0)
    return pl.pallas_call(
        _dftgen_kernel,
        grid=(rows // tf,),
        in_specs=[pl.BlockSpec((na, width), tile), pl.BlockSpec((na, width), tile),
                  pl.BlockSpec((DFT_FINE, width), const), pl.BlockSpec((DFT_FINE, width), const)],
        out_specs=[pl.BlockSpec((tf, width), tile), pl.BlockSpec((tf, width), tile)],
        out_shape=[jax.ShapeDtypeStruct((rows, width), BF16)] * 2,
        compiler_params=_cparams("arbitrary"),
        name=name,
    )(*tabs)


def _dft_matrices(l, tf):
    half = l // 2
    lo = np.arange(half, dtype=np.int64)[None, :]
    a = np.arange(half // DFT_FINE, dtype=np.int64)[:, None] * DFT_FINE
    i = np.arange(DFT_FINE, dtype=np.int64)[:, None]
    ce, se = _dftgen(a * 2 * lo, i * 2 * lo, l, tf, "dft_even")
    co, so = _dftgen(a * (2 * lo + 1), i * (2 * lo + 1), l, tf, "dft_odd")
    cot, sot = _dftgen(2 * a * lo, (2 * i + 1) * lo, l, tf, "dft_odd_t")
    return ce, se, co, so, cot, sot


def _kspec_kernel(n_fft, ce_ref, se_ref, co_ref, so_ref, kpe_ref, kpo_ref, kme_ref, kmo_ref,
                  krl_ref, krm_ref, ksl_ref, ksm_ref):
    tf = ce_ref.shape[0]
    f = pl.program_id(0) * tf + lax.broadcasted_iota(jnp.int32, (tf, 1), 0)
    w = jnp.where(f == 0, 1.0 / n_fft, 2.0 / n_fft)
    ce = _dot(ce_ref[...], kpe_ref[...])
    co = _dot(co_ref[...], kpo_ref[...])
    se = _dot(se_ref[...], kme_ref[...])
    so = _dot(so_ref[...], kmo_ref[...])
    krl_ref[...] = (ce + co) * w
    krm_ref[...] = (ce - co) * w
    ksl_ref[...] = (so + se) * w
    ksm_ref[...] = (so - se) * w


def _kspec(mats, kpe, kpo, kme, kmo, tf):
    half, c = kpe.shape
    const = lambda i: (0, 0)
    tile = lambda i: (i, 0)
    return pl.pallas_call(
        functools.partial(_kspec_kernel, 4 * half),
        grid=(half // tf,),
        in_specs=[pl.BlockSpec((tf, half), tile)] * 4 + [pl.BlockSpec((half, c), const)] * 4,
        out_specs=[pl.BlockSpec((tf, c), tile)] * 4,
        out_shape=[jax.ShapeDtypeStruct((half, c), F32)] * 4,
        compiler_params=_cparams("arbitrary"),
        name="kspec",
    )(*mats, kpe, kpo, kme, kmo)


def _hfwd_kernel(ce_ref, se_ref, co_ref, so_ref, ze_ref, zo_ref, krl_ref, krm_ref, ksl_ref, ksm_ref,
                 p1_ref, p2_ref, p3_ref, p4_ref):
    ze = ze_ref[...]
    zo = zo_ref[...]
    ce = _dot(ce_ref[...], ze)
    co = _dot(co_ref[...], zo)
    se = _dot(se_ref[...], ze)
    so = _dot(so_ref[...], zo)

    def product(zr, zs, kr_ref, ks_ref):
        kr = kr_ref[...]
        ks = ks_ref[...]
        return zr * kr - zs * ks, zr * ks + zs * kr

    al, bl = product(ce + co, so + se, krl_ref, ksl_ref)
    am, bm = product(ce - co, so - se, krm_ref, ksm_ref)
    p1_ref[...] = (al + am).astype(BF16)
    p2_ref[...] = (bl - bm).astype(BF16)
    p3_ref[...] = (al - am).astype(BF16)
    p4_ref[...] = (bl + bm).astype(BF16)


def _hfwd(mats, ze, zo, spec, tf):
    b, half, c = ze.shape
    tile = lambda i, bi: (i, 0)
    seq = lambda i, bi: (bi, 0, 0)
    out = lambda i, bi: (bi, i, 0)
    return pl.pallas_call(
        _hfwd_kernel,
        grid=(half // tf, b),
        in_specs=[pl.BlockSpec((tf, half), tile)] * 4 + [pl.BlockSpec((None, half, c), seq)] * 2
        + [pl.BlockSpec((tf, c), tile)] * 4,
        out_specs=[pl.BlockSpec((None, tf, c), out)] * 4,
        out_shape=[jax.ShapeDtypeStruct((b, half, c), BF16)] * 4,
        compiler_params=_cparams("arbitrary", "arbitrary"),
        name="hfwd",
    )(*mats, ze, zo, *spec)


def _hinv_kernel(ce_ref, se_ref, cot_ref, sot_ref, p1_ref, p2_ref, p3_ref, p4_ref, x0e_ref, x0o_ref,
                 ze_ref, zo_ref, qr_ref, qs_ref, krq_ref, ksq_ref, bias_ref, g_ref, o_ref, scr):
    tt, half = ce_ref.shape
    n_fft = 4 * half
    ye = _dot(ce_ref[...], p1_ref[...]) + _dot(se_ref[...], p2_ref[...])
    yo = _dot(cot_ref[...], p3_ref[...]) + _dot(sot_ref[...], p4_ref[...])
    sign = (1 - 2 * ((pl.program_id(0) * tt + lax.broadcasted_iota(jnp.int32, (tt, 1), 0)) % 2)).astype(F32)
    qr, qs, krq, ksq = qr_ref[...], qs_ref[...], krq_ref[...], ksq_ref[...]
    ye = ye + sign * ((qr * krq - qs * ksq) * (2.0 / n_fft))
    yo = yo + sign * ((qr * ksq + qs * krq) * (2.0 / n_fft))

    def finish(y, x0_ref, z_ref):
        hy = x0_ref[...] * (y + z_ref[...].astype(F32) * bias_ref[...])
        return hy * _rms(hy) * g_ref[...]

    he = finish(ye, x0e_ref, ze_ref)
    ho = finish(yo, x0o_ref, zo_ref)
    for c in range(scr.shape[0]):
        cs = slice(c * LANES, (c + 1) * LANES)
        scr[c, pl.ds(0, tt, stride=2), :] = he[:, cs]
        scr[c, pl.ds(1, tt, stride=2), :] = ho[:, cs]
        o_ref[:, cs] = scr[c].astype(BF16)


def _hinv(ce, se, cot, sot, ps, x0e, x0o, ze, zo, qr, qs, krq, ksq, bias, gain, tt):
    b, half, c = ze.shape
    tile = lambda i, bi: (i, 0)
    full = lambda i, bi: (bi, 0, 0)
    tok = lambda i, bi: (bi, i, 0)
    const = lambda i, bi: (0, 0)
    vec = pl.BlockSpec((1, c), const)
    return pl.pallas_call(
        _hinv_kernel,
        grid=(half // tt, b),
        in_specs=[pl.BlockSpec((tt, half), tile)] * 4 + [pl.BlockSpec((None, half, c), full)] * 4
        + [pl.BlockSpec((None, tt, c), tok)] * 4 + [pl.BlockSpec((None, 1, c), full)] * 2 + [vec] * 4,
        out_specs=pl.BlockSpec((None, 2 * tt, c), tok),
        out_shape=jax.ShapeDtypeStruct((b, 2 * half, c), BF16),
        scratch_shapes=[pltpu.VMEM((c // LANES, 2 * tt, LANES), F32)],
        compiler_params=_cparams("arbitrary", "arbitrary"),
        name="hinv",
    )(ce, se, cot, sot, *ps, x0e, x0o, ze, zo, qr, qs, krq, ksq, bias, gain)


def _mixout_kernel(an_ref, yn_ref, x_ref, mod_ref, wo_ref, n2_ref, wr_ref, br_ref, tri_ref,
                   xn_ref, h2_ref, route_ref, cnt_ref, carry):
    bi = pl.program_id(0)
    d = x_ref.shape[-1]
    half = an_ref.shape[-1]
    tl = x_ref.shape[0]

    @pl.when((bi == 0) & (pl.program_id(1) == 0))
    def _():
        carry[...] = jnp.zeros_like(carry)

    mix = _dot(an_ref[...], wo_ref[0:half, :]) + _dot(yn_ref[...], wo_ref[half:, :])
    g1 = mod_ref[pl.ds(bi, 1), 2 * d:3 * d]
    sh2 = mod_ref[pl.ds(bi, 1), 3 * d:4 * d]
    sc2 = mod_ref[pl.ds(bi, 1), 4 * d:5 * d]
    xn = x_ref[...] + g1 * mix
    xn_ref[...] = xn
    h2 = (xn * _rms(xn)) * n2_ref[...] * (1 + sc2) + sh2
    for j in range(d // LANES):
        _token_chunk(h2_ref, 0, tl, j)[...] = h2[:, j * LANES:(j + 1) * LANES]

    logits = _dot3(h2, wr_ref[...]) + br_ref[...]
    lane = lax.broadcasted_iota(jnp.int32, (tl, LANES), 1).astype(F32)
    vals, idxs, sels = [], [], []
    cur = logits
    for _ in range(TOP_K):
        m = jnp.max(cur, axis=-1, keepdims=True)
        idx = jnp.min(jnp.where(cur == m, lane, float(LANES)), axis=-1, keepdims=True)
        sel = lane == idx
        vals.append(m)
        idxs.append(idx)
        sels.append(sel)
        cur = jnp.where(sel, -jnp.inf, cur)
    es = [jnp.exp(v - vals[0]) for v in vals]
    den = es[0] + es[1] + es[2] + es[3]
    hot = sum(s.astype(F32) for s in sels)
    before = _dot(tri_ref[...], hot.astype(BF16)) + carry[...]
    carry[...] = carry[...] + jnp.sum(hot, axis=0, keepdims=True)
    cnt_ref[...] = carry[...]
    route = jnp.zeros((tl, LANES), F32)
    for k in range(TOP_K):
        pos = jnp.sum(jnp.where(sels[k], before, 0.0), axis=-1, keepdims=True)
        route = jnp.where(lane == k, idxs[k], route)
        route = jnp.where(lane == TOP_K + k, es[k] / den, route)
        route = jnp.where(lane == 2 * TOP_K + k, pos, route)
    route_ref[...] = route


def _mixout(an, yn, x, mod, w_out, norm2, wr, br, tri, tl):
    b, l, d = x.shape
    half = an.shape[-1]
    nt = l // tl
    tok = lambda bi, i: (bi, i, 0)
    const = lambda bi, i: (0, 0)
    flat = lambda bi, i: (bi * nt + i, 0)
    return pl.pallas_call(
        _mixout_kernel,
        grid=(b, nt),
        in_specs=[pl.BlockSpec((None, tl, half), tok), pl.BlockSpec((None, tl, half), tok),
                  pl.BlockSpec((None, tl, d), tok),
                  pl.BlockSpec(mod.shape, const), pl.BlockSpec(w_out.shape, const),
                  pl.BlockSpec((1, d), const), pl.BlockSpec(wr.shape, const),
                  pl.BlockSpec((1, LANES), const), pl.BlockSpec((tl, tl), const)],
        out_specs=[pl.BlockSpec((None, tl, d), tok),
                   pl.BlockSpec((tl * SUBLANES, LANES), flat),
                   pl.BlockSpec((tl, LANES), flat),
                   pl.BlockSpec((1, LANES), const)],
        out_shape=[jax.ShapeDtypeStruct((b, l, d), F32),
                   jax.ShapeDtypeStruct((b * l * SUBLANES, LANES), F32),
                   jax.ShapeDtypeStruct((b * l, LANES), F32),
                   jax.ShapeDtypeStruct((1, LANES), F32)],
        scratch_shapes=[pltpu.VMEM((1, LANES), F32)],
        compiler_params=_cparams("arbitrary", "arbitrary"),
        name="mixout",
    )(an, yn, x, mod, w_out, norm2, wr, br, tri)


def _token_chunk(ref, first_token, n, j):
    return ref.at[pl.ds(first_token * SUBLANES + j, n, stride=SUBLANES), :]


def _token_copy(idx_ref, r, src_hbm, dst, sem):
    first_row = lambda tok: tok * SUBLANES if isinstance(tok, int) else pl.multiple_of(tok * SUBLANES, SUBLANES)
    return pltpu.make_async_copy(src_hbm.at[pl.ds(first_row(idx_ref[0, r]), SUBLANES)],
                                 dst.at[pl.ds(first_row(r), SUBLANES)], sem)


def _gather_tokens(idx_ref, n, src_hbm, dst, sem):
    def body(r, carry):
        _token_copy(idx_ref, 2 * r, src_hbm, dst, sem).start(priority=0)
        _token_copy(idx_ref, 2 * r + 1, src_hbm, dst, sem).start(priority=1)
        return carry
    lax.fori_loop(0, n // 2, body, 0, unroll=4)


def _wait_tokens(src_hbm, dst, sem):
    pltpu.make_async_copy(src_hbm.at[pl.ds(0, dst.shape[0])], dst, sem).wait()


EXPERT_STAGES = 4
EXPERT_SLOTS = 3


def _experts_kernel(be_ref, tok0_ref, tok1_ref, tok2_ref, h2_hbm, wgu_ref, bgu_ref, wdn_ref, bdn_ref,
                    ys_ref, xbuf, wgu_bf, wdn_bf, sem):
    i = pl.program_id(0)
    n = pl.num_programs(0)
    rows = xbuf.shape[1] // SUBLANES
    nchunk = wgu_ref.shape[0] // LANES
    ff = wdn_ref.shape[0]
    fc = ff // EXPERT_STAGES
    rc = -(-rows // (EXPERT_STAGES - 1))
    slot = i % EXPERT_SLOTS
    nxt = (i + 1) % EXPERT_SLOTS
    ahead = (i + 2) % EXPERT_SLOTS

    @pl.when(i == 0)
    def _():
        _gather_tokens(tok0_ref, rows, h2_hbm, xbuf.at[0], sem.at[0])
        _gather_tokens(tok1_ref, rows, h2_hbm, xbuf.at[1], sem.at[1])

    @pl.when((i == 0) | (be_ref[i] != be_ref[jnp.maximum(i - 1, 0)]))
    def _():
        wgu_bf[...] = wgu_ref[...].astype(BF16)
        wdn_bf[...] = wdn_ref[...].astype(BF16)

    _wait_tokens(h2_hbm, xbuf.at[slot], sem.at[slot])
    acc = None
    for c in range(EXPERT_STAGES):
        xs = jnp.concatenate([_token_chunk(xbuf.at[slot], 0, rows, j)[...] for j in range(nchunk)],
                             axis=1).astype(BF16)
        for r in range(c * rc, min((c + 1) * rc, rows)):
            _token_copy(tok2_ref, r, h2_hbm, xbuf.at[ahead], sem.at[ahead]).start(priority=r % 2)
        cg = slice(c * fc, (c + 1) * fc)
        cu = slice(ff + c * fc, ff + (c + 1) * fc)
        gate = jnp.minimum(_dot(xs, wgu_bf[:, cg]) + bgu_ref[:, cg], SWIGLU_LIMIT)
        up = jnp.clip(_dot(xs, wgu_bf[:, cu]) + bgu_ref[:, cu], -SWIGLU_LIMIT, SWIGLU_LIMIT)
        act = ((up + 1) * (gate * jax.nn.sigmoid(SWIGLU_ALPHA * gate))).astype(BF16)
        part = _dot(act, wdn_bf[cg, :])
        acc = part if acc is None else acc + part
    y = acc + bdn_ref[...]
    for j in range(nchunk):
        _token_chunk(ys_ref, 0, rows, j)[...] = y[:, j * LANES:(j + 1) * LANES]

    @pl.when(i == n - 1)
    def _():
        _wait_tokens(h2_hbm, xbuf.at[nxt], sem.at[nxt])
        _wait_tokens(h2_hbm, xbuf.at[ahead], sem.at[ahead])


def _experts(block_e, slot_tok, h2, wgu, bgu, wdn, bdn, rows):
    n_blk = block_e.shape[0]
    d, ff2 = wgu.shape[1:]
    ff = wdn.shape[1]
    assert n_blk >= EXPERT_SLOTS
    tok = lambda k: pl.BlockSpec((None, 1, rows), lambda i, be: (jnp.minimum(i + k, n_blk - 1), 0, 0),
                                 memory_space=pltpu.SMEM)
    grid_spec = pltpu.PrefetchScalarGridSpec(
        num_scalar_prefetch=1,
        grid=(n_blk,),
        in_specs=[tok(0), tok(1), tok(2),
                  pl.BlockSpec(memory_space=pl.ANY),
                  pl.BlockSpec((None, d, ff2), lambda i, be: (be[i], 0, 0)),
                  pl.BlockSpec((None, 1, ff2), lambda i, be: (be[i], 0, 0)),
                  pl.BlockSpec((None, ff, d), lambda i, be: (be[i], 0, 0)),
                  pl.BlockSpec((None, 1, d), lambda i, be: (be[i], 0, 0))],
        out_specs=pl.BlockSpec((rows * SUBLANES, LANES), lambda i, be: (i, 0)),
        scratch_shapes=[pltpu.VMEM((EXPERT_SLOTS, rows * SUBLANES, LANES), F32),
                        pltpu.VMEM((d, ff2), BF16), pltpu.VMEM((ff, d), BF16),
                        pltpu.SemaphoreType.DMA((EXPERT_SLOTS,))],
    )
    return pl.pallas_call(
        _experts_kernel,
        grid_spec=grid_spec,
        out_shape=jax.ShapeDtypeStruct((n_blk * rows * SUBLANES, LANES), F32),
        compiler_params=_cparams("arbitrary"),
        name="experts",
    )(block_e, slot_tok, slot_tok, slot_tok, h2, wgu, bgu, wdn, bdn)


def _combine_kernel(dest_ref, destn_ref, ys_hbm, xn_ref, route_ref, g2_ref, o_ref, ybuf, sem):
    i = pl.program_id(0)
    n = pl.num_programs(0)
    tc = xn_ref.shape[0]
    slot = i % 2

    @pl.when(i == 0)
    def _():
        _gather_tokens(dest_ref, TOP_K * tc, ys_hbm, ybuf.at[0], sem.at[0])

    @pl.when(i + 1 < n)
    def _():
        _gather_tokens(destn_ref, TOP_K * tc, ys_hbm, ybuf.at[1 - slot], sem.at[1 - slot])

    _wait_tokens(ys_hbm, ybuf.at[slot], sem.at[slot])
    route = route_ref[...]
    gates = [jnp.broadcast_to(route[:, TOP_K + k:TOP_K + k + 1], (tc, LANES)) for k in range(TOP_K)]
    for j in range(xn_ref.shape[1] // LANES):
        cs = slice(j * LANES, (j + 1) * LANES)
        acc = gates[0] * _token_chunk(ybuf.at[slot], 0, tc, j)[...]
        for k in range(1, TOP_K):
            acc = acc + gates[k] * _token_chunk(ybuf.at[slot], k * tc, tc, j)[...]
        o_ref[:, cs] = xn_ref[:, cs] + g2_ref[:, cs] * acc


def _combine(dest_kmaj, ys, xn, route, g2, tc):
    t, d = xn.shape
    n = t // tc
    per_batch = t // g2.shape[0] // tc
    return pl.pallas_call(
        _combine_kernel,
        grid=(n,),
        in_specs=[pl.BlockSpec((None, 1, TOP_K * tc), lambda i: (i, 0, 0), memory_space=pltpu.SMEM),
                  pl.BlockSpec((None, 1, TOP_K * tc), lambda i: (jnp.minimum(i + 1, n - 1), 0, 0),
                               memory_space=pltpu.SMEM),
                  pl.BlockSpec(memory_space=pl.ANY),
                  pl.BlockSpec((tc, d), lambda i: (i, 0)),
                  pl.BlockSpec((tc, LANES), lambda i: (i, 0)),
                  pl.BlockSpec((None, 1, d), lambda i: (i // per_batch, 0, 0))],
        out_specs=pl.BlockSpec((tc, d), lambda i: (i, 0)),
        out_shape=jax.ShapeDtypeStruct((t, d), F32),
        scratch_shapes=[pltpu.VMEM((2, TOP_K * tc * SUBLANES, LANES), F32), pltpu.SemaphoreType.DMA((2,))],
        compiler_params=_cparams("arbitrary"),
        name="combine",
    )(dest_kmaj, dest_kmaj, ys, xn, route, g2)


def _rope_tables(l):
    n_freq = HEAD_DIM // 4
    inv_freq = ROPE_BASE ** (-np.arange(n_freq, dtype=np.float64) / n_freq)
    tpos = np.arange(l)
    lane = np.arange(LANES) % HEAD_DIM
    pos = np.where(lane[None, :] < HEAD_DIM // 2, (tpos // GRID_W)[:, None], (tpos % GRID_W)[:, None])
    ang = pos * inv_freq[lane % n_freq][None, :]
    sign = np.where(lane % (2 * n_freq) < n_freq, -1.0, 1.0)[None, :]
    return jnp.asarray(np.cos(ang), F32), jnp.asarray(np.sin(ang) * sign, F32)


def _filter_features(l, width):
    t = np.linspace(0.0, 1.0, l)[:, None]
    w = 2.0 * math.pi * np.arange(l)[:, None] / l
    bands = np.linspace(1e-4, POS_BANDS - 1, POS_BANDS)[None, :]
    z = np.concatenate([t, np.cos(bands * w), -np.sin(bands * w)], axis=-1)
    return np.pad(z, ((0, 0), (0, width - z.shape[1]))).astype(np.float32)


def kernel(x, c, ctx, c_ctx, w_mod, b_mod, norm1, norm2, w_in, q_norm, k_norm, sink, conv_w, conv_b,
           filt_w1, filt_b1, filt_w2, filt_b2, filt_w3, filt_b3, filt_w4, filt_freq, filt_bias,
           attn_out_norm, hyena_out_norm, w_out, w_router, b_router, w_gu, b_gu, w_down, b_down):
    assert w_mod.shape[0] == 1, "single-layer configuration"
    b, l, d = x.shape
    assert d == SUBLANES * LANES, "token-tiled rows assume one (8, 128) tile per token"
    t = b * l
    aw = N_Q_HEADS * HEAD_DIM
    kvw = N_KV_HEADS * HEAD_DIM
    hw = conv_w.shape[-1] // 3
    tl = min(512, l)

    ctx_row = b
    pad_rows = -(b + 1) % SUBLANES
    c_all = jnp.concatenate([c, c_ctx[None], jnp.zeros((pad_rows, d), F32)], axis=0)
    mod = _adaln(c_all, w_mod[0], b_mod[0])

    w = w_in[0]
    wq, wk, wv, wu = w[:, :aw], w[:, aw:aw + kvw], w[:, aw + kvw:aw + 2 * kvw], w[:, aw + 2 * kvw:]
    dup = lambda m: jnp.concatenate([m[:, h * HEAD_DIM:(h + 1) * HEAD_DIM]
                                     for h in range(N_KV_HEADS) for _ in range(2)], axis=1)
    w_all = jnp.concatenate([wq, dup(wk), dup(wv), wu], axis=1).astype(BF16)
    w_kv = jnp.concatenate([dup(wk), dup(wv)], axis=1).astype(BF16)
    gq = jnp.tile(q_norm[0], N_Q_HEADS)[None]
    gk = jnp.tile(k_norm[0], 2 * N_KV_HEADS)[None]
    bd = jnp.asarray(np.kron(np.eye(N_Q_HEADS), np.full((HEAD_DIM, HEAD_DIM), 1.0 / HEAD_DIM)), BF16)
    cos_t, sin_t = _rope_tables(l)

    q, k, v, u = _inproj(x, mod, norm1, w_all, gq, gk, bd, cos_t, sin_t, tl)
    kx, vx = _ctxkv(ctx, mod, ctx_row, norm1, w_kv, gk, bd[:2 * kvw, :2 * kvw])
    an = _attention(sink[0], q, k, v, kx, vx, attn_out_norm)

    x0e, x0o, ze, zo, qr, qs = _hconv(u, conv_w[0], conv_b)
    ffn = filt_w2.shape[-1]
    zf = _filter_features(l, ffn)
    w1 = jnp.pad(filt_w1[0], ((0, ffn - POS_EMB_DIM), (0, 0)))
    deltas = jnp.asarray(np.linspace(MIN_DECAY, MAX_DECAY, hw)[None, :], F32)
    kpe, kpo, kme, kmo, krq, ksq = _filter(jnp.asarray(zf[0::2]), jnp.asarray(zf[1::2]), w1, filt_b1, filt_w2[0],
                                           filt_b2, filt_w3[0], filt_b3, filt_freq, filt_w4[0], deltas)
    tf = min(512, l // 2)
    ce, se, co, so, cot, sot = _dft_matrices(l, tf)
    spec = _kspec((ce, se, co, so), kpe, kpo, kme, kmo, tf)
    ps = _hfwd((ce, se, co, so), ze, zo, spec, tf)
    yn = _hinv(ce, se, cot, sot, ps, x0e, x0o, ze, zo, qr, qs, krq, ksq, filt_bias, hyena_out_norm,
               min(256, l // 2))

    wr = jnp.pad(w_router[0], ((0, 0), (0, LANES - N_EXPERTS)))
    br = jnp.concatenate([b_router[0], jnp.full((LANES - N_EXPERTS,), NEG_INF, F32)])[None]
    tri = jnp.asarray(np.tril(np.ones((tl, tl)), -1), BF16)
    xn, h2, route, cnt = _mixout(an, yn, x, mod, w_out[0].astype(BF16), norm2, wr, br, tri, tl)

    rows = EXPERT_ROWS
    a_tot = t * TOP_K
    n_blk = -(-a_tot // rows) + N_EXPERTS
    idx = route[:, 0:TOP_K].astype(jnp.int32)
    pos = route[:, 2 * TOP_K:3 * TOP_K].astype(jnp.int32)
    counts = cnt[0, :N_EXPERTS].astype(jnp.int32)
    pcounts = (counts + rows - 1) // rows * rows
    pends = jnp.cumsum(pcounts)
    pstarts = pends - pcounts
    dest = pstarts[idx] + pos
    block_start = jnp.arange(n_blk, dtype=jnp.int32) * rows
    block_e = jnp.minimum(jnp.sum(pends[None, :] <= block_start[:, None], axis=1), N_EXPERTS - 1).astype(jnp.int32)
    order = jnp.argsort(idx.reshape(-1), stable=True).astype(jnp.int32)
    starts = jnp.cumsum(counts) - counts
    offset = (block_start - pstarts[block_e])[:, None] + jnp.arange(rows, dtype=jnp.int32)[None, :]
    src = starts[block_e][:, None] + jnp.minimum(offset, counts[block_e][:, None] - 1)
    slot_tok = order[jnp.clip(src, 0, a_tot - 1)] // TOP_K

    ys = _experts(block_e, slot_tok.reshape(n_blk, 1, rows), h2,
                  w_gu[0], b_gu[0][:, None, :], w_down[0], b_down[0][:, None, :], rows)

    tc = min(COMBINE_ROWS, l)
    dest_kmaj = dest.reshape(t // tc, tc, TOP_K).transpose(0, 2, 1).reshape(t // tc, 1, TOP_K * tc)
    g2 = mod[:b, None, 5 * d:6 * d]
    out = _combine(dest_kmaj, ys, xn.reshape(t, d), route, g2, tc)
    return out.reshape(b, l, d)
```

```python
import functools
import math

import numpy as np
import jax
import jax.numpy as jnp
from jax import lax
from jax.experimental import pallas as pl
from jax.experimental.pallas import tpu as pltpu

F32 = jnp.float32
BF16 = jnp.bfloat16

LANES = 128
SUBLANES = 8
VMEM_LIMIT = 56 * 1024 * 1024

HEAD_DIM = 64
N_Q_HEADS = 8
N_KV_HEADS = 2
GROUP = N_Q_HEADS // N_KV_HEADS
GRID_W = 64
WINDOW = 128
ROPE_BASE = 10000.0
ATTN_SCALE = HEAD_DIM ** -0.5
POS_EMB_DIM = 33
POS_BANDS = (POS_EMB_DIM - 1) // 2
DECAY_TARGET = 1e-2
MAX_DECAY = -math.log(DECAY_TARGET) / 0.3
MIN_DECAY = -math.log(DECAY_TARGET) / 1.5
N_EXPERTS = 32
TOP_K = 4
SWIGLU_LIMIT = 7.0
SWIGLU_ALPHA = 1.702
EPS = 1e-6
NEG_INF = -1e30

EXPERT_ROWS = 256
COMBINE_ROWS = 256


def _cparams(*sem):
    return pltpu.CompilerParams(dimension_semantics=sem, vmem_limit_bytes=VMEM_LIMIT)


def _split(a):
    hi = a.astype(BF16)
    lo = (a - hi.astype(F32)).astype(BF16)
    return hi, lo


def _dot(a, b):
    return jnp.dot(a, b, preferred_element_type=F32)


def _dot3(a, b):
    ah, al = _split(a)
    bh, bl = _split(b)
    return _dot(ah, bh) + _dot(al, bh) + _dot(ah, bl)


def _rms(x):
    return lax.rsqrt(jnp.mean(x * x, axis=-1, keepdims=True) + EPS)


def _adaln_kernel(c_ref, w_ref, b_ref, o_ref):
    c = c_ref[...]
    o_ref[...] = _dot3(c * jax.nn.sigmoid(c), w_ref[...]) + b_ref[...]


def _adaln(c_all, w_mod, b_mod):
    rows, d = c_all.shape
    n = w_mod.shape[1]
    tn = 1024
    return pl.pallas_call(
        _adaln_kernel,
        grid=(n // tn,),
        in_specs=[pl.BlockSpec((rows, d), lambda j: (0, 0)),
                  pl.BlockSpec((d, tn), lambda j: (0, j)),
                  pl.BlockSpec((1, tn), lambda j: (0, j))],
        out_specs=pl.BlockSpec((rows, tn), lambda j: (0, j)),
        out_shape=jax.ShapeDtypeStruct((rows, n), F32),
        compiler_params=_cparams("arbitrary"),
        name="adaln",
    )(c_all, w_mod, b_mod[None])


def _head_rms(x, bd):
    hi, lo = _split(x * x)
    return x * lax.rsqrt(_dot(hi, bd) + _dot(lo, bd) + EPS)


def _rope128(x, cos, sin):
    lane = lax.broadcasted_iota(jnp.int32, x.shape, 1)
    partner = jnp.where(lane % 32 < 16, pltpu.roll(x, LANES - 16, 1), pltpu.roll(x, 16, 1))
    return x * cos + partner * sin


def _modulated(x, mod_ref, row, norm_ref, d):
    sh = mod_ref[pl.ds(row, 1), 0:d]
    sc = mod_ref[pl.ds(row, 1), d:2 * d]
    return (x * _rms(x)) * norm_ref[...] * (1 + sc) + sh


INPROJ_SUBTILES = 2


def _inproj_kernel(x_ref, mod_ref, n1_ref, w_ref, gq_ref, gk_ref, bd_ref, cos_ref, sin_ref,
                   q_ref, k_ref, v_ref, u_ref):
    d = x_ref.shape[-1]
    aw = q_ref.shape[-1]
    kw = k_ref.shape[-1]
    bd = bd_ref[...]
    sub = x_ref.shape[0] // INPROJ_SUBTILES
    for s in range(INPROJ_SUBTILES):
        rs = slice(s * sub, (s + 1) * sub)
        hb = _modulated(x_ref[rs, :], mod_ref, pl.program_id(1), n1_ref, d).astype(BF16)
        cos = cos_ref[rs, :]
        sin = sin_ref[rs, :]
        q = _head_rms(_dot(hb, w_ref[:, 0:aw]), bd) * gq_ref[...]
        for c in range(aw // LANES):
            sl = slice(c * LANES, (c + 1) * LANES)
            q_ref[rs, sl] = (_rope128(q[:, sl], cos, sin) * ATTN_SCALE).astype(BF16)
        k = _head_rms(_dot(hb, w_ref[:, aw:aw + kw]), bd[0:kw, 0:kw]) * gk_ref[...]
        for c in range(kw // LANES):
            sl = slice(c * LANES, (c + 1) * LANES)
            k_ref[rs, sl] = _rope128(k[:, sl], cos, sin).astype(BF16)
        v_ref[rs, :] = _dot(hb, w_ref[:, aw + kw:aw + 2 * kw]).astype(BF16)
        u_ref[rs, :] = _dot(hb, w_ref[:, aw + 2 * kw:])


def _inproj(x, mod, norm1, w_all, gq, gk, bd, cos_t, sin_t, tl):
    b, l, d = x.shape
    aw, kw = gq.shape[1], gk.shape[1]
    uw = w_all.shape[1] - aw - 2 * kw
    const = lambda i, j: (0, 0)
    tok = lambda i, j: (j, i, 0)
    return pl.pallas_call(
        _inproj_kernel,
        grid=(l // tl, b),
        in_specs=[pl.BlockSpec((None, tl, d), tok),
                  pl.BlockSpec(mod.shape, const),
                  pl.BlockSpec((1, d), const),
                  pl.BlockSpec(w_all.shape, const),
                  pl.BlockSpec((1, aw), const),
                  pl.BlockSpec((1, kw), const),
                  pl.BlockSpec(bd.shape, const),
                  pl.BlockSpec((tl, LANES), lambda i, j: (i, 0)),
                  pl.BlockSpec((tl, LANES), lambda i, j: (i, 0))],
        out_specs=[pl.BlockSpec((None, tl, aw), tok),
                   pl.BlockSpec((None, tl, kw), tok),
                   pl.BlockSpec((None, tl, kw), tok),
                   pl.BlockSpec((None, tl, uw), tok)],
        out_shape=[jax.ShapeDtypeStruct((b, l, aw), BF16),
                   jax.ShapeDtypeStruct((b, l, kw), BF16),
                   jax.ShapeDtypeStruct((b, l, kw), BF16),
                   jax.ShapeDtypeStruct((b, l, uw), F32)],
        compiler_params=_cparams("arbitrary", "arbitrary"),
        name="inproj",
    )(x, mod, norm1, w_all, gq, gk, bd, cos_t, sin_t)


def _ctxkv_kernel(row, x_ref, mod_ref, n1_ref, w_ref, gk_ref, bd_ref, k_ref, v_ref):
    d = x_ref.shape[-1]
    kw = k_ref.shape[-1]
    hb = _modulated(x_ref[...], mod_ref, row, n1_ref, d).astype(BF16)
    k = _head_rms(_dot(hb, w_ref[:, 0:kw]), bd_ref[...]) * gk_ref[...]
    k_ref[...] = k.astype(BF16)
    v_ref[...] = _dot(hb, w_ref[:, kw:]).astype(BF16)


def _ctxkv(ctx, mod, ctx_row, norm1, w_kv, gk, bd):
    b, lc, d = ctx.shape
    kw = gk.shape[1]
    const = lambda i: (0, 0)
    tok = lambda i: (i, 0, 0)
    return pl.pallas_call(
        functools.partial(_ctxkv_kernel, ctx_row),
        grid=(b,),
        in_specs=[pl.BlockSpec((None, lc, d), tok),
                  pl.BlockSpec(mod.shape, const),
                  pl.BlockSpec((1, d), const),
                  pl.BlockSpec(w_kv.shape, const),
                  pl.BlockSpec((1, kw), const),
                  pl.BlockSpec(bd.shape, const)],
        out_specs=[pl.BlockSpec((None, lc, kw), tok), pl.BlockSpec((None, lc, kw), tok)],
        out_shape=[jax.ShapeDtypeStruct((b, lc, kw), BF16)] * 2,
        compiler_params=_cparams("arbitrary"),
        name="ctxkv",
    )(ctx, mod, norm1, w_kv, gk, bd)


def _attn_kernel(sink_ref, q_ref, kp_ref, kc_ref, kn_ref, vp_ref, vc_ref, vn_ref, kx_ref, vx_ref,
                 ga_ref, o_ref, acc_ref):
    i = pl.program_id(1)
    nb = pl.num_programs(1)
    tq = q_ref.shape[0]
    lc = kx_ref.shape[0]
    nk = 3 * tq + lc
    pairs = GROUP // 2
    rows = pairs * tq
    r = lax.broadcasted_iota(jnp.int32, (rows, tq), 0) % tq
    j = lax.broadcasted_iota(jnp.int32, (rows, tq), 1)
    ok_prev = j >= r + jnp.where(i > 0, 0, tq)
    ok_next = j <= r - jnp.where(i < nb - 1, 0, tq)
    lo = lax.broadcasted_iota(jnp.int32, (nk, LANES), 1) < HEAD_DIM
    zero = jnp.zeros((nk, LANES), BF16)
    top = lax.broadcasted_iota(jnp.int32, (rows, 1), 0) < tq

    ks, vs, qs, sinks = [], [], [], []
    for h in range(N_KV_HEADS):
        hs = slice(h * LANES, (h + 1) * LANES)
        kcat = jnp.concatenate([kp_ref[:, hs], kc_ref[:, hs], kn_ref[:, hs], kx_ref[:, hs]], axis=0)
        vcat = jnp.concatenate([vp_ref[:, hs], vc_ref[:, hs], vn_ref[:, hs], vx_ref[:, hs]], axis=0)
        q2 = jnp.concatenate([q_ref[:, (h * pairs + p) * LANES:(h * pairs + p + 1) * LANES]
                              for p in range(pairs)], axis=0)
        for half in range(2):
            ks.append(jnp.where(lo, kcat, zero) if half == 0 else jnp.where(lo, zero, kcat))
            vs.append(jnp.where(lo, vcat, zero) if half == 0 else jnp.where(lo, zero, vcat))
            qs.append(q2)
            sinks.append(jnp.where(top, sink_ref[h * GROUP + half], sink_ref[h * GROUP + 2 + half]))
    ss = [lax.dot_general(qq, kk, (((1,), (1,)), ((), ())), preferred_element_type=F32)
          for qq, kk in zip(qs, ks)]
    ss = [jnp.concatenate([jnp.where(ok_prev, s[:, 0:tq], NEG_INF), s[:, tq:2 * tq],
                           jnp.where(ok_next, s[:, 2 * tq:3 * tq], NEG_INF), s[:, 3 * tq:]], axis=1)
          for s in ss]
    ms = [jnp.maximum(jnp.max(s, axis=-1, keepdims=True), sk) for s, sk in zip(ss, sinks)]
    es = [jnp.exp(s - m) for s, m in zip(ss, ms)]
    dens = [jnp.sum(e, axis=-1, keepdims=True) + jnp.exp(sk - m) for e, sk, m in zip(es, sinks, ms)]
    outs = [_dot(e.astype(BF16), vv) / den for e, vv, den in zip(es, vs, dens)]
    for h in range(N_KV_HEADS):
        both = outs[2 * h] + outs[2 * h + 1]
        for p in range(pairs):
            acc_ref[:, (h * pairs + p) * LANES:(h * pairs + p + 1) * LANES] = both[p * tq:(p + 1) * tq]
    a = acc_ref[...]
    o_ref[...] = (a * _rms(a) * ga_ref[...]).astype(BF16)


def _attention(sink, q, k, v, kx, vx, ga):
    b, l, aw = q.shape
    kw = k.shape[-1]
    lc = kx.shape[1]
    tq = WINDOW
    nb = l // tq
    cur = lambda bi, i: (bi, i, 0)
    prev = lambda bi, i: (bi, jnp.maximum(i - 1, 0), 0)
    nxt = lambda bi, i: (bi, jnp.minimum(i + 1, nb - 1), 0)
    ctx = lambda bi, i: (bi, 0, 0)
    kv = lambda m: pl.BlockSpec((None, tq, kw), m)
    return pl.pallas_call(
        _attn_kernel,
        grid=(b, nb),
        in_specs=[pl.BlockSpec(memory_space=pltpu.SMEM),
                  pl.BlockSpec((None, tq, aw), cur),
                  kv(prev), kv(cur), kv(nxt), kv(prev), kv(cur), kv(nxt),
                  pl.BlockSpec((None, lc, kw), ctx), pl.BlockSpec((None, lc, kw), ctx),
                  pl.BlockSpec((1, aw), lambda bi, i: (0, 0))],
        out_specs=pl.BlockSpec((None, tq, aw), cur),
        out_shape=jax.ShapeDtypeStruct((b, l, aw), BF16),
        scratch_shapes=[pltpu.VMEM((tq, aw), F32)],
        compiler_params=_cparams("arbitrary", "arbitrary"),
        name="attn",
    )(sink, q, k, k, k, v, v, v, kx, vx, ga)


def _alternating(rows, cols):
    return (1 - 2 * (lax.broadcasted_iota(jnp.int32, (rows, cols), 0) % 2)).astype(F32)


def _hconv_kernel(u0_ref, u1_ref, u2_ref, w0_ref, w1_ref, w2_ref, b0_ref, b1_ref, b2_ref,
                  x0e_ref, x0o_ref, ze_ref, zo_ref, qr_ref, qs_ref, scr):
    l, cb = u0_ref.shape
    half = l // 2
    row = lax.broadcasted_iota(jnp.int32, (l, cb), 0)

    def conv(u_ref, w_ref, b_ref):
        u = u_ref[...]
        before = jnp.where(row == 0, 0.0, pltpu.roll(u, 1, 0))
        after = jnp.where(row == l - 1, 0.0, pltpu.roll(u, l - 1, 0))
        return b_ref[...] + before * w_ref[0:1, :] + u * w_ref[1:2, :] + after * w_ref[2:3, :]

    def parity_halves(v):
        scr[...] = v
        return scr[pl.ds(0, half, stride=2), :], scr[pl.ds(1, half, stride=2), :]

    x0e_ref[...], x0o_ref[...] = parity_halves(conv(u0_ref, w0_ref, b0_ref))
    ze, zo = parity_halves(conv(u1_ref, w1_ref, b1_ref) * conv(u2_ref, w2_ref, b2_ref))
    ze_ref[...] = ze.astype(BF16)
    zo_ref[...] = zo.astype(BF16)
    sign = _alternating(half, cb)
    qr_ref[...] = jnp.sum(ze * sign, axis=0, keepdims=True)
    qs_ref[...] = jnp.sum(zo * sign, axis=0, keepdims=True)


def _hconv(u, conv_w, conv_b, cb=LANES):
    b, l, w3 = u.shape
    c = w3 // 3
    n = c // cb
    us = [pl.BlockSpec((None, l, cb), lambda bi, j, g=g: (bi, 0, g * n + j)) for g in range(3)]
    ws = [pl.BlockSpec((3, cb), lambda bi, j, g=g: (0, g * n + j)) for g in range(3)]
    bs = [pl.BlockSpec((1, cb), lambda bi, j, g=g: (0, g * n + j)) for g in range(3)]
    out = lambda bi, j: (bi, 0, j)
    seq = pl.BlockSpec((None, l // 2, cb), out)
    vec = pl.BlockSpec((None, 1, cb), out)
    return pl.pallas_call(
        _hconv_kernel,
        grid=(b, n),
        in_specs=us + ws + bs,
        out_specs=[seq, seq, seq, seq, vec, vec],
        out_shape=[jax.ShapeDtypeStruct((b, l // 2, c), F32)] * 2 + [jax.ShapeDtypeStruct((b, l // 2, c), BF16)] * 2
        + [jax.ShapeDtypeStruct((b, 1, c), F32)] * 2,
        scratch_shapes=[pltpu.VMEM((l, cb), F32)],
        compiler_params=_cparams("arbitrary", "arbitrary"),
        name="hconv",
    )(u, u, u, conv_w, conv_w, conv_w, conv_b, conv_b, conv_b)


def _filter_kernel(fe_ref, fo_ref, w1_ref, b1_ref, w2_ref, b2_ref, w3_ref, b3_ref, fr_ref, w4f_ref, w4b_ref,
                   dl_ref, kpe_ref, kpo_ref, kme_ref, kmo_ref, krq_ref, ksq_ref, he_scr, ho_scr):
    half, cf = kpe_ref.shape
    l = 2 * half

    @pl.when(pl.program_id(0) == 0)
    def _():
        fr = fr_ref[...]
        for f_ref, h_scr in ((fe_ref, he_scr), (fo_ref, ho_scr)):
            h = jnp.sin(fr * (_dot3(f_ref[...], w1_ref[...]) + b1_ref[...]))
            h = jnp.sin(fr * (_dot3(h, w2_ref[...]) + b2_ref[...]))
            h_scr[...] = jnp.sin(fr * (_dot3(h, w3_ref[...]) + b3_ref[...]))

    row = lax.broadcasted_iota(jnp.int32, (half, cf), 0)

    def taps(h_scr, parity):
        t = (2 * row + parity).astype(F32) / (l - 1)
        decay = jnp.exp(-t * dl_ref[...])
        h = h_scr[...]
        return _dot3(h, w4f_ref[...]) * decay, _dot3(h, w4b_ref[...]) * decay

    kfe, kbe = taps(he_scr, 0)
    kbe = jnp.where(row == 0, 0.0, kbe)
    kfo, kbo = taps(ho_scr, 1)
    nrm = lax.rsqrt(jnp.sum(kfe * kfe + kbe * kbe + kfo * kfo + kbo * kbo, axis=0, keepdims=True) + EPS)
    kpe = (kfe + kbe) * nrm
    kmo = (kfo - kbo) * nrm
    kpe_ref[...] = kpe.astype(BF16)
    kpo_ref[...] = ((kfo + kbo) * nrm).astype(BF16)
    kme_ref[...] = ((kfe - kbe) * nrm).astype(BF16)
    kmo_ref[...] = kmo.astype(BF16)
    sign = _alternating(half, cf)
    krq_ref[...] = jnp.sum(kpe * sign, axis=0, keepdims=True)
    ksq_ref[...] = jnp.sum(kmo * sign, axis=0, keepdims=True)


def _filter(fe, fo, w1, b1, w2, b2, w3, b3, fr, w4, deltas, cf=LANES):
    half, zw = fe.shape
    ffn = w2.shape[0]
    c = w4.shape[1] // 2
    n = c // cf
    const = lambda j: (0, 0)
    col = lambda j: (0, j)
    vec = pl.BlockSpec((1, ffn), const)
    mat = pl.BlockSpec((ffn, ffn), const)
    return pl.pallas_call(
        _filter_kernel,
        grid=(n,),
        in_specs=[pl.BlockSpec((half, zw), const), pl.BlockSpec((half, zw), const),
                  pl.BlockSpec((zw, ffn), const), vec, mat, vec, mat, vec, vec,
                  pl.BlockSpec((ffn, cf), col),
                  pl.BlockSpec((ffn, cf), lambda j: (0, n + j)),
                  pl.BlockSpec((1, cf), col)],
        out_specs=[pl.BlockSpec((half, cf), col)] * 4 + [pl.BlockSpec((1, cf), col)] * 2,
        out_shape=[jax.ShapeDtypeStruct((half, c), BF16)] * 4 + [jax.ShapeDtypeStruct((1, c), F32)] * 2,
        scratch_shapes=[pltpu.VMEM((half, ffn), F32)] * 2,
        compiler_params=_cparams("arbitrary"),
        name="filt",
    )(fe, fo, w1, b1, w2, b2, w3, b3, fr, w4, w4, deltas)


DFT_FINE = 64


def _dftgen_kernel(ca_ref, sa_ref, cb_ref, sb_ref, c_ref, s_ref):
    cb = cb_ref[...]
    sb = sb_ref[...]
    for a in range(ca_ref.shape[0]):
        ca = ca_ref[a:a + 1, :]
        sa = sa_ref[a:a + 1, :]
        rs = slice(a * DFT_FINE, (a + 1) * DFT_FINE)
        c_ref[rs, :] = (ca * cb - sa * sb).astype(BF16)
        s_ref[rs, :] = (sa * cb + ca * sb).astype(BF16)


def _dftgen(coarse, fine, l, tf, name):
    tabs = [jnp.asarray(fn((k % (2 * l)) * (math.pi / l)), F32) for k in (coarse, fine) for fn in (np.cos, np.sin)]
    rows, width = coarse.shape[0] * DFT_FINE, coarse.shape[1]
    na = tf // DFT_FINE
    tile = lambda i: (i, 0)
    const = lambda i: (0, 0)
    return pl.pallas_call(
        _dftgen_kernel,
        grid=(rows // tf,),
        in_specs=[pl.BlockSpec((na, width), tile), pl.BlockSpec((na, width), tile),
                  pl.BlockSpec((DFT_FINE, width), const), pl.BlockSpec((DFT_FINE, width), const)],
        out_specs=[pl.BlockSpec((tf, width), tile), pl.BlockSpec((tf, width), tile)],
        out_shape=[jax.ShapeDtypeStruct((rows, width), BF16)] * 2,
        compiler_params=_cparams("arbitrary"),
        name=name,
    )(*tabs)


def _dft_matrices(l, tf):
    half = l // 2
    lo = np.arange(half, dtype=np.int64)[None, :]
    a = np.arange(half // DFT_FINE, dtype=np.int64)[:, None] * DFT_FINE
    i = np.arange(DFT_FINE, dtype=np.int64)[:, None]
    ce, se = _dftgen(a * 2 * lo, i * 2 * lo, l, tf, "dft_even")
    co, so = _dftgen(a * (2 * lo + 1), i * (2 * lo + 1), l, tf, "dft_odd")
    cot, sot = _dftgen(2 * a * lo, (2 * i + 1) * lo, l, tf, "dft_odd_t")
    return ce, se, co, so, cot, sot


def _kspec_kernel(n_fft, ce_ref, se_ref, co_ref, so_ref, kpe_ref, kpo_ref, kme_ref, kmo_ref,
                  krl_ref, krm_ref, ksl_ref, ksm_ref):
    tf = ce_ref.shape[0]
    f = pl.program_id(0) * tf + lax.broadcasted_iota(jnp.int32, (tf, 1), 0)
    w = jnp.where(f == 0, 1.0 / n_fft, 2.0 / n_fft)
    ce = _dot(ce_ref[...], kpe_ref[...])
    co = _dot(co_ref[...], kpo_ref[...])
    se = _dot(se_ref[...], kme_ref[...])
    so = _dot(so_ref[...], kmo_ref[...])
    krl_ref[...] = (ce + co) * w
    krm_ref[...] = (ce - co) * w
    ksl_ref[...] = (so + se) * w
    ksm_ref[...] = (so - se) * w


def _kspec(mats, kpe, kpo, kme, kmo, tf):
    half, c = kpe.shape
    const = lambda i: (0, 0)
    tile = lambda i: (i, 0)
    return pl.pallas_call(
        functools.partial(_kspec_kernel, 4 * half),
        grid=(half // tf,),
        in_specs=[pl.BlockSpec((tf, half), tile)] * 4 + [pl.BlockSpec((half, c), const)] * 4,
        out_specs=[pl.BlockSpec((tf, c), tile)] * 4,
        out_shape=[jax.ShapeDtypeStruct((half, c), F32)] * 4,
        compiler_params=_cparams("arbitrary"),
        name="kspec",
    )(*mats, kpe, kpo, kme, kmo)


def _hfwd_kernel(ce_ref, se_ref, co_ref, so_ref, ze_ref, zo_ref, krl_ref, krm_ref, ksl_ref, ksm_ref,
                 p1_ref, p2_ref, p3_ref, p4_ref):
    ze = ze_ref[...]
    zo = zo_ref[...]
    ce = _dot(ce_ref[...], ze)
    co = _dot(co_ref[...], zo)
    se = _dot(se_ref[...], ze)
    so = _dot(so_ref[...], zo)

    def product(zr, zs, kr_ref, ks_ref):
        kr = kr_ref[...]
        ks = ks_ref[...]
        return zr * kr - zs * ks, zr * ks + zs * kr

    al, bl = product(ce + co, so + se, krl_ref, ksl_ref)
    am, bm = product(ce - co, so - se, krm_ref, ksm_ref)
    p1_ref[...] = (al + am).astype(BF16)
    p2_ref[...] = (bl - bm).astype(BF16)
    p3_ref[...] = (al - am).astype(BF16)
    p4_ref[...] = (bl + bm).astype(BF16)


def _hfwd(mats, ze, zo, spec, tf):
    b, half, c = ze.shape
    tile = lambda i, bi: (i, 0)
    seq = lambda i, bi: (bi, 0, 0)
    out = lambda i, bi: (bi, i, 0)
    return pl.pallas_call(
        _hfwd_kernel,
        grid=(half // tf, b),
        in_specs=[pl.BlockSpec((tf, half), tile)] * 4 + [pl.BlockSpec((None, half, c), seq)] * 2
        + [pl.BlockSpec((tf, c), tile)] * 4,
        out_specs=[pl.BlockSpec((None, tf, c), out)] * 4,
        out_shape=[jax.ShapeDtypeStruct((b, half, c), BF16)] * 4,
        compiler_params=_cparams("arbitrary", "arbitrary"),
        name="hfwd",
    )(*mats, ze, zo, *spec)


def _hinv_kernel(ce_ref, se_ref, cot_ref, sot_ref, p1_ref, p2_ref, p3_ref, p4_ref, x0e_ref, x0o_ref,
                 ze_ref, zo_ref, qr_ref, qs_ref, krq_ref, ksq_ref, bias_ref, g_ref, o_ref, scr):
    tt, half = ce_ref.shape
    n_fft = 4 * half
    ye = _dot(ce_ref[...], p1_ref[...]) + _dot(se_ref[...], p2_ref[...])
    yo = _dot(cot_ref[...], p3_ref[...]) + _dot(sot_ref[...], p4_ref[...])
    sign = (1 - 2 * ((pl.program_id(0) * tt + lax.broadcasted_iota(jnp.int32, (tt, 1), 0)) % 2)).astype(F32)
    qr, qs, krq, ksq = qr_ref[...], qs_ref[...], krq_ref[...], ksq_ref[...]
    ye = ye + sign * ((qr * krq - qs * ksq) * (2.0 / n_fft))
    yo = yo + sign * ((qr * ksq + qs * krq) * (2.0 / n_fft))

    def finish(y, x0_ref, z_ref):
        hy = x0_ref[...] * (y + z_ref[...].astype(F32) * bias_ref[...])
        return hy * _rms(hy) * g_ref[...]

    he = finish(ye, x0e_ref, ze_ref)
    ho = finish(yo, x0o_ref, zo_ref)
    for c in range(scr.shape[0]):
        cs = slice(c * LANES, (c + 1) * LANES)
        scr[c, pl.ds(0, tt, stride=2), :] = he[:, cs]
        scr[c, pl.ds(1, tt, stride=2), :] = ho[:, cs]
        o_ref[:, cs] = scr[c].astype(BF16)


def _hinv(ce, se, cot, sot, ps, x0e, x0o, ze, zo, qr, qs, krq, ksq, bias, gain, tt):
    b, half, c = ze.shape
    tile = lambda i, bi: (i, 0)
    full = lambda i, bi: (bi, 0, 0)
    tok = lambda i, bi: (bi, i, 0)
    const = lambda i, bi: (0, 0)
    vec = pl.BlockSpec((1, c), const)
    return pl.pallas_call(
        _hinv_kernel,
        grid=(half // tt, b),
        in_specs=[pl.BlockSpec((tt, half), tile)] * 4 + [pl.BlockSpec((None, half, c), full)] * 4
        + [pl.BlockSpec((None, tt, c), tok)] * 4 + [pl.BlockSpec((None, 1, c), full)] * 2 + [vec] * 4,
        out_specs=pl.BlockSpec((None, 2 * tt, c), tok),
        out_shape=jax.ShapeDtypeStruct((b, 2 * half, c), BF16),
        scratch_shapes=[pltpu.VMEM((c // LANES, 2 * tt, LANES), F32)],
        compiler_params=_cparams("arbitrary", "arbitrary"),
        name="hinv",
    )(ce, se, cot, sot, *ps, x0e, x0o, ze, zo, qr, qs, krq, ksq, bias, gain)


def _mixout_kernel(an_ref, yn_ref, x_ref, mod_ref, wo_ref, n2_ref, wr_ref, br_ref, tri_ref,
                   xn_ref, h2_ref, route_ref, cnt_ref, carry):
    bi = pl.program_id(0)
    d = x_ref.shape[-1]
    half = an_ref.shape[-1]
    tl = x_ref.shape[0]

    @pl.when((bi == 0) & (pl.program_id(1) == 0))
    def _():
        carry[...] = jnp.zeros_like(carry)

    mix = _dot(an_ref[...], wo_ref[0:half, :]) + _dot(yn_ref[...], wo_ref[half:, :])
    g1 = mod_ref[pl.ds(bi, 1), 2 * d:3 * d]
    sh2 = mod_ref[pl.ds(bi, 1), 3 * d:4 * d]
    sc2 = mod_ref[pl.ds(bi, 1), 4 * d:5 * d]
    xn = x_ref[...] + g1 * mix
    xn_ref[...] = xn
    h2 = (xn * _rms(xn)) * n2_ref[...] * (1 + sc2) + sh2
    for j in range(d // LANES):
        _token_chunk(h2_ref, 0, tl, j)[...] = h2[:, j * LANES:(j + 1) * LANES]

    logits = _dot3(h2, wr_ref[...]) + br_ref[...]
    lane = lax.broadcasted_iota(jnp.int32, (tl, LANES), 1).astype(F32)
    vals, idxs, sels = [], [], []
    cur = logits
    for _ in range(TOP_K):
        m = jnp.max(cur, axis=-1, keepdims=True)
        idx = jnp.min(jnp.where(cur == m, lane, float(LANES)), axis=-1, keepdims=True)
        sel = lane == idx
        vals.append(m)
        idxs.append(idx)
        sels.append(sel)
        cur = jnp.where(sel, -jnp.inf, cur)
    es = [jnp.exp(v - vals[0]) for v in vals]
    den = es[0] + es[1] + es[2] + es[3]
    hot = sum(s.astype(F32) for s in sels)
    before = _dot(tri_ref[...], hot.astype(BF16)) + carry[...]
    carry[...] = carry[...] + jnp.sum(hot, axis=0, keepdims=True)
    cnt_ref[...] = carry[...]
    route = jnp.zeros((tl, LANES), F32)
    for k in range(TOP_K):
        pos = jnp.sum(jnp.where(sels[k], before, 0.0), axis=-1, keepdims=True)
        route = jnp.where(lane == k, idxs[k], route)
        route = jnp.where(lane == TOP_K + k, es[k] / den, route)
        route = jnp.where(lane == 2 * TOP_K + k, pos, route)
    route_ref[...] = route


def _mixout(an, yn, x, mod, w_out, norm2, wr, br, tri, tl):
    b, l, d = x.shape
    half = an.shape[-1]
    nt = l // tl
    tok = lambda bi, i: (bi, i, 0)
    const = lambda bi, i: (0, 0)
    flat = lambda bi, i: (bi * nt + i, 0)
    return pl.pallas_call(
        _mixout_kernel,
        grid=(b, nt),
        in_specs=[pl.BlockSpec((None, tl, half), tok), pl.BlockSpec((None, tl, half), tok),
                  pl.BlockSpec((None, tl, d), tok),
                  pl.BlockSpec(mod.shape, const), pl.BlockSpec(w_out.shape, const),
                  pl.BlockSpec((1, d), const), pl.BlockSpec(wr.shape, const),
                  pl.BlockSpec((1, LANES), const), pl.BlockSpec((tl, tl), const)],
        out_specs=[pl.BlockSpec((None, tl, d), tok),
                   pl.BlockSpec((tl * SUBLANES, LANES), flat),
                   pl.BlockSpec((tl, LANES), flat),
                   pl.BlockSpec((1, LANES), const)],
        out_shape=[jax.ShapeDtypeStruct((b, l, d), F32),
                   jax.ShapeDtypeStruct((b * l * SUBLANES, LANES), F32),
                   jax.ShapeDtypeStruct((b * l, LANES), F32),
                   jax.ShapeDtypeStruct((1, LANES), F32)],
        scratch_shapes=[pltpu.VMEM((1, LANES), F32)],
        compiler_params=_cparams("arbitrary", "arbitrary"),
        name="mixout",
    )(an, yn, x, mod, w_out, norm2, wr, br, tri)


def _token_chunk(ref, first_token, n, j):
    return ref.at[pl.ds(first_token * SUBLANES + j, n, stride=SUBLANES), :]


def _token_copy(idx_ref, r, src_hbm, dst, sem):
    first_row = lambda tok: tok * SUBLANES if isinstance(tok, int) else pl.multiple_of(tok * SUBLANES, SUBLANES)
    return pltpu.make_async_copy(src_hbm.at[pl.ds(first_row(idx_ref[0, r]), SUBLANES)],
                                 dst.at[pl.ds(first_row(r), SUBLANES)], sem)


def _gather_tokens(idx_ref, n, src_hbm, dst, sem):
    def body(r, carry):
        _token_copy(idx_ref, 2 * r, src_hbm, dst, sem).start(priority=0)
        _token_copy(idx_ref, 2 * r + 1, src_hbm, dst, sem).start(priority=1)
        return carry
    lax.fori_loop(0, n // 2, body, 0, unroll=4)


def _wait_tokens(src_hbm, dst, sem):
    pltpu.make_async_copy(src_hbm.at[pl.ds(0, dst.shape[0])], dst, sem).wait()


EXPERT_STAGES = 4
def _experts_kernel(be_ref, nvb_ref, tok_ref, tokn_ref, h2_hbm, wgu_ref, bgu_ref, wdn_ref, bdn_ref,
                    ys_ref, xbuf, xs_bf, wgu_bf, wdn_bf, sem):
    i = pl.program_id(0)
    nvb = nvb_ref[0]
    rows = xbuf.shape[1] // SUBLANES
    nchunk = wgu_ref.shape[0] // LANES
    ff = wdn_ref.shape[0]
    fc = ff // EXPERT_STAGES
    slot = i % 2

    @pl.when(i == 0)
    def _():
        _gather_tokens(tok_ref, rows, h2_hbm, xbuf.at[0], sem.at[0])

    @pl.when(i + 1 < nvb)
    def _():
        _gather_tokens(tokn_ref, rows, h2_hbm, xbuf.at[1 - slot], sem.at[1 - slot])

    @pl.when((i == 0) | (be_ref[i] != be_ref[jnp.maximum(i - 1, 0)]))
    def _():
        wgu_bf[...] = wgu_ref[...].astype(BF16)
        wdn_bf[...] = wdn_ref[...].astype(BF16)

    @pl.when(i < nvb)
    def _():
        _wait_tokens(h2_hbm, xbuf.at[slot], sem.at[slot])
        for j in range(nchunk):
            xs_bf[:, j * LANES:(j + 1) * LANES] = _token_chunk(xbuf.at[slot], 0, rows, j)[...].astype(BF16)
        xs = xs_bf[...]
        def gate_up(c):
            cg = slice(c * fc, (c + 1) * fc)
            cu = slice(ff + c * fc, ff + (c + 1) * fc)
            return _dot(xs, wgu_bf[:, cg]) + bgu_ref[:, cg], _dot(xs, wgu_bf[:, cu]) + bgu_ref[:, cu]

        acc = None
        nxt = gate_up(0)
        for c in range(EXPERT_STAGES):
            gate, up = nxt
            if c + 1 < EXPERT_STAGES:
                nxt = gate_up(c + 1)
            gate = jnp.minimum(gate, SWIGLU_LIMIT)
            up = jnp.clip(up, -SWIGLU_LIMIT, SWIGLU_LIMIT)
            act = ((up + 1) * (gate * jax.nn.sigmoid(SWIGLU_ALPHA * gate))).astype(BF16)
            part = _dot(act, wdn_bf[c * fc:(c + 1) * fc, :])
            acc = part if acc is None else acc + part
        y = acc + bdn_ref[...]
        for j in range(nchunk):
            _token_chunk(ys_ref, 0, rows, j)[...] = y[:, j * LANES:(j + 1) * LANES]

    @pl.when(i >= nvb)
    def _():
        ys_ref[...] = jnp.zeros_like(ys_ref)


def _experts(block_e, nvb, slot_tok, h2, wgu, bgu, wdn, bdn, rows):
    n_blk = block_e.shape[0]
    d, ff2 = wgu.shape[1:]
    ff = wdn.shape[1]
    tok = lambda k: pl.BlockSpec((None, 1, rows), lambda i, be, nv: (jnp.minimum(i + k, n_blk - 1), 0, 0),
                                 memory_space=pltpu.SMEM)
    expert = lambda i, be, nv: (be[i], 0, 0)
    grid_spec = pltpu.PrefetchScalarGridSpec(
        num_scalar_prefetch=2,
        grid=(n_blk,),
        in_specs=[tok(0), tok(1),
                  pl.BlockSpec(memory_space=pl.ANY),
                  pl.BlockSpec((None, d, ff2), expert), pl.BlockSpec((None, 1, ff2), expert),
                  pl.BlockSpec((None, ff, d), expert), pl.BlockSpec((None, 1, d), expert)],
        out_specs=pl.BlockSpec((rows * SUBLANES, LANES), lambda i, be, nv: (i, 0)),
        scratch_shapes=[pltpu.VMEM((2, rows * SUBLANES, LANES), F32), pltpu.VMEM((rows, d), BF16),
                        pltpu.VMEM((d, ff2), BF16), pltpu.VMEM((ff, d), BF16),
                        pltpu.SemaphoreType.DMA((2,))],
    )
    return pl.pallas_call(
        _experts_kernel,
        grid_spec=grid_spec,
        out_shape=jax.ShapeDtypeStruct((n_blk * rows * SUBLANES, LANES), F32),
        compiler_params=_cparams("arbitrary"),
        name="experts",
    )(block_e, nvb, slot_tok, slot_tok, h2, wgu, bgu, wdn, bdn)


def _combine_kernel(dest_ref, destn_ref, ys_hbm, xn_ref, route_ref, g2_ref, o_ref, ybuf, sem):
    i = pl.program_id(0)
    n = pl.num_programs(0)
    tc = xn_ref.shape[0]
    slot = i % 2

    @pl.when(i == 0)
    def _():
        _gather_tokens(dest_ref, TOP_K * tc, ys_hbm, ybuf.at[0], sem.at[0])

    @pl.when(i + 1 < n)
    def _():
        _gather_tokens(destn_ref, TOP_K * tc, ys_hbm, ybuf.at[1 - slot], sem.at[1 - slot])

    _wait_tokens(ys_hbm, ybuf.at[slot], sem.at[slot])
    route = route_ref[...]
    gates = [jnp.broadcast_to(route[:, TOP_K + k:TOP_K + k + 1], (tc, LANES)) for k in range(TOP_K)]
    for j in range(xn_ref.shape[1] // LANES):
        cs = slice(j * LANES, (j + 1) * LANES)
        acc = gates[0] * _token_chunk(ybuf.at[slot], 0, tc, j)[...]
        for k in range(1, TOP_K):
            acc = acc + gates[k] * _token_chunk(ybuf.at[slot], k * tc, tc, j)[...]
        o_ref[:, cs] = xn_ref[:, cs] + g2_ref[:, cs] * acc


def _combine(dest_kmaj, ys, xn, route, g2, tc):
    t, d = xn.shape
    n = t // tc
    per_batch = t // g2.shape[0] // tc
    return pl.pallas_call(
        _combine_kernel,
        grid=(n,),
        in_specs=[pl.BlockSpec((None, 1, TOP_K * tc), lambda i: (i, 0, 0), memory_space=pltpu.SMEM),
                  pl.BlockSpec((None, 1, TOP_K * tc), lambda i: (jnp.minimum(i + 1, n - 1), 0, 0),
                               memory_space=pltpu.SMEM),
                  pl.BlockSpec(memory_space=pl.ANY),
                  pl.BlockSpec((tc, d), lambda i: (i, 0)),
                  pl.BlockSpec((tc, LANES), lambda i: (i, 0)),
                  pl.BlockSpec((None, 1, d), lambda i: (i // per_batch, 0, 0))],
        out_specs=pl.BlockSpec((tc, d), lambda i: (i, 0)),
        out_shape=jax.ShapeDtypeStruct((t, d), F32),
        scratch_shapes=[pltpu.VMEM((2, TOP_K * tc * SUBLANES, LANES), F32), pltpu.SemaphoreType.DMA((2,))],
        compiler_params=_cparams("arbitrary"),
        name="combine",
    )(dest_kmaj, dest_kmaj, ys, xn, route, g2)


def _rope_tables(l):
    n_freq = HEAD_DIM // 4
    inv_freq = ROPE_BASE ** (-np.arange(n_freq, dtype=np.float64) / n_freq)
    tpos = np.arange(l)
    lane = np.arange(LANES) % HEAD_DIM
    pos = np.where(lane[None, :] < HEAD_DIM // 2, (tpos // GRID_W)[:, None], (tpos % GRID_W)[:, None])
    ang = pos * inv_freq[lane % n_freq][None, :]
    sign = np.where(lane % (2 * n_freq) < n_freq, -1.0, 1.0)[None, :]
    return jnp.asarray(np.cos(ang), F32), jnp.asarray(np.sin(ang) * sign, F32)


def _filter_features(l, width):
    t = np.linspace(0.0, 1.0, l)[:, None]
    w = 2.0 * math.pi * np.arange(l)[:, None] / l
    bands = np.linspace(1e-4, POS_BANDS - 1, POS_BANDS)[None, :]
    z = np.concatenate([t, np.cos(bands * w), -np.sin(bands * w)], axis=-1)
    return np.pad(z, ((0, 0), (0, width - z.shape[1]))).astype(np.float32)


def kernel(x, c, ctx, c_ctx, w_mod, b_mod, norm1, norm2, w_in, q_norm, k_norm, sink, conv_w, conv_b,
           filt_w1, filt_b1, filt_w2, filt_b2, filt_w3, filt_b3, filt_w4, filt_freq, filt_bias,
           attn_out_norm, hyena_out_norm, w_out, w_router, b_router, w_gu, b_gu, w_down, b_down):
    assert w_mod.shape[0] == 1, "single-layer configuration"
    b, l, d = x.shape
    assert d == SUBLANES * LANES, "token-tiled rows assume one (8, 128) tile per token"
    t = b * l
    aw = N_Q_HEADS * HEAD_DIM
    kvw = N_KV_HEADS * HEAD_DIM
    hw = conv_w.shape[-1] // 3
    tl = min(512, l)

    ctx_row = b
    pad_rows = -(b + 1) % SUBLANES
    c_all = jnp.concatenate([c, c_ctx[None], jnp.zeros((pad_rows, d), F32)], axis=0)
    mod = _adaln(c_all, w_mod[0], b_mod[0])

    w = w_in[0]
    wq, wk, wv, wu = w[:, :aw], w[:, aw:aw + kvw], w[:, aw + kvw:aw + 2 * kvw], w[:, aw + 2 * kvw:]
    dup = lambda m: jnp.concatenate([m[:, h * HEAD_DIM:(h + 1) * HEAD_DIM]
                                     for h in range(N_KV_HEADS) for _ in range(2)], axis=1)
    w_all = jnp.concatenate([wq, dup(wk), dup(wv), wu], axis=1).astype(BF16)
    w_kv = jnp.concatenate([dup(wk), dup(wv)], axis=1).astype(BF16)
    gq = jnp.tile(q_norm[0], N_Q_HEADS)[None]
    gk = jnp.tile(k_norm[0], 2 * N_KV_HEADS)[None]
    bd = jnp.asarray(np.kron(np.eye(N_Q_HEADS), np.full((HEAD_DIM, HEAD_DIM), 1.0 / HEAD_DIM)), BF16)
    cos_t, sin_t = _rope_tables(l)

    q, k, v, u = _inproj(x, mod, norm1, w_all, gq, gk, bd, cos_t, sin_t, tl)
    kx, vx = _ctxkv(ctx, mod, ctx_row, norm1, w_kv, gk, bd[:2 * kvw, :2 * kvw])
    an = _attention(sink[0], q, k, v, kx, vx, attn_out_norm)

    x0e, x0o, ze, zo, qr, qs = _hconv(u, conv_w[0], conv_b)
    ffn = filt_w2.shape[-1]
    zf = _filter_features(l, ffn)
    w1 = jnp.pad(filt_w1[0], ((0, ffn - POS_EMB_DIM), (0, 0)))
    deltas = jnp.asarray(np.linspace(MIN_DECAY, MAX_DECAY, hw)[None, :], F32)
    kpe, kpo, kme, kmo, krq, ksq = _filter(jnp.asarray(zf[0::2]), jnp.asarray(zf[1::2]), w1, filt_b1, filt_w2[0],
                                           filt_b2, filt_w3[0], filt_b3, filt_freq, filt_w4[0], deltas)
    tf = min(512, l // 2)
    ce, se, co, so, cot, sot = _dft_matrices(l, tf)
    spec = _kspec((ce, se, co, so), kpe, kpo, kme, kmo, tf)
    ps = _hfwd((ce, se, co, so), ze, zo, spec, tf)
    yn = _hinv(ce, se, cot, sot, ps, x0e, x0o, ze, zo, qr, qs, krq, ksq, filt_bias, hyena_out_norm,
               min(256, l // 2))

    wr = jnp.pad(w_router[0], ((0, 0), (0, LANES - N_EXPERTS)))
    br = jnp.concatenate([b_router[0], jnp.full((LANES - N_EXPERTS,), NEG_INF, F32)])[None]
    tri = jnp.asarray(np.tril(np.ones((tl, tl)), -1), BF16)
    xn, h2, route, cnt = _mixout(an, yn, x, mod, w_out[0].astype(BF16), norm2, wr, br, tri, tl)

    rows = EXPERT_ROWS
    a_tot = t * TOP_K
    n_blk = -(-a_tot // rows) + N_EXPERTS
    idx = route[:, 0:TOP_K].astype(jnp.int32)
    pos = route[:, 2 * TOP_K:3 * TOP_K].astype(jnp.int32)
    counts = cnt[0, :N_EXPERTS].astype(jnp.int32)
    pcounts = (counts + rows - 1) // rows * rows
    pends = jnp.cumsum(pcounts)
    pstarts = pends - pcounts
    dest = pstarts[idx] + pos
    block_start = jnp.arange(n_blk, dtype=jnp.int32) * rows
    block_e = jnp.minimum(jnp.sum(pends[None, :] <= block_start[:, None], axis=1), N_EXPERTS - 1).astype(jnp.int32)
    order = jnp.argsort(idx.reshape(-1), stable=True).astype(jnp.int32)
    starts = jnp.cumsum(counts) - counts
    offset = (block_start - pstarts[block_e])[:, None] + jnp.arange(rows, dtype=jnp.int32)[None, :]
    src = starts[block_e][:, None] + jnp.minimum(offset, counts[block_e][:, None] - 1)
    slot_tok = order[jnp.clip(src, 0, a_tot - 1)] // TOP_K

    nvb = (pends[-1:] // rows).astype(jnp.int32)
    ys = _experts(block_e, nvb, slot_tok.reshape(n_blk, 1, rows), h2,
                  w_gu[0], b_gu[0][:, None, :], w_down[0], b_down[0][:, None, :], rows)

    tc = min(COMBINE_ROWS, l)
    dest_kmaj = dest.reshape(t // tc, tc, TOP_K).transpose(0, 2, 1).reshape(t // tc, 1, TOP_K * tc)
    g2 = mod[:b, None, 5 * d:6 * d]
    out = _combine(dest_kmaj, ys, xn.reshape(t, d), route, g2, tc)
    return out.reshape(b, l, d)
```

```python
import functools
import math

import numpy as np
import jax
import jax.numpy as jnp
from jax import lax
from jax.experimental import pallas as pl
from jax.experimental.pallas import tpu as pltpu

F32 = jnp.float32
BF16 = jnp.bfloat16

LANES = 128
SUBLANES = 8
VMEM_LIMIT = 56 * 1024 * 1024

HEAD_DIM = 64
N_Q_HEADS = 8
N_KV_HEADS = 2
GROUP = N_Q_HEADS // N_KV_HEADS
GRID_W = 64
WINDOW = 128
ROPE_BASE = 10000.0
ATTN_SCALE = HEAD_DIM ** -0.5
POS_EMB_DIM = 33
POS_BANDS = (POS_EMB_DIM - 1) // 2
DECAY_TARGET = 1e-2
MAX_DECAY = -math.log(DECAY_TARGET) / 0.3
MIN_DECAY = -math.log(DECAY_TARGET) / 1.5
N_EXPERTS = 32
TOP_K = 4
SWIGLU_LIMIT = 7.0
SWIGLU_ALPHA = 1.702
EPS = 1e-6
NEG_INF = -1e30

EXPERT_ROWS = 256
COMBINE_ROWS = 256


def _cparams(*sem):
    return pltpu.CompilerParams(dimension_semantics=sem, vmem_limit_bytes=VMEM_LIMIT)


def _split(a):
    hi = a.astype(BF16)
    lo = (a - hi.astype(F32)).astype(BF16)
    return hi, lo


def _dot(a, b):
    return jnp.dot(a, b, preferred_element_type=F32)


def _dot3(a, b):
    ah, al = _split(a)
    bh, bl = _split(b)
    return _dot(ah, bh) + _dot(al, bh) + _dot(ah, bl)


def _rms(x):
    return lax.rsqrt(jnp.mean(x * x, axis=-1, keepdims=True) + EPS)


def _adaln_kernel(c_ref, w_ref, b_ref, o_ref):
    c = c_ref[...]
    o_ref[...] = _dot3(c * jax.nn.sigmoid(c), w_ref[...]) + b_ref[...]


def _adaln(c_all, w_mod, b_mod):
    rows, d = c_all.shape
    n = w_mod.shape[1]
    tn = 1024
    return pl.pallas_call(
        _adaln_kernel,
        grid=(n // tn,),
        in_specs=[pl.BlockSpec((rows, d), lambda j: (0, 0)),
                  pl.BlockSpec((d, tn), lambda j: (0, j)),
                  pl.BlockSpec((1, tn), lambda j: (0, j))],
        out_specs=pl.BlockSpec((rows, tn), lambda j: (0, j)),
        out_shape=jax.ShapeDtypeStruct((rows, n), F32),
        compiler_params=_cparams("arbitrary"),
        name="adaln",
    )(c_all, w_mod, b_mod[None])


def _head_rms(x, bd):
    hi, lo = _split(x * x)
    return x * lax.rsqrt(_dot(hi, bd) + _dot(lo, bd) + EPS)


def _rope128(x, cos, sin):
    lane = lax.broadcasted_iota(jnp.int32, x.shape, 1)
    partner = jnp.where(lane % 32 < 16, pltpu.roll(x, LANES - 16, 1), pltpu.roll(x, 16, 1))
    return x * cos + partner * sin


def _modulated(x, mod_ref, row, norm_ref, d):
    sh = mod_ref[pl.ds(row, 1), 0:d]
    sc = mod_ref[pl.ds(row, 1), d:2 * d]
    return (x * _rms(x)) * norm_ref[...] * (1 + sc) + sh


INPROJ_SUBTILES = 2


def _inproj_kernel(x_ref, mod_ref, n1_ref, w_ref, gq_ref, gk_ref, bd_ref, cos_ref, sin_ref,
                   q_ref, k_ref, v_ref, u_ref):
    d = x_ref.shape[-1]
    aw = q_ref.shape[-1]
    kw = k_ref.shape[-1]
    bd = bd_ref[...]
    sub = x_ref.shape[0] // INPROJ_SUBTILES
    for s in range(INPROJ_SUBTILES):
        rs = slice(s * sub, (s + 1) * sub)
        hb = _modulated(x_ref[rs, :], mod_ref, pl.program_id(1), n1_ref, d).astype(BF16)
        cos = cos_ref[rs, :]
        sin = sin_ref[rs, :]
        q = _head_rms(_dot(hb, w_ref[:, 0:aw]), bd) * gq_ref[...]
        for c in range(aw // LANES):
            sl = slice(c * LANES, (c + 1) * LANES)
            q_ref[rs, sl] = (_rope128(q[:, sl], cos, sin) * ATTN_SCALE).astype(BF16)
        k = _head_rms(_dot(hb, w_ref[:, aw:aw + kw]), bd[0:kw, 0:kw]) * gk_ref[...]
        for c in range(kw // LANES):
            sl = slice(c * LANES, (c + 1) * LANES)
            k_ref[rs, sl] = _rope128(k[:, sl], cos, sin).astype(BF16)
        v_ref[rs, :] = _dot(hb, w_ref[:, aw + kw:aw + 2 * kw]).astype(BF16)
        u_ref[rs, :] = _dot(hb, w_ref[:, aw + 2 * kw:])


def _inproj(x, mod, norm1, w_all, gq, gk, bd, cos_t, sin_t, tl):
    b, l, d = x.shape
    aw, kw = gq.shape[1], gk.shape[1]
    uw = w_all.shape[1] - aw - 2 * kw
    const = lambda i, j: (0, 0)
    tok = lambda i, j: (j, i, 0)
    return pl.pallas_call(
        _inproj_kernel,
        grid=(l // tl, b),
        in_specs=[pl.BlockSpec((None, tl, d), tok),
                  pl.BlockSpec(mod.shape, const),
                  pl.BlockSpec((1, d), const),
                  pl.BlockSpec(w_all.shape, const),
                  pl.BlockSpec((1, aw), const),
                  pl.BlockSpec((1, kw), const),
                  pl.BlockSpec(bd.shape, const),
                  pl.BlockSpec((tl, LANES), lambda i, j: (i, 0)),
                  pl.BlockSpec((tl, LANES), lambda i, j: (i, 0))],
        out_specs=[pl.BlockSpec((None, tl, aw), tok),
                   pl.BlockSpec((None, tl, kw), tok),
                   pl.BlockSpec((None, tl, kw), tok),
                   pl.BlockSpec((None, tl, uw), tok)],
        out_shape=[jax.ShapeDtypeStruct((b, l, aw), BF16),
                   jax.ShapeDtypeStruct((b, l, kw), BF16),
                   jax.ShapeDtypeStruct((b, l, kw), BF16),
                   jax.ShapeDtypeStruct((b, l, uw), F32)],
        compiler_params=_cparams("arbitrary", "arbitrary"),
        name="inproj",
    )(x, mod, norm1, w_all, gq, gk, bd, cos_t, sin_t)


def _ctxkv_kernel(row, x_ref, mod_ref, n1_ref, w_ref, gk_ref, bd_ref, k_ref, v_ref):
    d = x_ref.shape[-1]
    kw = k_ref.shape[-1]
    hb = _modulated(x_ref[...], mod_ref, row, n1_ref, d).astype(BF16)
    k = _head_rms(_dot(hb, w_ref[:, 0:kw]), bd_ref[...]) * gk_ref[...]
    k_ref[...] = k.astype(BF16)
    v_ref[...] = _dot(hb, w_ref[:, kw:]).astype(BF16)


def _ctxkv(ctx, mod, ctx_row, norm1, w_kv, gk, bd):
    b, lc, d = ctx.shape
    kw = gk.shape[1]
    const = lambda i: (0, 0)
    tok = lambda i: (i, 0, 0)
    return pl.pallas_call(
        functools.partial(_ctxkv_kernel, ctx_row),
        grid=(b,),
        in_specs=[pl.BlockSpec((None, lc, d), tok),
                  pl.BlockSpec(mod.shape, const),
                  pl.BlockSpec((1, d), const),
                  pl.BlockSpec(w_kv.shape, const),
                  pl.BlockSpec((1, kw), const),
                  pl.BlockSpec(bd.shape, const)],
        out_specs=[pl.BlockSpec((None, lc, kw), tok), pl.BlockSpec((None, lc, kw), tok)],
        out_shape=[jax.ShapeDtypeStruct((b, lc, kw), BF16)] * 2,
        compiler_params=_cparams("arbitrary"),
        name="ctxkv",
    )(ctx, mod, norm1, w_kv, gk, bd)


def _attn_kernel(sink_ref, q_ref, k_ref, v_ref, kx_ref, vx_ref, ga_ref, o_ref, acc_ref):
    tq = WINDOW
    nb = q_ref.shape[0] // tq
    lc = kx_ref.shape[0]
    nk = 3 * tq + lc
    pairs = GROUP // 2
    rows = pairs * tq

    def block(i, carry):
        r = lax.broadcasted_iota(jnp.int32, (rows, tq), 0) % tq
        j = lax.broadcasted_iota(jnp.int32, (rows, tq), 1)
        lo = lax.broadcasted_iota(jnp.int32, (nk, LANES), 1) < HEAD_DIM
        zero = jnp.zeros((nk, LANES), BF16)
        top = lax.broadcasted_iota(jnp.int32, (rows, 1), 0) < tq
        at = lambda blk: pl.ds(pl.multiple_of(blk * tq, tq), tq)
        cur, prev, nxt = at(i), at(jnp.maximum(i - 1, 0)), at(jnp.minimum(i + 1, nb - 1))
        ok_prev = j >= r + jnp.where(i > 0, 0, tq)
        ok_next = j <= r - jnp.where(i < nb - 1, 0, tq)
        ks, vs, qs, sinks = [], [], [], []
        for h in range(N_KV_HEADS):
            hs = slice(h * LANES, (h + 1) * LANES)
            kcat = jnp.concatenate([k_ref[prev, hs], k_ref[cur, hs], k_ref[nxt, hs], kx_ref[:, hs]], axis=0)
            vcat = jnp.concatenate([v_ref[prev, hs], v_ref[cur, hs], v_ref[nxt, hs], vx_ref[:, hs]], axis=0)
            q2 = jnp.concatenate([q_ref[cur, (h * pairs + p) * LANES:(h * pairs + p + 1) * LANES]
                                  for p in range(pairs)], axis=0)
            for half in range(2):
                ks.append(jnp.where(lo, kcat, zero) if half == 0 else jnp.where(lo, zero, kcat))
                vs.append(jnp.where(lo, vcat, zero) if half == 0 else jnp.where(lo, zero, vcat))
                qs.append(q2)
                sinks.append(jnp.where(top, sink_ref[h * GROUP + half], sink_ref[h * GROUP + 2 + half]))
        ss = [lax.dot_general(qq, kk, (((1,), (1,)), ((), ())), preferred_element_type=F32)
              for qq, kk in zip(qs, ks)]
        ss = [jnp.concatenate([jnp.where(ok_prev, s[:, 0:tq], NEG_INF), s[:, tq:2 * tq],
                               jnp.where(ok_next, s[:, 2 * tq:3 * tq], NEG_INF), s[:, 3 * tq:]], axis=1)
              for s in ss]
        ms = [jnp.maximum(jnp.max(s, axis=-1, keepdims=True), sk) for s, sk in zip(ss, sinks)]
        es = [jnp.exp(s - m) for s, m in zip(ss, ms)]
        dens = [jnp.sum(e, axis=-1, keepdims=True) + jnp.exp(sk - m) for e, sk, m in zip(es, sinks, ms)]
        outs = [_dot(e.astype(BF16), vv) / den for e, vv, den in zip(es, vs, dens)]
        for h in range(N_KV_HEADS):
            both = outs[2 * h] + outs[2 * h + 1]
            for p in range(pairs):
                acc_ref[:, (h * pairs + p) * LANES:(h * pairs + p + 1) * LANES] = both[p * tq:(p + 1) * tq]
        a = acc_ref[...]
        o_ref[cur, :] = (a * _rms(a) * ga_ref[...]).astype(BF16)
        return carry

    lax.fori_loop(0, nb, block, 0)


def _attention(sink, q, k, v, kx, vx, ga):
    b, l, aw = q.shape
    kw = k.shape[-1]
    lc = kx.shape[1]
    seq = lambda bi: (bi, 0, 0)
    return pl.pallas_call(
        _attn_kernel,
        grid=(b,),
        in_specs=[pl.BlockSpec(memory_space=pltpu.SMEM),
                  pl.BlockSpec((None, l, aw), seq),
                  pl.BlockSpec((None, l, kw), seq), pl.BlockSpec((None, l, kw), seq),
                  pl.BlockSpec((None, lc, kw), seq), pl.BlockSpec((None, lc, kw), seq),
                  pl.BlockSpec((1, aw), lambda bi: (0, 0))],
        out_specs=pl.BlockSpec((None, l, aw), seq),
        out_shape=jax.ShapeDtypeStruct((b, l, aw), BF16),
        scratch_shapes=[pltpu.VMEM((WINDOW, aw), F32)],
        compiler_params=_cparams("arbitrary"),
        name="attn",
    )(sink, q, k, v, kx, vx, ga)


def _alternating(rows, cols):
    return (1 - 2 * (lax.broadcasted_iota(jnp.int32, (rows, cols), 0) % 2)).astype(F32)


def _hconv_kernel(u0_ref, u1_ref, u2_ref, w0_ref, w1_ref, w2_ref, b0_ref, b1_ref, b2_ref,
                  x0e_ref, x0o_ref, ze_ref, zo_ref, qr_ref, qs_ref, scr):
    l, cb = u0_ref.shape
    half = l // 2
    row = lax.broadcasted_iota(jnp.int32, (l, cb), 0)

    def conv(u_ref, w_ref, b_ref):
        u = u_ref[...]
        before = jnp.where(row == 0, 0.0, pltpu.roll(u, 1, 0))
        after = jnp.where(row == l - 1, 0.0, pltpu.roll(u, l - 1, 0))
        return b_ref[...] + before * w_ref[0:1, :] + u * w_ref[1:2, :] + after * w_ref[2:3, :]

    def parity_halves(v):
        scr[...] = v
        return scr[pl.ds(0, half, stride=2), :], scr[pl.ds(1, half, stride=2), :]

    x0e_ref[...], x0o_ref[...] = parity_halves(conv(u0_ref, w0_ref, b0_ref))
    ze, zo = parity_halves(conv(u1_ref, w1_ref, b1_ref) * conv(u2_ref, w2_ref, b2_ref))
    ze_ref[...] = ze.astype(BF16)
    zo_ref[...] = zo.astype(BF16)
    sign = _alternating(half, cb)
    qr_ref[...] = jnp.sum(ze * sign, axis=0, keepdims=True)
    qs_ref[...] = jnp.sum(zo * sign, axis=0, keepdims=True)


def _hconv(u, conv_w, conv_b, cb=LANES):
    b, l, w3 = u.shape
    c = w3 // 3
    n = c // cb
    us = [pl.BlockSpec((None, l, cb), lambda bi, j, g=g: (bi, 0, g * n + j)) for g in range(3)]
    ws = [pl.BlockSpec((3, cb), lambda bi, j, g=g: (0, g * n + j)) for g in range(3)]
    bs = [pl.BlockSpec((1, cb), lambda bi, j, g=g: (0, g * n + j)) for g in range(3)]
    out = lambda bi, j: (bi, 0, j)
    seq = pl.BlockSpec((None, l // 2, cb), out)
    vec = pl.BlockSpec((None, 1, cb), out)
    return pl.pallas_call(
        _hconv_kernel,
        grid=(b, n),
        in_specs=us + ws + bs,
        out_specs=[seq, seq, seq, seq, vec, vec],
        out_shape=[jax.ShapeDtypeStruct((b, l // 2, c), F32)] * 2 + [jax.ShapeDtypeStruct((b, l // 2, c), BF16)] * 2
        + [jax.ShapeDtypeStruct((b, 1, c), F32)] * 2,
        scratch_shapes=[pltpu.VMEM((l, cb), F32)],
        compiler_params=_cparams("arbitrary", "arbitrary"),
        name="hconv",
    )(u, u, u, conv_w, conv_w, conv_w, conv_b, conv_b, conv_b)


def _filter_kernel(fe_ref, fo_ref, w1_ref, b1_ref, w2_ref, b2_ref, w3_ref, b3_ref, fr_ref, w4f_ref, w4b_ref,
                   dl_ref, kpe_ref, kpo_ref, kme_ref, kmo_ref, krq_ref, ksq_ref, he_scr, ho_scr):
    half, cf = kpe_ref.shape
    l = 2 * half

    @pl.when(pl.program_id(0) == 0)
    def _():
        fr = fr_ref[...]
        for f_ref, h_scr in ((fe_ref, he_scr), (fo_ref, ho_scr)):
            h = jnp.sin(fr * (_dot3(f_ref[...], w1_ref[...]) + b1_ref[...]))
            h = jnp.sin(fr * (_dot3(h, w2_ref[...]) + b2_ref[...]))
            h_scr[...] = jnp.sin(fr * (_dot3(h, w3_ref[...]) + b3_ref[...]))

    row = lax.broadcasted_iota(jnp.int32, (half, cf), 0)

    def taps(h_scr, parity):
        t = (2 * row + parity).astype(F32) / (l - 1)
        decay = jnp.exp(-t * dl_ref[...])
        h = h_scr[...]
        return _dot3(h, w4f_ref[...]) * decay, _dot3(h, w4b_ref[...]) * decay

    kfe, kbe = taps(he_scr, 0)
    kbe = jnp.where(row == 0, 0.0, kbe)
    kfo, kbo = taps(ho_scr, 1)
    nrm = lax.rsqrt(jnp.sum(kfe * kfe + kbe * kbe + kfo * kfo + kbo * kbo, axis=0, keepdims=True) + EPS)
    kpe = (kfe + kbe) * nrm
    kmo = (kfo - kbo) * nrm
    kpe_ref[...] = kpe.astype(BF16)
    kpo_ref[...] = ((kfo + kbo) * nrm).astype(BF16)
    kme_ref[...] = ((kfe - kbe) * nrm).astype(BF16)
    kmo_ref[...] = kmo.astype(BF16)
    sign = _alternating(half, cf)
    krq_ref[...] = jnp.sum(kpe * sign, axis=0, keepdims=True)
    ksq_ref[...] = jnp.sum(kmo * sign, axis=0, keepdims=True)


def _filter(fe, fo, w1, b1, w2, b2, w3, b3, fr, w4, deltas, cf=LANES):
    half, zw = fe.shape
    ffn = w2.shape[0]
    c = w4.shape[1] // 2
    n = c // cf
    const = lambda j: (0, 0)
    col = lambda j: (0, j)
    vec = pl.BlockSpec((1, ffn), const)
    mat = pl.BlockSpec((ffn, ffn), const)
    return pl.pallas_call(
        _filter_kernel,
        grid=(n,),
        in_specs=[pl.BlockSpec((half, zw), const), pl.BlockSpec((half, zw), const),
                  pl.BlockSpec((zw, ffn), const), vec, mat, vec, mat, vec, vec,
                  pl.BlockSpec((ffn, cf), col),
                  pl.BlockSpec((ffn, cf), lambda j: (0, n + j)),
                  pl.BlockSpec((1, cf), col)],
        out_specs=[pl.BlockSpec((half, cf), col)] * 4 + [pl.BlockSpec((1, cf), col)] * 2,
        out_shape=[jax.ShapeDtypeStruct((half, c), BF16)] * 4 + [jax.ShapeDtypeStruct((1, c), F32)] * 2,
        scratch_shapes=[pltpu.VMEM((half, ffn), F32)] * 2,
        compiler_params=_cparams("arbitrary"),
        name="filt",
    )(fe, fo, w1, b1, w2, b2, w3, b3, fr, w4, w4, deltas)


DFT_FINE = 64


def _dftgen_kernel(ca_ref, sa_ref, cb_ref, sb_ref, c_ref, s_ref):
    cb = cb_ref[...]
    sb = sb_ref[...]
    for a in range(ca_ref.shape[0]):
        ca = ca_ref[a:a + 1, :]
        sa = sa_ref[a:a + 1, :]
        rs = slice(a * DFT_FINE, (a + 1) * DFT_FINE)
        c_ref[rs, :] = (ca * cb - sa * sb).astype(BF16)
        s_ref[rs, :] = (sa * cb + ca * sb).astype(BF16)


def _dftgen(coarse, fine, l, tf, name):
    tabs = [jnp.asarray(fn((k % (2 * l)) * (math.pi / l)), F32) for k in (coarse, fine) for fn in (np.cos, np.sin)]
    rows, width = coarse.shape[0] * DFT_FINE, coarse.shape[1]
    na = tf // DFT_FINE
    tile = lambda i: (i, 0)
    const = lambda i: (0, 0)
    return pl.pallas_call(
        _dftgen_kernel,
        grid=(rows // tf,),
        in_specs=[pl.BlockSpec((na, width), tile), pl.BlockSpec((na, width), tile),
                  pl.BlockSpec((DFT_FINE, width), const), pl.BlockSpec((DFT_FINE, width), const)],
        out_specs=[pl.BlockSpec((tf, width), tile), pl.BlockSpec((tf, width), tile)],
        out_shape=[jax.ShapeDtypeStruct((rows, width), BF16)] * 2,
        compiler_params=_cparams("arbitrary"),
        name=name,
    )(*tabs)


def _dft_matrices(l, tf):
    half = l // 2
    lo = np.arange(half, dtype=np.int64)[None, :]
    a = np.arange(half // DFT_FINE, dtype=np.int64)[:, None] * DFT_FINE
    i = np.arange(DFT_FINE, dtype=np.int64)[:, None]
    ce, se = _dftgen(a * 2 * lo, i * 2 * lo, l, tf, "dft_even")
    co, so = _dftgen(a * (2 * lo + 1), i * (2 * lo + 1), l, tf, "dft_odd")
    cot, sot = _dftgen(2 * a * lo, (2 * i + 1) * lo, l, tf, "dft_odd_t")
    return ce, se, co, so, cot, sot


def _kspec_kernel(n_fft, ce_ref, se_ref, co_ref, so_ref, kpe_ref, kpo_ref, kme_ref, kmo_ref,
                  krl_ref, krm_ref, ksl_ref, ksm_ref):
    tf = ce_ref.shape[0]
    f = pl.program_id(0) * tf + lax.broadcasted_iota(jnp.int32, (tf, 1), 0)
    w = jnp.where(f == 0, 1.0 / n_fft, 2.0 / n_fft)
    ce = _dot(ce_ref[...], kpe_ref[...])
    co = _dot(co_ref[...], kpo_ref[...])
    se = _dot(se_ref[...], kme_ref[...])
    so = _dot(so_ref[...], kmo_ref[...])
    krl_ref[...] = (ce + co) * w
    krm_ref[...] = (ce - co) * w
    ksl_ref[...] = (so + se) * w
    ksm_ref[...] = (so - se) * w


def _kspec(mats, kpe, kpo, kme, kmo, tf):
    half, c = kpe.shape
    const = lambda i: (0, 0)
    tile = lambda i: (i, 0)
    return pl.pallas_call(
        functools.partial(_kspec_kernel, 4 * half),
        grid=(half // tf,),
        in_specs=[pl.BlockSpec((tf, half), tile)] * 4 + [pl.BlockSpec((half, c), const)] * 4,
        out_specs=[pl.BlockSpec((tf, c), tile)] * 4,
        out_shape=[jax.ShapeDtypeStruct((half, c), F32)] * 4,
        compiler_params=_cparams("arbitrary"),
        name="kspec",
    )(*mats, kpe, kpo, kme, kmo)


def _hfwd_kernel(ce_ref, se_ref, co_ref, so_ref, ze_ref, zo_ref, krl_ref, krm_ref, ksl_ref, ksm_ref,
                 p1_ref, p2_ref, p3_ref, p4_ref):
    ze = ze_ref[...]
    zo = zo_ref[...]
    ce = _dot(ce_ref[...], ze)
    co = _dot(co_ref[...], zo)
    se = _dot(se_ref[...], ze)
    so = _dot(so_ref[...], zo)

    def product(zr, zs, kr_ref, ks_ref):
        kr = kr_ref[...]
        ks = ks_ref[...]
        return zr * kr - zs * ks, zr * ks + zs * kr

    al, bl = product(ce + co, so + se, krl_ref, ksl_ref)
    am, bm = product(ce - co, so - se, krm_ref, ksm_ref)
    p1_ref[...] = (al + am).astype(BF16)
    p2_ref[...] = (bl - bm).astype(BF16)
    p3_ref[...] = (al - am).astype(BF16)
    p4_ref[...] = (bl + bm).astype(BF16)


def _hfwd(mats, ze, zo, spec, tf):
    b, half, c = ze.shape
    tile = lambda i, bi: (i, 0)
    seq = lambda i, bi: (bi, 0, 0)
    out = lambda i, bi: (bi, i, 0)
    return pl.pallas_call(
        _hfwd_kernel,
        grid=(half // tf, b),
        in_specs=[pl.BlockSpec((tf, half), tile)] * 4 + [pl.BlockSpec((None, half, c), seq)] * 2
        + [pl.BlockSpec((tf, c), tile)] * 4,
        out_specs=[pl.BlockSpec((None, tf, c), out)] * 4,
        out_shape=[jax.ShapeDtypeStruct((b, half, c), BF16)] * 4,
        compiler_params=_cparams("arbitrary", "arbitrary"),
        name="hfwd",
    )(*mats, ze, zo, *spec)


def _hinv_kernel(ce_ref, se_ref, cot_ref, sot_ref, p1_ref, p2_ref, p3_ref, p4_ref, x0e_ref, x0o_ref,
                 ze_ref, zo_ref, qr_ref, qs_ref, krq_ref, ksq_ref, bias_ref, g_ref, o_ref, scr):
    tt, half = ce_ref.shape
    n_fft = 4 * half
    ye = _dot(ce_ref[...], p1_ref[...]) + _dot(se_ref[...], p2_ref[...])
    yo = _dot(cot_ref[...], p3_ref[...]) + _dot(sot_ref[...], p4_ref[...])
    sign = (1 - 2 * ((pl.program_id(0) * tt + lax.broadcasted_iota(jnp.int32, (tt, 1), 0)) % 2)).astype(F32)
    qr, qs, krq, ksq = qr_ref[...], qs_ref[...], krq_ref[...], ksq_ref[...]
    ye = ye + sign * ((qr * krq - qs * ksq) * (2.0 / n_fft))
    yo = yo + sign * ((qr * ksq + qs * krq) * (2.0 / n_fft))

    def finish(y, x0_ref, z_ref):
        hy = x0_ref[...] * (y + z_ref[...].astype(F32) * bias_ref[...])
        return hy * _rms(hy) * g_ref[...]

    he = finish(ye, x0e_ref, ze_ref)
    ho = finish(yo, x0o_ref, zo_ref)
    for c in range(scr.shape[0]):
        cs = slice(c * LANES, (c + 1) * LANES)
        scr[c, pl.ds(0, tt, stride=2), :] = he[:, cs]
        scr[c, pl.ds(1, tt, stride=2), :] = ho[:, cs]
        o_ref[:, cs] = scr[c].astype(BF16)


def _hinv(ce, se, cot, sot, ps, x0e, x0o, ze, zo, qr, qs, krq, ksq, bias, gain, tt):
    b, half, c = ze.shape
    tile = lambda i, bi: (i, 0)
    full = lambda i, bi: (bi, 0, 0)
    tok = lambda i, bi: (bi, i, 0)
    const = lambda i, bi: (0, 0)
    vec = pl.BlockSpec((1, c), const)
    return pl.pallas_call(
        _hinv_kernel,
        grid=(half // tt, b),
        in_specs=[pl.BlockSpec((tt, half), tile)] * 4 + [pl.BlockSpec((None, half, c), full)] * 4
        + [pl.BlockSpec((None, tt, c), tok)] * 4 + [pl.BlockSpec((None, 1, c), full)] * 2 + [vec] * 4,
        out_specs=pl.BlockSpec((None, 2 * tt, c), tok),
        out_shape=jax.ShapeDtypeStruct((b, 2 * half, c), BF16),
        scratch_shapes=[pltpu.VMEM((c // LANES, 2 * tt, LANES), F32)],
        compiler_params=_cparams("arbitrary", "arbitrary"),
        name="hinv",
    )(ce, se, cot, sot, *ps, x0e, x0o, ze, zo, qr, qs, krq, ksq, bias, gain)


def _mixout_kernel(an_ref, yn_ref, x_ref, mod_ref, wo_ref, n2_ref, wr_ref, br_ref, tri_ref,
                   xn_ref, h2_ref, route_ref, route_t_ref, cnt_ref, carry):
    bi = pl.program_id(0)
    d = x_ref.shape[-1]
    half = an_ref.shape[-1]
    tl = x_ref.shape[0]

    @pl.when((bi == 0) & (pl.program_id(1) == 0))
    def _():
        carry[...] = jnp.zeros_like(carry)

    mix = _dot(an_ref[...], wo_ref[0:half, :]) + _dot(yn_ref[...], wo_ref[half:, :])
    g1 = mod_ref[pl.ds(bi, 1), 2 * d:3 * d]
    sh2 = mod_ref[pl.ds(bi, 1), 3 * d:4 * d]
    sc2 = mod_ref[pl.ds(bi, 1), 4 * d:5 * d]
    xn = x_ref[...] + g1 * mix
    xn_ref[...] = xn
    h2 = (xn * _rms(xn)) * n2_ref[...] * (1 + sc2) + sh2
    for j in range(d // LANES):
        _token_chunk(h2_ref, 0, tl, j)[...] = h2[:, j * LANES:(j + 1) * LANES]

    logits = _dot3(h2, wr_ref[...]) + br_ref[...]
    lane = lax.broadcasted_iota(jnp.int32, (tl, LANES), 1).astype(F32)
    vals, idxs, sels = [], [], []
    cur = logits
    for _ in range(TOP_K):
        m = jnp.max(cur, axis=-1, keepdims=True)
        idx = jnp.min(jnp.where(cur == m, lane, float(LANES)), axis=-1, keepdims=True)
        sel = lane == idx
        vals.append(m)
        idxs.append(idx)
        sels.append(sel)
        cur = jnp.where(sel, -jnp.inf, cur)
    es = [jnp.exp(v - vals[0]) for v in vals]
    den = es[0] + es[1] + es[2] + es[3]
    hot = sum(s.astype(F32) for s in sels)
    before = _dot(tri_ref[...], hot.astype(BF16)) + carry[...]
    carry[...] = carry[...] + jnp.sum(hot, axis=0, keepdims=True)
    cnt_ref[...] = carry[...]
    route = jnp.zeros((tl, LANES), F32)
    for k in range(TOP_K):
        pos = jnp.sum(jnp.where(sels[k], before, 0.0), axis=-1, keepdims=True)
        route = jnp.where(lane == k, idxs[k], route)
        route = jnp.where(lane == TOP_K + k, es[k] / den, route)
        route = jnp.where(lane == 2 * TOP_K + k, pos, route)
    route_ref[...] = route
    route_t_ref[...] = route.T[0:route_t_ref.shape[0], :]


def _mixout(an, yn, x, mod, w_out, norm2, wr, br, tri, tl):
    b, l, d = x.shape
    half = an.shape[-1]
    nt = l // tl
    tok = lambda bi, i: (bi, i, 0)
    const = lambda bi, i: (0, 0)
    flat = lambda bi, i: (bi * nt + i, 0)
    return pl.pallas_call(
        _mixout_kernel,
        grid=(b, nt),
        in_specs=[pl.BlockSpec((None, tl, half), tok), pl.BlockSpec((None, tl, half), tok),
                  pl.BlockSpec((None, tl, d), tok),
                  pl.BlockSpec(mod.shape, const), pl.BlockSpec(w_out.shape, const),
                  pl.BlockSpec((1, d), const), pl.BlockSpec(wr.shape, const),
                  pl.BlockSpec((1, LANES), const), pl.BlockSpec((tl, tl), const)],
        out_specs=[pl.BlockSpec((None, tl, d), tok),
                   pl.BlockSpec((tl * SUBLANES, LANES), flat),
                   pl.BlockSpec((tl, LANES), flat),
                   pl.BlockSpec((2 * SUBLANES, tl), lambda bi, i: (0, bi * nt + i)),
                   pl.BlockSpec((1, LANES), const)],
        out_shape=[jax.ShapeDtypeStruct((b, l, d), F32),
                   jax.ShapeDtypeStruct((b * l * SUBLANES, LANES), F32),
                   jax.ShapeDtypeStruct((b * l, LANES), F32),
                   jax.ShapeDtypeStruct((2 * SUBLANES, b * l), F32),
                   jax.ShapeDtypeStruct((1, LANES), F32)],
        scratch_shapes=[pltpu.VMEM((1, LANES), F32)],
        compiler_params=_cparams("arbitrary", "arbitrary"),
        name="mixout",
    )(an, yn, x, mod, w_out, norm2, wr, br, tri)


def _token_chunk(ref, first_token, n, j):
    return ref.at[pl.ds(first_token * SUBLANES + j, n, stride=SUBLANES), :]


def _token_copy(idx_ref, r, src_hbm, dst, sem):
    first_row = lambda tok: tok * SUBLANES if isinstance(tok, int) else pl.multiple_of(tok * SUBLANES, SUBLANES)
    return pltpu.make_async_copy(src_hbm.at[pl.ds(first_row(idx_ref[0, r]), SUBLANES)],
                                 dst.at[pl.ds(first_row(r), SUBLANES)], sem)


def _gather_tokens(idx_ref, n, src_hbm, dst, sem):
    def body(r, carry):
        _token_copy(idx_ref, 2 * r, src_hbm, dst, sem).start(priority=0)
        _token_copy(idx_ref, 2 * r + 1, src_hbm, dst, sem).start(priority=1)
        return carry
    lax.fori_loop(0, n // 2, body, 0, unroll=4)


def _wait_tokens(src_hbm, dst, sem):
    pltpu.make_async_copy(src_hbm.at[pl.ds(0, dst.shape[0])], dst, sem).wait()


EXPERT_STAGES = 4
def _experts_kernel(be_ref, nvb_ref, tok_ref, tokn_ref, h2_hbm, wgu_ref, bgu_ref, wdn_ref, bdn_ref,
                    ys_ref, xbuf, xs_bf, wgu_bf, wdn_bf, sem):
    i = pl.program_id(0)
    nvb = nvb_ref[0]
    rows = xbuf.shape[1] // SUBLANES
    nchunk = wgu_ref.shape[0] // LANES
    ff = wdn_ref.shape[0]
    fc = ff // EXPERT_STAGES
    slot = i % 2

    @pl.when(i == 0)
    def _():
        _gather_tokens(tok_ref, rows, h2_hbm, xbuf.at[0], sem.at[0])

    @pl.when(i + 1 < nvb)
    def _():
        _gather_tokens(tokn_ref, rows, h2_hbm, xbuf.at[1 - slot], sem.at[1 - slot])

    @pl.when((i == 0) | (be_ref[i] != be_ref[jnp.maximum(i - 1, 0)]))
    def _():
        wgu_bf[...] = wgu_ref[...].astype(BF16)
        wdn_bf[...] = wdn_ref[...].astype(BF16)

    @pl.when(i < nvb)
    def _():
        _wait_tokens(h2_hbm, xbuf.at[slot], sem.at[slot])
        for j in range(nchunk):
            xs_bf[:, j * LANES:(j + 1) * LANES] = _token_chunk(xbuf.at[slot], 0, rows, j)[...].astype(BF16)
        xs = xs_bf[...]
        def gate_up(c):
            cg = slice(c * fc, (c + 1) * fc)
            cu = slice(ff + c * fc, ff + (c + 1) * fc)
            return _dot(xs, wgu_bf[:, cg]) + bgu_ref[:, cg], _dot(xs, wgu_bf[:, cu]) + bgu_ref[:, cu]

        acc = None
        nxt = gate_up(0)
        for c in range(EXPERT_STAGES):
            gate, up = nxt
            if c + 1 < EXPERT_STAGES:
                nxt = gate_up(c + 1)
            gate = jnp.minimum(gate, SWIGLU_LIMIT)
            up = jnp.clip(up, -SWIGLU_LIMIT, SWIGLU_LIMIT)
            act = ((up + 1) * (gate * jax.nn.sigmoid(SWIGLU_ALPHA * gate))).astype(BF16)
            part = _dot(act, wdn_bf[c * fc:(c + 1) * fc, :])
            acc = part if acc is None else acc + part
        y = acc + bdn_ref[...]
        for j in range(nchunk):
            _token_chunk(ys_ref, 0, rows, j)[...] = y[:, j * LANES:(j + 1) * LANES]

    @pl.when(i >= nvb)
    def _():
        ys_ref[...] = jnp.zeros_like(ys_ref)


def _experts(block_e, nvb, slot_tok, h2, wgu, bgu, wdn, bdn, rows):
    n_blk = block_e.shape[0]
    d, ff2 = wgu.shape[1:]
    ff = wdn.shape[1]
    tok = lambda k: pl.BlockSpec((None, 1, rows), lambda i, be, nv: (jnp.minimum(i + k, n_blk - 1), 0, 0),
                                 memory_space=pltpu.SMEM)
    expert = lambda i, be, nv: (be[i], 0, 0)
    grid_spec = pltpu.PrefetchScalarGridSpec(
        num_scalar_prefetch=2,
        grid=(n_blk,),
        in_specs=[tok(0), tok(1),
                  pl.BlockSpec(memory_space=pl.ANY),
                  pl.BlockSpec((None, d, ff2), expert), pl.BlockSpec((None, 1, ff2), expert),
                  pl.BlockSpec((None, ff, d), expert), pl.BlockSpec((None, 1, d), expert)],
        out_specs=pl.BlockSpec((rows * SUBLANES, LANES), lambda i, be, nv: (i, 0)),
        scratch_shapes=[pltpu.VMEM((2, rows * SUBLANES, LANES), F32), pltpu.VMEM((rows, d), BF16),
                        pltpu.VMEM((d, ff2), BF16), pltpu.VMEM((ff, d), BF16),
                        pltpu.SemaphoreType.DMA((2,))],
    )
    return pl.pallas_call(
        _experts_kernel,
        grid_spec=grid_spec,
        out_shape=jax.ShapeDtypeStruct((n_blk * rows * SUBLANES, LANES), F32),
        compiler_params=_cparams("arbitrary"),
        name="experts",
    )(block_e, nvb, slot_tok, slot_tok, h2, wgu, bgu, wdn, bdn)


def _combine_kernel(dest_ref, destn_ref, ys_hbm, xn_ref, route_ref, g2_ref, o_ref, ybuf, sem):
    i = pl.program_id(0)
    n = pl.num_programs(0)
    tc = xn_ref.shape[0]
    slot = i % 2

    @pl.when(i == 0)
    def _():
        _gather_tokens(dest_ref, TOP_K * tc, ys_hbm, ybuf.at[0], sem.at[0])

    @pl.when(i + 1 < n)
    def _():
        _gather_tokens(destn_ref, TOP_K * tc, ys_hbm, ybuf.at[1 - slot], sem.at[1 - slot])

    _wait_tokens(ys_hbm, ybuf.at[slot], sem.at[slot])
    route = route_ref[...]
    gates = [jnp.broadcast_to(route[:, TOP_K + k:TOP_K + k + 1], (tc, LANES)) for k in range(TOP_K)]
    for j in range(xn_ref.shape[1] // LANES):
        cs = slice(j * LANES, (j + 1) * LANES)
        acc = gates[0] * _token_chunk(ybuf.at[slot], 0, tc, j)[...]
        for k in range(1, TOP_K):
            acc = acc + gates[k] * _token_chunk(ybuf.at[slot], k * tc, tc, j)[...]
        o_ref[:, cs] = xn_ref[:, cs] + g2_ref[:, cs] * acc


def _combine(dest_kmaj, ys, xn, route, g2, tc):
    t, d = xn.shape
    n = t // tc
    per_batch = t // g2.shape[0] // tc
    return pl.pallas_call(
        _combine_kernel,
        grid=(n,),
        in_specs=[pl.BlockSpec((None, 1, TOP_K * tc), lambda i: (i, 0, 0), memory_space=pltpu.SMEM),
                  pl.BlockSpec((None, 1, TOP_K * tc), lambda i: (jnp.minimum(i + 1, n - 1), 0, 0),
                               memory_space=pltpu.SMEM),
                  pl.BlockSpec(memory_space=pl.ANY),
                  pl.BlockSpec((tc, d), lambda i: (i, 0)),
                  pl.BlockSpec((tc, LANES), lambda i: (i, 0)),
                  pl.BlockSpec((None, 1, d), lambda i: (i // per_batch, 0, 0))],
        out_specs=pl.BlockSpec((tc, d), lambda i: (i, 0)),
        out_shape=jax.ShapeDtypeStruct((t, d), F32),
        scratch_shapes=[pltpu.VMEM((2, TOP_K * tc * SUBLANES, LANES), F32), pltpu.SemaphoreType.DMA((2,))],
        compiler_params=_cparams("arbitrary"),
        name="combine",
    )(dest_kmaj, dest_kmaj, ys, xn, route, g2)


def _rope_tables(l):
    n_freq = HEAD_DIM // 4
    inv_freq = ROPE_BASE ** (-np.arange(n_freq, dtype=np.float64) / n_freq)
    tpos = np.arange(l)
    lane = np.arange(LANES) % HEAD_DIM
    pos = np.where(lane[None, :] < HEAD_DIM // 2, (tpos // GRID_W)[:, None], (tpos % GRID_W)[:, None])
    ang = pos * inv_freq[lane % n_freq][None, :]
    sign = np.where(lane % (2 * n_freq) < n_freq, -1.0, 1.0)[None, :]
    return jnp.asarray(np.cos(ang), F32), jnp.asarray(np.sin(ang) * sign, F32)


def _filter_features(l, width):
    t = np.linspace(0.0, 1.0, l)[:, None]
    w = 2.0 * math.pi * np.arange(l)[:, None] / l
    bands = np.linspace(1e-4, POS_BANDS - 1, POS_BANDS)[None, :]
    z = np.concatenate([t, np.cos(bands * w), -np.sin(bands * w)], axis=-1)
    return np.pad(z, ((0, 0), (0, width - z.shape[1]))).astype(np.float32)


def kernel(x, c, ctx, c_ctx, w_mod, b_mod, norm1, norm2, w_in, q_norm, k_norm, sink, conv_w, conv_b,
           filt_w1, filt_b1, filt_w2, filt_b2, filt_w3, filt_b3, filt_w4, filt_freq, filt_bias,
           attn_out_norm, hyena_out_norm, w_out, w_router, b_router, w_gu, b_gu, w_down, b_down):
    assert w_mod.shape[0] == 1, "single-layer configuration"
    b, l, d = x.shape
    assert d == SUBLANES * LANES, "token-tiled rows assume one (8, 128) tile per token"
    t = b * l
    aw = N_Q_HEADS * HEAD_DIM
    kvw = N_KV_HEADS * HEAD_DIM
    hw = conv_w.shape[-1] // 3
    tl = min(512, l)

    ctx_row = b
    pad_rows = -(b + 1) % SUBLANES
    c_all = jnp.concatenate([c, c_ctx[None], jnp.zeros((pad_rows, d), F32)], axis=0)
    mod = _adaln(c_all, w_mod[0], b_mod[0])

    w = w_in[0]
    wq, wk, wv, wu = w[:, :aw], w[:, aw:aw + kvw], w[:, aw + kvw:aw + 2 * kvw], w[:, aw + 2 * kvw:]
    dup = lambda m: jnp.concatenate([m[:, h * HEAD_DIM:(h + 1) * HEAD_DIM]
                                     for h in range(N_KV_HEADS) for _ in range(2)], axis=1)
    w_all = jnp.concatenate([wq, dup(wk), dup(wv), wu], axis=1).astype(BF16)
    w_kv = jnp.concatenate([dup(wk), dup(wv)], axis=1).astype(BF16)
    gq = jnp.tile(q_norm[0], N_Q_HEADS)[None]
    gk = jnp.tile(k_norm[0], 2 * N_KV_HEADS)[None]
    bd = jnp.asarray(np.kron(np.eye(N_Q_HEADS), np.full((HEAD_DIM, HEAD_DIM), 1.0 / HEAD_DIM)), BF16)
    cos_t, sin_t = _rope_tables(l)

    q, k, v, u = _inproj(x, mod, norm1, w_all, gq, gk, bd, cos_t, sin_t, tl)
    kx, vx = _ctxkv(ctx, mod, ctx_row, norm1, w_kv, gk, bd[:2 * kvw, :2 * kvw])
    an = _attention(sink[0], q, k, v, kx, vx, attn_out_norm)

    x0e, x0o, ze, zo, qr, qs = _hconv(u, conv_w[0], conv_b)
    ffn = filt_w2.shape[-1]
    zf = _filter_features(l, ffn)
    w1 = jnp.pad(filt_w1[0], ((0, ffn - POS_EMB_DIM), (0, 0)))
    deltas = jnp.asarray(np.linspace(MIN_DECAY, MAX_DECAY, hw)[None, :], F32)
    kpe, kpo, kme, kmo, krq, ksq = _filter(jnp.asarray(zf[0::2]), jnp.asarray(zf[1::2]), w1, filt_b1, filt_w2[0],
                                           filt_b2, filt_w3[0], filt_b3, filt_freq, filt_w4[0], deltas)
    tf = min(512, l // 2)
    ce, se, co, so, cot, sot = _dft_matrices(l, tf)
    spec = _kspec((ce, se, co, so), kpe, kpo, kme, kmo, tf)
    ps = _hfwd((ce, se, co, so), ze, zo, spec, tf)
    yn = _hinv(ce, se, cot, sot, ps, x0e, x0o, ze, zo, qr, qs, krq, ksq, filt_bias, hyena_out_norm,
               min(256, l // 2))

    wr = jnp.pad(w_router[0], ((0, 0), (0, LANES - N_EXPERTS)))
    br = jnp.concatenate([b_router[0], jnp.full((LANES - N_EXPERTS,), NEG_INF, F32)])[None]
    tri = jnp.asarray(np.tril(np.ones((tl, tl)), -1), BF16)
    xn, h2, route, route_t, cnt = _mixout(an, yn, x, mod, w_out[0].astype(BF16), norm2, wr, br, tri, tl)

    rows = EXPERT_ROWS
    a_tot = t * TOP_K
    n_blk = -(-a_tot // rows) + N_EXPERTS
    idx = route_t[0:TOP_K].astype(jnp.int32)
    pos = route_t[2 * TOP_K:3 * TOP_K].astype(jnp.int32)
    counts = cnt[0, :N_EXPERTS].astype(jnp.int32)
    pcounts = (counts + rows - 1) // rows * rows
    pends = jnp.cumsum(pcounts)
    pstarts = pends - pcounts
    experts = jnp.arange(N_EXPERTS, dtype=jnp.int32)[:, None, None]
    dest = jnp.sum(jnp.where(idx[None] == experts, pstarts[:, None, None], 0), axis=0) + pos
    block_start = jnp.arange(n_blk, dtype=jnp.int32) * rows
    block_e = jnp.minimum(jnp.sum(pends[None, :] <= block_start[:, None], axis=1), N_EXPERTS - 1).astype(jnp.int32)
    keys = jnp.sort((idx * t + jnp.arange(t, dtype=jnp.int32)[None, :]).reshape(-1))
    starts = jnp.cumsum(counts) - counts
    offset = (block_start - pstarts[block_e])[:, None] + jnp.arange(rows, dtype=jnp.int32)[None, :]
    src = starts[block_e][:, None] + jnp.minimum(offset, counts[block_e][:, None] - 1)
    slot_tok = keys[jnp.clip(src, 0, a_tot - 1)] % t

    nvb = (pends[-1:] // rows).astype(jnp.int32)
    ys = _experts(block_e, nvb, slot_tok.reshape(n_blk, 1, rows), h2,
                  w_gu[0], b_gu[0][:, None, :], w_down[0], b_down[0][:, None, :], rows)

    tc = min(COMBINE_ROWS, l)
    dest_kmaj = dest.reshape(TOP_K, t // tc, tc).transpose(1, 0, 2).reshape(t // tc, 1, TOP_K * tc)
    g2 = mod[:b, None, 5 * d:6 * d]
    out = _combine(dest_kmaj, ys, xn.reshape(t, d), route, g2, tc)
    return out.reshape(b, l, d)
```

```python
import functools
import math

import numpy as np
import jax
import jax.numpy as jnp
from jax import lax
from jax.experimental import pallas as pl
from jax.experimental.pallas import tpu as pltpu

F32 = jnp.float32
BF16 = jnp.bfloat16

LANES = 128
SUBLANES = 8
VMEM_LIMIT = 56 * 1024 * 1024

HEAD_DIM = 64
N_Q_HEADS = 8
N_KV_HEADS = 2
GROUP = N_Q_HEADS // N_KV_HEADS
GRID_W = 64
WINDOW = 128
ROPE_BASE = 10000.0
ATTN_SCALE = HEAD_DIM ** -0.5
POS_EMB_DIM = 33
POS_BANDS = (POS_EMB_DIM - 1) // 2
DECAY_TARGET = 1e-2
MAX_DECAY = -math.log(DECAY_TARGET) / 0.3
MIN_DECAY = -math.log(DECAY_TARGET) / 1.5
N_EXPERTS = 32
TOP_K = 4
SWIGLU_LIMIT = 7.0
SWIGLU_ALPHA = 1.702
EPS = 1e-6
NEG_INF = -1e30

EXPERT_ROWS = 256
COMBINE_ROWS = 256


def _cparams(*sem):
    return pltpu.CompilerParams(dimension_semantics=sem, vmem_limit_bytes=VMEM_LIMIT)


def _split(a):
    hi = a.astype(BF16)
    lo = (a - hi.astype(F32)).astype(BF16)
    return hi, lo


def _dot(a, b):
    return jnp.dot(a, b, preferred_element_type=F32)


def _dot3(a, b):
    ah, al = _split(a)
    bh, bl = _split(b)
    return _dot(ah, bh) + _dot(al, bh) + _dot(ah, bl)


def _rms(x):
    return lax.rsqrt(jnp.mean(x * x, axis=-1, keepdims=True) + EPS)


def _adaln_kernel(c_ref, w_ref, b_ref, o_ref):
    c = c_ref[...]
    o_ref[...] = _dot3(c * jax.nn.sigmoid(c), w_ref[...]) + b_ref[...]


def _adaln(c_all, w_mod, b_mod):
    rows, d = c_all.shape
    n = w_mod.shape[1]
    tn = 1024
    return pl.pallas_call(
        _adaln_kernel,
        grid=(n // tn,),
        in_specs=[pl.BlockSpec((rows, d), lambda j: (0, 0)),
                  pl.BlockSpec((d, tn), lambda j: (0, j)),
                  pl.BlockSpec((1, tn), lambda j: (0, j))],
        out_specs=pl.BlockSpec((rows, tn), lambda j: (0, j)),
        out_shape=jax.ShapeDtypeStruct((rows, n), F32),
        compiler_params=_cparams("arbitrary"),
        name="adaln",
    )(c_all, w_mod, b_mod[None])


def _head_rms(x, bd):
    hi, lo = _split(x * x)
    return x * lax.rsqrt(_dot(hi, bd) + _dot(lo, bd) + EPS)


def _rope128(x, cos, sin):
    lane = lax.broadcasted_iota(jnp.int32, x.shape, 1)
    partner = jnp.where(lane % 32 < 16, pltpu.roll(x, LANES - 16, 1), pltpu.roll(x, 16, 1))
    return x * cos + partner * sin


def _modulated(x, mod_ref, row, norm_ref, d):
    sh = mod_ref[pl.ds(row, 1), 0:d]
    sc = mod_ref[pl.ds(row, 1), d:2 * d]
    return (x * _rms(x)) * norm_ref[...] * (1 + sc) + sh


INPROJ_SUBTILES = 2


def _inproj_kernel(x_ref, mod_ref, n1_ref, w_ref, gq_ref, gk_ref, bd_ref, cos_ref, sin_ref,
                   q_ref, k_ref, v_ref, u_ref):
    d = x_ref.shape[-1]
    aw = q_ref.shape[-1]
    kw = k_ref.shape[-1]
    bd = bd_ref[...]
    sub = x_ref.shape[0] // INPROJ_SUBTILES
    for s in range(INPROJ_SUBTILES):
        rs = slice(s * sub, (s + 1) * sub)
        hb = _modulated(x_ref[rs, :], mod_ref, pl.program_id(1), n1_ref, d).astype(BF16)
        cos = cos_ref[rs, :]
        sin = sin_ref[rs, :]
        q = _head_rms(_dot(hb, w_ref[:, 0:aw]), bd) * gq_ref[...]
        for c in range(aw // LANES):
            sl = slice(c * LANES, (c + 1) * LANES)
            q_ref[rs, sl] = (_rope128(q[:, sl], cos, sin) * ATTN_SCALE).astype(BF16)
        k = _head_rms(_dot(hb, w_ref[:, aw:aw + kw]), bd[0:kw, 0:kw]) * gk_ref[...]
        for c in range(kw // LANES):
            sl = slice(c * LANES, (c + 1) * LANES)
            k_ref[rs, sl] = _rope128(k[:, sl], cos, sin).astype(BF16)
        v_ref[rs, :] = _dot(hb, w_ref[:, aw + kw:aw + 2 * kw]).astype(BF16)
        u_ref[rs, :] = _dot(hb, w_ref[:, aw + 2 * kw:])


def _inproj(x, mod, norm1, w_all, gq, gk, bd, cos_t, sin_t, tl):
    b, l, d = x.shape
    aw, kw = gq.shape[1], gk.shape[1]
    uw = w_all.shape[1] - aw - 2 * kw
    const = lambda i, j: (0, 0)
    tok = lambda i, j: (j, i, 0)
    return pl.pallas_call(
        _inproj_kernel,
        grid=(l // tl, b),
        in_specs=[pl.BlockSpec((None, tl, d), tok),
                  pl.BlockSpec(mod.shape, const),
                  pl.BlockSpec((1, d), const),
                  pl.BlockSpec(w_all.shape, const),
                  pl.BlockSpec((1, aw), const),
                  pl.BlockSpec((1, kw), const),
                  pl.BlockSpec(bd.shape, const),
                  pl.BlockSpec((tl, LANES), lambda i, j: (i, 0)),
                  pl.BlockSpec((tl, LANES), lambda i, j: (i, 0))],
        out_specs=[pl.BlockSpec((None, tl, aw), tok),
                   pl.BlockSpec((None, tl, kw), tok),
                   pl.BlockSpec((None, tl, kw), tok),
                   pl.BlockSpec((None, tl, uw), tok)],
        out_shape=[jax.ShapeDtypeStruct((b, l, aw), BF16),
                   jax.ShapeDtypeStruct((b, l, kw), BF16),
                   jax.ShapeDtypeStruct((b, l, kw), BF16),
                   jax.ShapeDtypeStruct((b, l, uw), F32)],
        compiler_params=_cparams("arbitrary", "arbitrary"),
        name="inproj",
    )(x, mod, norm1, w_all, gq, gk, bd, cos_t, sin_t)


def _ctxkv_kernel(row, x_ref, mod_ref, n1_ref, w_ref, gk_ref, bd_ref, k_ref, v_ref):
    d = x_ref.shape[-1]
    kw = k_ref.shape[-1]
    hb = _modulated(x_ref[...], mod_ref, row, n1_ref, d).astype(BF16)
    k = _head_rms(_dot(hb, w_ref[:, 0:kw]), bd_ref[...]) * gk_ref[...]
    k_ref[...] = k.astype(BF16)
    v_ref[...] = _dot(hb, w_ref[:, kw:]).astype(BF16)


def _ctxkv(ctx, mod, ctx_row, norm1, w_kv, gk, bd):
    b, lc, d = ctx.shape
    kw = gk.shape[1]
    const = lambda i: (0, 0)
    tok = lambda i: (i, 0, 0)
    return pl.pallas_call(
        functools.partial(_ctxkv_kernel, ctx_row),
        grid=(b,),
        in_specs=[pl.BlockSpec((None, lc, d), tok),
                  pl.BlockSpec(mod.shape, const),
                  pl.BlockSpec((1, d), const),
                  pl.BlockSpec(w_kv.shape, const),
                  pl.BlockSpec((1, kw), const),
                  pl.BlockSpec(bd.shape, const)],
        out_specs=[pl.BlockSpec((None, lc, kw), tok), pl.BlockSpec((None, lc, kw), tok)],
        out_shape=[jax.ShapeDtypeStruct((b, lc, kw), BF16)] * 2,
        compiler_params=_cparams("arbitrary"),
        name="ctxkv",
    )(ctx, mod, norm1, w_kv, gk, bd)


def _attn_kernel(sink_ref, q_ref, k_ref, v_ref, kx_ref, vx_ref, ga_ref, o_ref, acc_ref):
    tq = WINDOW
    nb = q_ref.shape[0] // tq
    lc = kx_ref.shape[0]
    nk = 3 * tq + lc
    pairs = GROUP // 2
    rows = pairs * tq

    def block(i, carry):
        r = lax.broadcasted_iota(jnp.int32, (rows, tq), 0) % tq
        j = lax.broadcasted_iota(jnp.int32, (rows, tq), 1)
        lo = lax.broadcasted_iota(jnp.int32, (nk, LANES), 1) < HEAD_DIM
        zero = jnp.zeros((nk, LANES), BF16)
        top = lax.broadcasted_iota(jnp.int32, (rows, 1), 0) < tq
        at = lambda blk: pl.ds(pl.multiple_of(blk * tq, tq), tq)
        cur, prev, nxt = at(i), at(jnp.maximum(i - 1, 0)), at(jnp.minimum(i + 1, nb - 1))
        ok_prev = j >= r + jnp.where(i > 0, 0, tq)
        ok_next = j <= r - jnp.where(i < nb - 1, 0, tq)
        ks, vs, qs, sinks = [], [], [], []
        for h in range(N_KV_HEADS):
            hs = slice(h * LANES, (h + 1) * LANES)
            kcat = jnp.concatenate([k_ref[prev, hs], k_ref[cur, hs], k_ref[nxt, hs], kx_ref[:, hs]], axis=0)
            vcat = jnp.concatenate([v_ref[prev, hs], v_ref[cur, hs], v_ref[nxt, hs], vx_ref[:, hs]], axis=0)
            q2 = jnp.concatenate([q_ref[cur, (h * pairs + p) * LANES:(h * pairs + p + 1) * LANES]
                                  for p in range(pairs)], axis=0)
            for half in range(2):
                ks.append(jnp.where(lo, kcat, zero) if half == 0 else jnp.where(lo, zero, kcat))
                vs.append(jnp.where(lo, vcat, zero) if half == 0 else jnp.where(lo, zero, vcat))
                qs.append(q2)
                sinks.append(jnp.where(top, sink_ref[h * GROUP + half], sink_ref[h * GROUP + 2 + half]))
        ss = [lax.dot_general(qq, kk, (((1,), (1,)), ((), ())), preferred_element_type=F32)
              for qq, kk in zip(qs, ks)]
        ss = [jnp.concatenate([jnp.where(ok_prev, s[:, 0:tq], NEG_INF), s[:, tq:2 * tq],
                               jnp.where(ok_next, s[:, 2 * tq:3 * tq], NEG_INF), s[:, 3 * tq:]], axis=1)
              for s in ss]
        ms = [jnp.maximum(jnp.max(s, axis=-1, keepdims=True), sk) for s, sk in zip(ss, sinks)]
        es = [jnp.exp(s - m) for s, m in zip(ss, ms)]
        dens = [jnp.sum(e, axis=-1, keepdims=True) + jnp.exp(sk - m) for e, sk, m in zip(es, sinks, ms)]
        outs = [_dot(e.astype(BF16), vv) / den for e, vv, den in zip(es, vs, dens)]
        for h in range(N_KV_HEADS):
            both = outs[2 * h] + outs[2 * h + 1]
            for p in range(pairs):
                acc_ref[:, (h * pairs + p) * LANES:(h * pairs + p + 1) * LANES] = both[p * tq:(p + 1) * tq]
        a = acc_ref[...]
        o_ref[cur, :] = (a * _rms(a) * ga_ref[...]).astype(BF16)
        return carry

    lax.fori_loop(0, nb, block, 0)


def _attention(sink, q, k, v, kx, vx, ga):
    b, l, aw = q.shape
    kw = k.shape[-1]
    lc = kx.shape[1]
    seq = lambda bi: (bi, 0, 0)
    return pl.pallas_call(
        _attn_kernel,
        grid=(b,),
        in_specs=[pl.BlockSpec(memory_space=pltpu.SMEM),
                  pl.BlockSpec((None, l, aw), seq),
                  pl.BlockSpec((None, l, kw), seq), pl.BlockSpec((None, l, kw), seq),
                  pl.BlockSpec((None, lc, kw), seq), pl.BlockSpec((None, lc, kw), seq),
                  pl.BlockSpec((1, aw), lambda bi: (0, 0))],
        out_specs=pl.BlockSpec((None, l, aw), seq),
        out_shape=jax.ShapeDtypeStruct((b, l, aw), BF16),
        scratch_shapes=[pltpu.VMEM((WINDOW, aw), F32)],
        compiler_params=_cparams("arbitrary"),
        name="attn",
    )(sink, q, k, v, kx, vx, ga)


def _alternating(rows, cols):
    return (1 - 2 * (lax.broadcasted_iota(jnp.int32, (rows, cols), 0) % 2)).astype(F32)


def _hconv_kernel(u0_ref, u1_ref, u2_ref, w0_ref, w1_ref, w2_ref, b0_ref, b1_ref, b2_ref,
                  x0e_ref, x0o_ref, ze_ref, zo_ref, qr_ref, qs_ref, scr):
    l, cb = u0_ref.shape
    half = l // 2
    row = lax.broadcasted_iota(jnp.int32, (l, cb), 0)

    def conv(u_ref, w_ref, b_ref):
        u = u_ref[...]
        before = jnp.where(row == 0, 0.0, pltpu.roll(u, 1, 0))
        after = jnp.where(row == l - 1, 0.0, pltpu.roll(u, l - 1, 0))
        return b_ref[...] + before * w_ref[0:1, :] + u * w_ref[1:2, :] + after * w_ref[2:3, :]

    def parity_halves(v):
        scr[...] = v
        return scr[pl.ds(0, half, stride=2), :], scr[pl.ds(1, half, stride=2), :]

    x0e_ref[...], x0o_ref[...] = parity_halves(conv(u0_ref, w0_ref, b0_ref))
    ze, zo = parity_halves(conv(u1_ref, w1_ref, b1_ref) * conv(u2_ref, w2_ref, b2_ref))
    ze_ref[...] = ze.astype(BF16)
    zo_ref[...] = zo.astype(BF16)
    sign = _alternating(half, cb)
    qr_ref[...] = jnp.sum(ze * sign, axis=0, keepdims=True)
    qs_ref[...] = jnp.sum(zo * sign, axis=0, keepdims=True)


def _hconv(u, conv_w, conv_b, cb=LANES):
    b, l, w3 = u.shape
    c = w3 // 3
    n = c // cb
    us = [pl.BlockSpec((None, l, cb), lambda bi, j, g=g: (bi, 0, g * n + j)) for g in range(3)]
    ws = [pl.BlockSpec((3, cb), lambda bi, j, g=g: (0, g * n + j)) for g in range(3)]
    bs = [pl.BlockSpec((1, cb), lambda bi, j, g=g: (0, g * n + j)) for g in range(3)]
    out = lambda bi, j: (bi, 0, j)
    seq = pl.BlockSpec((None, l // 2, cb), out)
    vec = pl.BlockSpec((None, 1, cb), out)
    return pl.pallas_call(
        _hconv_kernel,
        grid=(b, n),
        in_specs=us + ws + bs,
        out_specs=[seq, seq, seq, seq, vec, vec],
        out_shape=[jax.ShapeDtypeStruct((b, l // 2, c), F32)] * 2 + [jax.ShapeDtypeStruct((b, l // 2, c), BF16)] * 2
        + [jax.ShapeDtypeStruct((b, 1, c), F32)] * 2,
        scratch_shapes=[pltpu.VMEM((l, cb), F32)],
        compiler_params=_cparams("arbitrary", "arbitrary"),
        name="hconv",
    )(u, u, u, conv_w, conv_w, conv_w, conv_b, conv_b, conv_b)


def _filter_kernel(fe_ref, fo_ref, w1_ref, b1_ref, w2_ref, b2_ref, w3_ref, b3_ref, fr_ref, w4f_ref, w4b_ref,
                   dl_ref, kpe_ref, kpo_ref, kme_ref, kmo_ref, krq_ref, ksq_ref, he_scr, ho_scr):
    half, cf = kpe_ref.shape
    l = 2 * half

    @pl.when(pl.program_id(0) == 0)
    def _():
        fr = fr_ref[...]
        for f_ref, h_scr in ((fe_ref, he_scr), (fo_ref, ho_scr)):
            h = jnp.sin(fr * (_dot3(f_ref[...], w1_ref[...]) + b1_ref[...]))
            h = jnp.sin(fr * (_dot3(h, w2_ref[...]) + b2_ref[...]))
            h_scr[...] = jnp.sin(fr * (_dot3(h, w3_ref[...]) + b3_ref[...]))

    row = lax.broadcasted_iota(jnp.int32, (half, cf), 0)

    def taps(h_scr, parity):
        t = (2 * row + parity).astype(F32) / (l - 1)
        decay = jnp.exp(-t * dl_ref[...])
        h = h_scr[...]
        return _dot3(h, w4f_ref[...]) * decay, _dot3(h, w4b_ref[...]) * decay

    kfe, kbe = taps(he_scr, 0)
    kbe = jnp.where(row == 0, 0.0, kbe)
    kfo, kbo = taps(ho_scr, 1)
    nrm = lax.rsqrt(jnp.sum(kfe * kfe + kbe * kbe + kfo * kfo + kbo * kbo, axis=0, keepdims=True) + EPS)
    kpe = (kfe + kbe) * nrm
    kmo = (kfo - kbo) * nrm
    kpe_ref[...] = kpe.astype(BF16)
    kpo_ref[...] = ((kfo + kbo) * nrm).astype(BF16)
    kme_ref[...] = ((kfe - kbe) * nrm).astype(BF16)
    kmo_ref[...] = kmo.astype(BF16)
    sign = _alternating(half, cf)
    krq_ref[...] = jnp.sum(kpe * sign, axis=0, keepdims=True)
    ksq_ref[...] = jnp.sum(kmo * sign, axis=0, keepdims=True)


def _filter(fe, fo, w1, b1, w2, b2, w3, b3, fr, w4, deltas, cf=LANES):
    half, zw = fe.shape
    ffn = w2.shape[0]
    c = w4.shape[1] // 2
    n = c // cf
    const = lambda j: (0, 0)
    col = lambda j: (0, j)
    vec = pl.BlockSpec((1, ffn), const)
    mat = pl.BlockSpec((ffn, ffn), const)
    return pl.pallas_call(
        _filter_kernel,
        grid=(n,),
        in_specs=[pl.BlockSpec((half, zw), const), pl.BlockSpec((half, zw), const),
                  pl.BlockSpec((zw, ffn), const), vec, mat, vec, mat, vec, vec,
                  pl.BlockSpec((ffn, cf), col),
                  pl.BlockSpec((ffn, cf), lambda j: (0, n + j)),
                  pl.BlockSpec((1, cf), col)],
        out_specs=[pl.BlockSpec((half, cf), col)] * 4 + [pl.BlockSpec((1, cf), col)] * 2,
        out_shape=[jax.ShapeDtypeStruct((half, c), BF16)] * 4 + [jax.ShapeDtypeStruct((1, c), F32)] * 2,
        scratch_shapes=[pltpu.VMEM((half, ffn), F32)] * 2,
        compiler_params=_cparams("arbitrary"),
        name="filt",
    )(fe, fo, w1, b1, w2, b2, w3, b3, fr, w4, w4, deltas)


DFT_FINE = 64


def _dftgen_kernel(ca_ref, sa_ref, cb_ref, sb_ref, c_ref, s_ref):
    cb = cb_ref[...]
    sb = sb_ref[...]
    for a in range(ca_ref.shape[0]):
        ca = ca_ref[a:a + 1, :]
        sa = sa_ref[a:a + 1, :]
        rs = slice(a * DFT_FINE, (a + 1) * DFT_FINE)
        c_ref[rs, :] = (ca * cb - sa * sb).astype(BF16)
        s_ref[rs, :] = (sa * cb + ca * sb).astype(BF16)


def _dftgen(coarse, fine, l, tf, name):
    tabs = [jnp.asarray(fn((k % (2 * l)) * (math.pi / l)), F32) for k in (coarse, fine) for fn in (np.cos, np.sin)]
    rows, width = coarse.shape[0] * DFT_FINE, coarse.shape[1]
    na = tf // DFT_FINE
    tile = lambda i: (i, 0)
    const = lambda i: (0, 0)
    return pl.pallas_call(
        _dftgen_kernel,
        grid=(rows // tf,),
        in_specs=[pl.BlockSpec((na, width), tile), pl.BlockSpec((na, width), tile),
                  pl.BlockSpec((DFT_FINE, width), const), pl.BlockSpec((DFT_FINE, width), const)],
        out_specs=[pl.BlockSpec((tf, width), tile), pl.BlockSpec((tf, width), tile)],
        out_shape=[jax.ShapeDtypeStruct((rows, width), BF16)] * 2,
        compiler_params=_cparams("arbitrary"),
        name=name,
    )(*tabs)


def _dft_matrices(l, tf):
    half = l // 2
    lo = np.arange(half, dtype=np.int64)[None, :]
    a = np.arange(half // DFT_FINE, dtype=np.int64)[:, None] * DFT_FINE
    i = np.arange(DFT_FINE, dtype=np.int64)[:, None]
    ce, se = _dftgen(a * 2 * lo, i * 2 * lo, l, tf, "dft_even")
    co, so = _dftgen(a * (2 * lo + 1), i * (2 * lo + 1), l, tf, "dft_odd")
    cot, sot = _dftgen(2 * a * lo, (2 * i + 1) * lo, l, tf, "dft_odd_t")
    return ce, se, co, so, cot, sot


def _kspec_kernel(n_fft, ce_ref, se_ref, co_ref, so_ref, kpe_ref, kpo_ref, kme_ref, kmo_ref,
                  krl_ref, krm_ref, ksl_ref, ksm_ref):
    tf = ce_ref.shape[0]
    f = pl.program_id(0) * tf + lax.broadcasted_iota(jnp.int32, (tf, 1), 0)
    w = jnp.where(f == 0, 1.0 / n_fft, 2.0 / n_fft)
    ce = _dot(ce_ref[...], kpe_ref[...])
    co = _dot(co_ref[...], kpo_ref[...])
    se = _dot(se_ref[...], kme_ref[...])
    so = _dot(so_ref[...], kmo_ref[...])
    krl_ref[...] = (ce + co) * w
    krm_ref[...] = (ce - co) * w
    ksl_ref[...] = (so + se) * w
    ksm_ref[...] = (so - se) * w


def _kspec(mats, kpe, kpo, kme, kmo, tf):
    half, c = kpe.shape
    const = lambda i: (0, 0)
    tile = lambda i: (i, 0)
    return pl.pallas_call(
        functools.partial(_kspec_kernel, 4 * half),
        grid=(half // tf,),
        in_specs=[pl.BlockSpec((tf, half), tile)] * 4 + [pl.BlockSpec((half, c), const)] * 4,
        out_specs=[pl.BlockSpec((tf, c), tile)] * 4,
        out_shape=[jax.ShapeDtypeStruct((half, c), F32)] * 4,
        compiler_params=_cparams("arbitrary"),
        name="kspec",
    )(*mats, kpe, kpo, kme, kmo)


def _hfwd_kernel(ce_ref, se_ref, co_ref, so_ref, ze_ref, zo_ref, krl_ref, krm_ref, ksl_ref, ksm_ref,
                 p1_ref, p2_ref, p3_ref, p4_ref):
    ze = ze_ref[...]
    zo = zo_ref[...]
    ce = _dot(ce_ref[...], ze)
    co = _dot(co_ref[...], zo)
    se = _dot(se_ref[...], ze)
    so = _dot(so_ref[...], zo)

    def product(zr, zs, kr_ref, ks_ref):
        kr = kr_ref[...]
        ks = ks_ref[...]
        return zr * kr - zs * ks, zr * ks + zs * kr

    al, bl = product(ce + co, so + se, krl_ref, ksl_ref)
    am, bm = product(ce - co, so - se, krm_ref, ksm_ref)
    p1_ref[...] = (al + am).astype(BF16)
    p2_ref[...] = (bl - bm).astype(BF16)
    p3_ref[...] = (al - am).astype(BF16)
    p4_ref[...] = (bl + bm).astype(BF16)


def _hfwd(mats, ze, zo, spec, tf):
    b, half, c = ze.shape
    tile = lambda i, bi: (i, 0)
    seq = lambda i, bi: (bi, 0, 0)
    out = lambda i, bi: (bi, i, 0)
    return pl.pallas_call(
        _hfwd_kernel,
        grid=(half // tf, b),
        in_specs=[pl.BlockSpec((tf, half), tile)] * 4 + [pl.BlockSpec((None, half, c), seq)] * 2
        + [pl.BlockSpec((tf, c), tile)] * 4,
        out_specs=[pl.BlockSpec((None, tf, c), out)] * 4,
        out_shape=[jax.ShapeDtypeStruct((b, half, c), BF16)] * 4,
        compiler_params=_cparams("arbitrary", "arbitrary"),
        name="hfwd",
    )(*mats, ze, zo, *spec)


def _hinv_kernel(ce_ref, se_ref, cot_ref, sot_ref, p1_ref, p2_ref, p3_ref, p4_ref, x0e_ref, x0o_ref,
                 ze_ref, zo_ref, qr_ref, qs_ref, krq_ref, ksq_ref, bias_ref, g_ref, o_ref, scr):
    tt, half = ce_ref.shape
    n_fft = 4 * half
    ye = _dot(ce_ref[...], p1_ref[...]) + _dot(se_ref[...], p2_ref[...])
    yo = _dot(cot_ref[...], p3_ref[...]) + _dot(sot_ref[...], p4_ref[...])
    sign = (1 - 2 * ((pl.program_id(0) * tt + lax.broadcasted_iota(jnp.int32, (tt, 1), 0)) % 2)).astype(F32)
    qr, qs, krq, ksq = qr_ref[...], qs_ref[...], krq_ref[...], ksq_ref[...]
    ye = ye + sign * ((qr * krq - qs * ksq) * (2.0 / n_fft))
    yo = yo + sign * ((qr * ksq + qs * krq) * (2.0 / n_fft))

    def finish(y, x0_ref, z_ref):
        hy = x0_ref[...] * (y + z_ref[...].astype(F32) * bias_ref[...])
        return hy * _rms(hy) * g_ref[...]

    he = finish(ye, x0e_ref, ze_ref)
    ho = finish(yo, x0o_ref, zo_ref)
    for c in range(scr.shape[0]):
        cs = slice(c * LANES, (c + 1) * LANES)
        scr[c, pl.ds(0, tt, stride=2), :] = he[:, cs]
        scr[c, pl.ds(1, tt, stride=2), :] = ho[:, cs]
        o_ref[:, cs] = scr[c].astype(BF16)


def _hinv(ce, se, cot, sot, ps, x0e, x0o, ze, zo, qr, qs, krq, ksq, bias, gain, tt):
    b, half, c = ze.shape
    tile = lambda i, bi: (i, 0)
    full = lambda i, bi: (bi, 0, 0)
    tok = lambda i, bi: (bi, i, 0)
    const = lambda i, bi: (0, 0)
    vec = pl.BlockSpec((1, c), const)
    return pl.pallas_call(
        _hinv_kernel,
        grid=(half // tt, b),
        in_specs=[pl.BlockSpec((tt, half), tile)] * 4 + [pl.BlockSpec((None, half, c), full)] * 4
        + [pl.BlockSpec((None, tt, c), tok)] * 4 + [pl.BlockSpec((None, 1, c), full)] * 2 + [vec] * 4,
        out_specs=pl.BlockSpec((None, 2 * tt, c), tok),
        out_shape=jax.ShapeDtypeStruct((b, 2 * half, c), BF16),
        scratch_shapes=[pltpu.VMEM((c // LANES, 2 * tt, LANES), F32)],
        compiler_params=_cparams("arbitrary", "arbitrary"),
        name="hinv",
    )(ce, se, cot, sot, *ps, x0e, x0o, ze, zo, qr, qs, krq, ksq, bias, gain)


def _mixout_kernel(an_ref, yn_ref, x_ref, mod_ref, wo_ref, n2_ref, wr_ref, br_ref, tri_ref,
                   xn_ref, h2_ref, route_ref, route_t_ref, cnt_ref, carry):
    bi = pl.program_id(0)
    d = x_ref.shape[-1]
    half = an_ref.shape[-1]
    tl = x_ref.shape[0]

    @pl.when((bi == 0) & (pl.program_id(1) == 0))
    def _():
        carry[...] = jnp.zeros_like(carry)

    mix = _dot(an_ref[...], wo_ref[0:half, :]) + _dot(yn_ref[...], wo_ref[half:, :])
    g1 = mod_ref[pl.ds(bi, 1), 2 * d:3 * d]
    sh2 = mod_ref[pl.ds(bi, 1), 3 * d:4 * d]
    sc2 = mod_ref[pl.ds(bi, 1), 4 * d:5 * d]
    xn = x_ref[...] + g1 * mix
    xn_ref[...] = xn
    h2 = (xn * _rms(xn)) * n2_ref[...] * (1 + sc2) + sh2
    for j in range(d // LANES):
        _token_chunk(h2_ref, 0, tl, j)[...] = h2[:, j * LANES:(j + 1) * LANES]

    logits = _dot3(h2, wr_ref[...]) + br_ref[...]
    lane = lax.broadcasted_iota(jnp.int32, (tl, LANES), 1).astype(F32)
    vals, idxs, sels = [], [], []
    cur = logits
    for _ in range(TOP_K):
        m = jnp.max(cur, axis=-1, keepdims=True)
        idx = jnp.min(jnp.where(cur == m, lane, float(LANES)), axis=-1, keepdims=True)
        sel = lane == idx
        vals.append(m)
        idxs.append(idx)
        sels.append(sel)
        cur = jnp.where(sel, -jnp.inf, cur)
    es = [jnp.exp(v - vals[0]) for v in vals]
    den = es[0] + es[1] + es[2] + es[3]
    hot = sum(s.astype(F32) for s in sels)
    before = _dot(tri_ref[...], hot.astype(BF16)) + carry[...]
    carry[...] = carry[...] + jnp.sum(hot, axis=0, keepdims=True)
    cnt_ref[...] = carry[...]
    route = jnp.zeros((tl, LANES), F32)
    for k in range(TOP_K):
        pos = jnp.sum(jnp.where(sels[k], before, 0.0), axis=-1, keepdims=True)
        route = jnp.where(lane == k, idxs[k], route)
        route = jnp.where(lane == TOP_K + k, es[k] / den, route)
        route = jnp.where(lane == 2 * TOP_K + k, pos, route)
    route_ref[...] = route
    route_t_ref[...] = route.T[0:route_t_ref.shape[0], :]


def _mixout(an, yn, x, mod, w_out, norm2, wr, br, tri, tl):
    b, l, d = x.shape
    half = an.shape[-1]
    nt = l // tl
    tok = lambda bi, i: (bi, i, 0)
    const = lambda bi, i: (0, 0)
    flat = lambda bi, i: (bi * nt + i, 0)
    return pl.pallas_call(
        _mixout_kernel,
        grid=(b, nt),
        in_specs=[pl.BlockSpec((None, tl, half), tok), pl.BlockSpec((None, tl, half), tok),
                  pl.BlockSpec((None, tl, d), tok),
                  pl.BlockSpec(mod.shape, const), pl.BlockSpec(w_out.shape, const),
                  pl.BlockSpec((1, d), const), pl.BlockSpec(wr.shape, const),
                  pl.BlockSpec((1, LANES), const), pl.BlockSpec((tl, tl), const)],
        out_specs=[pl.BlockSpec((None, tl, d), tok),
                   pl.BlockSpec((tl * SUBLANES, LANES), flat),
                   pl.BlockSpec((tl, LANES), flat),
                   pl.BlockSpec((2 * SUBLANES, tl), lambda bi, i: (0, bi * nt + i)),
                   pl.BlockSpec((1, LANES), const)],
        out_shape=[jax.ShapeDtypeStruct((b, l, d), F32),
                   jax.ShapeDtypeStruct((b * l * SUBLANES, LANES), F32),
                   jax.ShapeDtypeStruct((b * l, LANES), F32),
                   jax.ShapeDtypeStruct((2 * SUBLANES, b * l), F32),
                   jax.ShapeDtypeStruct((1, LANES), F32)],
        scratch_shapes=[pltpu.VMEM((1, LANES), F32)],
        compiler_params=_cparams("arbitrary", "arbitrary"),
        name="mixout",
    )(an, yn, x, mod, w_out, norm2, wr, br, tri)


def _token_chunk(ref, first_token, n, j):
    return ref.at[pl.ds(first_token * SUBLANES + j, n, stride=SUBLANES), :]


def _token_copy(idx_ref, r, src_hbm, dst, sem):
    first_row = lambda tok: tok * SUBLANES if isinstance(tok, int) else pl.multiple_of(tok * SUBLANES, SUBLANES)
    return pltpu.make_async_copy(src_hbm.at[pl.ds(first_row(idx_ref[0, r]), SUBLANES)],
                                 dst.at[pl.ds(first_row(r), SUBLANES)], sem)


def _gather_tokens(idx_ref, n, src_hbm, dst, sem):
    def body(r, carry):
        _token_copy(idx_ref, 2 * r, src_hbm, dst, sem).start(priority=0)
        _token_copy(idx_ref, 2 * r + 1, src_hbm, dst, sem).start(priority=1)
        return carry
    lax.fori_loop(0, n // 2, body, 0, unroll=4)


def _wait_tokens(src_hbm, dst, sem):
    pltpu.make_async_copy(src_hbm.at[pl.ds(0, dst.shape[0])], dst, sem).wait()


EXPERT_STAGES = 4
def _dispatch_kernel(last_ref, nvb_ref, dest_ref, h2_ref, xs_hbm, sbuf, zbuf, sem, zsem):
    i = pl.program_id(0)
    n = pl.num_programs(0)
    tc = h2_ref.shape[0] // SUBLANES
    slot = i % 2

    def wait_slot(s):
        for _ in range(TOP_K):
            pltpu.make_async_copy(sbuf.at[s], xs_hbm.at[pl.ds(0, tc * SUBLANES)], sem.at[s]).wait()

    @pl.when(i == 0)
    def _():
        zbuf[...] = jnp.zeros_like(zbuf)
        n_blk = xs_hbm.shape[0] // zbuf.shape[0]
        nvb = nvb_ref[0]
        targets = [(last_ref[e] >= 0, last_ref[e]) for e in range(N_EXPERTS)]
        targets += [(nvb + j < n_blk, nvb + j) for j in range(N_EXPERTS)]

        def zero_copy(blk):
            first = pl.multiple_of(blk * zbuf.shape[0], zbuf.shape[0])
            return pltpu.make_async_copy(zbuf, xs_hbm.at[pl.ds(first, zbuf.shape[0])], zsem)

        for op in ("start", "wait"):
            for needed, blk in targets:
                @pl.when(needed)
                def _():
                    getattr(zero_copy(blk), op)()

    @pl.when(i >= 2)
    def _():
        wait_slot(slot)

    sbuf[slot] = h2_ref[...]

    def body(r, carry):
        first = pl.multiple_of(r * SUBLANES, SUBLANES)
        for k in range(TOP_K):
            dst = pl.multiple_of(dest_ref[0, k * tc + r] * SUBLANES, SUBLANES)
            pltpu.make_async_copy(sbuf.at[slot, pl.ds(first, SUBLANES)], xs_hbm.at[pl.ds(dst, SUBLANES)],
                                  sem.at[slot]).start(priority=k % 2)
        return carry
    lax.fori_loop(0, tc, body, 0, unroll=2)

    @pl.when(i == n - 1)
    def _():
        wait_slot(slot)
        wait_slot(1 - slot)


def _dispatch(last_block, nvb, dest_kmaj, h2, n_blk, rows, tc):
    n = dest_kmaj.shape[0]
    assert n >= 2
    grid_spec = pltpu.PrefetchScalarGridSpec(
        num_scalar_prefetch=2,
        grid=(n,),
        in_specs=[pl.BlockSpec((None, 1, TOP_K * tc), lambda i, lb, nv: (i, 0, 0), memory_space=pltpu.SMEM),
                  pl.BlockSpec((tc * SUBLANES, LANES), lambda i, lb, nv: (i, 0))],
        out_specs=pl.BlockSpec(memory_space=pl.ANY),
        scratch_shapes=[pltpu.VMEM((2, tc * SUBLANES, LANES), F32), pltpu.VMEM((rows * SUBLANES, LANES), F32),
                        pltpu.SemaphoreType.DMA((2,)), pltpu.SemaphoreType.DMA(())],
    )
    return pl.pallas_call(
        _dispatch_kernel,
        grid_spec=grid_spec,
        out_shape=jax.ShapeDtypeStruct((n_blk * rows * SUBLANES, LANES), F32),
        compiler_params=_cparams("arbitrary"),
        name="dispatch",
    )(last_block, nvb, dest_kmaj, h2)


def _experts_kernel(be_ref, nvb_ref, xs_ref, wgu_ref, bgu_ref, wdn_ref, bdn_ref, ys_ref, xs_bf, wgu_bf, wdn_bf):
    i = pl.program_id(0)
    nvb = nvb_ref[0]
    rows = xs_bf.shape[0]
    nchunk = wgu_ref.shape[0] // LANES
    ff = wdn_ref.shape[0]
    fc = ff // EXPERT_STAGES

    @pl.when((i == 0) | (be_ref[i] != be_ref[jnp.maximum(i - 1, 0)]))
    def _():
        wgu_bf[...] = wgu_ref[...].astype(BF16)
        wdn_bf[...] = wdn_ref[...].astype(BF16)

    @pl.when(i < nvb)
    def _():
        for j in range(nchunk):
            xs_bf[:, j * LANES:(j + 1) * LANES] = _token_chunk(xs_ref, 0, rows, j)[...].astype(BF16)
        xs = xs_bf[...]
        def gate_up(c):
            cg = slice(c * fc, (c + 1) * fc)
            cu = slice(ff + c * fc, ff + (c + 1) * fc)
            return _dot(xs, wgu_bf[:, cg]) + bgu_ref[:, cg], _dot(xs, wgu_bf[:, cu]) + bgu_ref[:, cu]

        acc = None
        nxt = gate_up(0)
        for c in range(EXPERT_STAGES):
            gate, up = nxt
            if c + 1 < EXPERT_STAGES:
                nxt = gate_up(c + 1)
            gate = jnp.minimum(gate, SWIGLU_LIMIT)
            up = jnp.clip(up, -SWIGLU_LIMIT, SWIGLU_LIMIT)
            act = ((up + 1) * (gate * jax.nn.sigmoid(SWIGLU_ALPHA * gate))).astype(BF16)
            part = _dot(act, wdn_bf[c * fc:(c + 1) * fc, :])
            acc = part if acc is None else acc + part
        y = acc + bdn_ref[...]
        for j in range(nchunk):
            _token_chunk(ys_ref, 0, rows, j)[...] = y[:, j * LANES:(j + 1) * LANES]

    @pl.when(i >= nvb)
    def _():
        ys_ref[...] = jnp.zeros_like(ys_ref)


def _experts(block_e, nvb, xs, wgu, bgu, wdn, bdn, rows):
    n_blk = block_e.shape[0]
    d, ff2 = wgu.shape[1:]
    ff = wdn.shape[1]
    expert = lambda i, be, nv: (be[i], 0, 0)
    grid_spec = pltpu.PrefetchScalarGridSpec(
        num_scalar_prefetch=2,
        grid=(n_blk,),
        in_specs=[pl.BlockSpec((rows * SUBLANES, LANES), lambda i, be, nv: (jnp.minimum(i, nv[0] - 1), 0)),
                  pl.BlockSpec((None, d, ff2), expert), pl.BlockSpec((None, 1, ff2), expert),
                  pl.BlockSpec((None, ff, d), expert), pl.BlockSpec((None, 1, d), expert)],
        out_specs=pl.BlockSpec((rows * SUBLANES, LANES), lambda i, be, nv: (i, 0)),
        scratch_shapes=[pltpu.VMEM((rows, d), BF16), pltpu.VMEM((d, ff2), BF16), pltpu.VMEM((ff, d), BF16)],
    )
    return pl.pallas_call(
        _experts_kernel,
        grid_spec=grid_spec,
        out_shape=jax.ShapeDtypeStruct((n_blk * rows * SUBLANES, LANES), F32),
        compiler_params=_cparams("arbitrary"),
        name="experts",
    )(block_e, nvb, xs, wgu, bgu, wdn, bdn)


def _combine_kernel(dest_ref, destn_ref, ys_hbm, xn_ref, route_ref, g2_ref, o_ref, ybuf, sem):
    i = pl.program_id(0)
    n = pl.num_programs(0)
    tc = xn_ref.shape[0]
    slot = i % 2

    @pl.when(i == 0)
    def _():
        _gather_tokens(dest_ref, TOP_K * tc, ys_hbm, ybuf.at[0], sem.at[0])

    @pl.when(i + 1 < n)
    def _():
        _gather_tokens(destn_ref, TOP_K * tc, ys_hbm, ybuf.at[1 - slot], sem.at[1 - slot])

    _wait_tokens(ys_hbm, ybuf.at[slot], sem.at[slot])
    route = route_ref[...]
    gates = [jnp.broadcast_to(route[:, TOP_K + k:TOP_K + k + 1], (tc, LANES)) for k in range(TOP_K)]
    for j in range(xn_ref.shape[1] // LANES):
        cs = slice(j * LANES, (j + 1) * LANES)
        acc = gates[0] * _token_chunk(ybuf.at[slot], 0, tc, j)[...]
        for k in range(1, TOP_K):
            acc = acc + gates[k] * _token_chunk(ybuf.at[slot], k * tc, tc, j)[...]
        o_ref[:, cs] = xn_ref[:, cs] + g2_ref[:, cs] * acc


def _combine(dest_kmaj, ys, xn, route, g2, tc):
    t, d = xn.shape
    n = t // tc
    per_batch = t // g2.shape[0] // tc
    return pl.pallas_call(
        _combine_kernel,
        grid=(n,),
        in_specs=[pl.BlockSpec((None, 1, TOP_K * tc), lambda i: (i, 0, 0), memory_space=pltpu.SMEM),
                  pl.BlockSpec((None, 1, TOP_K * tc), lambda i: (jnp.minimum(i + 1, n - 1), 0, 0),
                               memory_space=pltpu.SMEM),
                  pl.BlockSpec(memory_space=pl.ANY),
                  pl.BlockSpec((tc, d), lambda i: (i, 0)),
                  pl.BlockSpec((tc, LANES), lambda i: (i, 0)),
                  pl.BlockSpec((None, 1, d), lambda i: (i // per_batch, 0, 0))],
        out_specs=pl.BlockSpec((tc, d), lambda i: (i, 0)),
        out_shape=jax.ShapeDtypeStruct((t, d), F32),
        scratch_shapes=[pltpu.VMEM((2, TOP_K * tc * SUBLANES, LANES), F32), pltpu.SemaphoreType.DMA((2,))],
        compiler_params=_cparams("arbitrary"),
        name="combine",
    )(dest_kmaj, dest_kmaj, ys, xn, route, g2)


def _rope_tables(l):
    n_freq = HEAD_DIM // 4
    inv_freq = ROPE_BASE ** (-np.arange(n_freq, dtype=np.float64) / n_freq)
    tpos = np.arange(l)
    lane = np.arange(LANES) % HEAD_DIM
    pos = np.where(lane[None, :] < HEAD_DIM // 2, (tpos // GRID_W)[:, None], (tpos % GRID_W)[:, None])
    ang = pos * inv_freq[lane % n_freq][None, :]
    sign = np.where(lane % (2 * n_freq) < n_freq, -1.0, 1.0)[None, :]
    return jnp.asarray(np.cos(ang), F32), jnp.asarray(np.sin(ang) * sign, F32)


def _filter_features(l, width):
    t = np.linspace(0.0, 1.0, l)[:, None]
    w = 2.0 * math.pi * np.arange(l)[:, None] / l
    bands = np.linspace(1e-4, POS_BANDS - 1, POS_BANDS)[None, :]
    z = np.concatenate([t, np.cos(bands * w), -np.sin(bands * w)], axis=-1)
    return np.pad(z, ((0, 0), (0, width - z.shape[1]))).astype(np.float32)


def kernel(x, c, ctx, c_ctx, w_mod, b_mod, norm1, norm2, w_in, q_norm, k_norm, sink, conv_w, conv_b,
           filt_w1, filt_b1, filt_w2, filt_b2, filt_w3, filt_b3, filt_w4, filt_freq, filt_bias,
           attn_out_norm, hyena_out_norm, w_out, w_router, b_router, w_gu, b_gu, w_down, b_down):
    assert w_mod.shape[0] == 1, "single-layer configuration"
    b, l, d = x.shape
    assert d == SUBLANES * LANES, "token-tiled rows assume one (8, 128) tile per token"
    t = b * l
    aw = N_Q_HEADS * HEAD_DIM
    kvw = N_KV_HEADS * HEAD_DIM
    hw = conv_w.shape[-1] // 3
    tl = min(512, l)

    ctx_row = b
    pad_rows = -(b + 1) % SUBLANES
    c_all = jnp.concatenate([c, c_ctx[None], jnp.zeros((pad_rows, d), F32)], axis=0)
    mod = _adaln(c_all, w_mod[0], b_mod[0])

    w = w_in[0]
    wq, wk, wv, wu = w[:, :aw], w[:, aw:aw + kvw], w[:, aw + kvw:aw + 2 * kvw], w[:, aw + 2 * kvw:]
    dup = lambda m: jnp.concatenate([m[:, h * HEAD_DIM:(h + 1) * HEAD_DIM]
                                     for h in range(N_KV_HEADS) for _ in range(2)], axis=1)
    w_all = jnp.concatenate([wq, dup(wk), dup(wv), wu], axis=1).astype(BF16)
    w_kv = jnp.concatenate([dup(wk), dup(wv)], axis=1).astype(BF16)
    gq = jnp.tile(q_norm[0], N_Q_HEADS)[None]
    gk = jnp.tile(k_norm[0], 2 * N_KV_HEADS)[None]
    bd = jnp.asarray(np.kron(np.eye(N_Q_HEADS), np.full((HEAD_DIM, HEAD_DIM), 1.0 / HEAD_DIM)), BF16)
    cos_t, sin_t = _rope_tables(l)

    q, k, v, u = _inproj(x, mod, norm1, w_all, gq, gk, bd, cos_t, sin_t, tl)
    kx, vx = _ctxkv(ctx, mod, ctx_row, norm1, w_kv, gk, bd[:2 * kvw, :2 * kvw])
    an = _attention(sink[0], q, k, v, kx, vx, attn_out_norm)

    x0e, x0o, ze, zo, qr, qs = _hconv(u, conv_w[0], conv_b)
    ffn = filt_w2.shape[-1]
    zf = _filter_features(l, ffn)
    w1 = jnp.pad(filt_w1[0], ((0, ffn - POS_EMB_DIM), (0, 0)))
    deltas = jnp.asarray(np.linspace(MIN_DECAY, MAX_DECAY, hw)[None, :], F32)
    kpe, kpo, kme, kmo, krq, ksq = _filter(jnp.asarray(zf[0::2]), jnp.asarray(zf[1::2]), w1, filt_b1, filt_w2[0],
                                           filt_b2, filt_w3[0], filt_b3, filt_freq, filt_w4[0], deltas)
    tf = min(512, l // 2)
    ce, se, co, so, cot, sot = _dft_matrices(l, tf)
    spec = _kspec((ce, se, co, so), kpe, kpo, kme, kmo, tf)
    ps = _hfwd((ce, se, co, so), ze, zo, spec, tf)
    yn = _hinv(ce, se, cot, sot, ps, x0e, x0o, ze, zo, qr, qs, krq, ksq, filt_bias, hyena_out_norm,
               min(256, l // 2))

    wr = jnp.pad(w_router[0], ((0, 0), (0, LANES - N_EXPERTS)))
    br = jnp.concatenate([b_router[0], jnp.full((LANES - N_EXPERTS,), NEG_INF, F32)])[None]
    tri = jnp.asarray(np.tril(np.ones((tl, tl)), -1), BF16)
    xn, h2, route, route_t, cnt = _mixout(an, yn, x, mod, w_out[0].astype(BF16), norm2, wr, br, tri, tl)

    rows = EXPERT_ROWS
    a_tot = t * TOP_K
    n_blk = -(-a_tot // rows) + N_EXPERTS
    idx = route_t[0:TOP_K].astype(jnp.int32)
    pos = route_t[2 * TOP_K:3 * TOP_K].astype(jnp.int32)
    counts = cnt[0, :N_EXPERTS].astype(jnp.int32)
    pcounts = (counts + rows - 1) // rows * rows
    pends = jnp.cumsum(pcounts)
    pstarts = pends - pcounts
    experts = jnp.arange(N_EXPERTS, dtype=jnp.int32)[:, None, None]
    dest = jnp.sum(jnp.where(idx[None] == experts, pstarts[:, None, None], 0), axis=0) + pos
    block_start = jnp.arange(n_blk, dtype=jnp.int32) * rows
    block_e = jnp.minimum(jnp.sum(pends[None, :] <= block_start[:, None], axis=1), N_EXPERTS - 1).astype(jnp.int32)
    last_block = jnp.where(pcounts > 0, pends // rows - 1, -1).astype(jnp.int32)
    nvb = (pends[-1:] // rows).astype(jnp.int32)
    tc = min(COMBINE_ROWS, l)
    dest_kmaj = dest.reshape(TOP_K, t // tc, tc).transpose(1, 0, 2).reshape(t // tc, 1, TOP_K * tc)

    xs = _dispatch(last_block, nvb, dest_kmaj, h2, n_blk, rows, tc)
    ys = _experts(block_e, nvb, xs, w_gu[0], b_gu[0][:, None, :], w_down[0], b_down[0][:, None, :], rows)

    g2 = mod[:b, None, 5 * d:6 * d]
    out = _combine(dest_kmaj, ys, xn.reshape(t, d), route, g2, tc)
    return out.reshape(b, l, d)
```

```python
import functools
import math

import numpy as np
import jax
import jax.numpy as jnp
from jax import lax
from jax.experimental import pallas as pl
from jax.experimental.pallas import tpu as pltpu

F32 = jnp.float32
BF16 = jnp.bfloat16

LANES = 128
SUBLANES = 8
VMEM_LIMIT = 56 * 1024 * 1024

HEAD_DIM = 64
N_Q_HEADS = 8
N_KV_HEADS = 2
GROUP = N_Q_HEADS // N_KV_HEADS
GRID_W = 64
WINDOW = 128
ROPE_BASE = 10000.0
ATTN_SCALE = HEAD_DIM ** -0.5
LOG2E = math.log2(math.e)
POS_EMB_DIM = 33
POS_BANDS = (POS_EMB_DIM - 1) // 2
DECAY_TARGET = 1e-2
MAX_DECAY = -math.log(DECAY_TARGET) / 0.3
MIN_DECAY = -math.log(DECAY_TARGET) / 1.5
N_EXPERTS = 32
TOP_K = 4
SWIGLU_LIMIT = 7.0
SWIGLU_ALPHA = 1.702
EPS = 1e-6
NEG_INF = -1e30

EXPERT_ROWS = 256
COMBINE_ROWS = 256


def _cparams(*sem):
    return pltpu.CompilerParams(dimension_semantics=sem, vmem_limit_bytes=VMEM_LIMIT)


def _split(a):
    hi = a.astype(BF16)
    lo = (a - hi.astype(F32)).astype(BF16)
    return hi, lo


def _dot(a, b):
    return jnp.dot(a, b, preferred_element_type=F32)


def _dot3(a, b):
    ah, al = _split(a)
    bh, bl = _split(b)
    return _dot(ah, bh) + _dot(al, bh) + _dot(ah, bl)


def _rms(x):
    return lax.rsqrt(jnp.mean(x * x, axis=-1, keepdims=True) + EPS)


def _adaln_kernel(c_ref, w_ref, b_ref, o_ref):
    c = c_ref[...]
    o_ref[...] = _dot3(c * jax.nn.sigmoid(c), w_ref[...]) + b_ref[...]


def _adaln(c_all, w_mod, b_mod):
    rows, d = c_all.shape
    n = w_mod.shape[1]
    tn = 1024
    return pl.pallas_call(
        _adaln_kernel,
        grid=(n // tn,),
        in_specs=[pl.BlockSpec((rows, d), lambda j: (0, 0)),
                  pl.BlockSpec((d, tn), lambda j: (0, j)),
                  pl.BlockSpec((1, tn), lambda j: (0, j))],
        out_specs=pl.BlockSpec((rows, tn), lambda j: (0, j)),
        out_shape=jax.ShapeDtypeStruct((rows, n), F32),
        compiler_params=_cparams("arbitrary"),
        name="adaln",
    )(c_all, w_mod, b_mod[None])


def _head_rms(x, bd):
    hi, lo = _split(x * x)
    return x * lax.rsqrt(_dot(hi, bd) + _dot(lo, bd) + EPS)


def _rope128(x, cos, sin):
    lane = lax.broadcasted_iota(jnp.int32, x.shape, 1)
    partner = jnp.where(lane % 32 < 16, pltpu.roll(x, LANES - 16, 1), pltpu.roll(x, 16, 1))
    return x * cos + partner * sin


def _modulated(x, mod_ref, row, norm_ref, d):
    sh = mod_ref[pl.ds(row, 1), 0:d]
    sc = mod_ref[pl.ds(row, 1), d:2 * d]
    return (x * _rms(x)) * norm_ref[...] * (1 + sc) + sh


INPROJ_SUBTILES = 2


def _inproj_kernel(x_ref, mod_ref, n1_ref, w_ref, gq_ref, gk_ref, bd_ref, cos_ref, sin_ref,
                   q_ref, k_ref, v_ref, u_ref):
    d = x_ref.shape[-1]
    aw = q_ref.shape[-1]
    kw = k_ref.shape[-1]
    bd = bd_ref[...]
    sub = x_ref.shape[0] // INPROJ_SUBTILES
    for s in range(INPROJ_SUBTILES):
        rs = slice(s * sub, (s + 1) * sub)
        hb = _modulated(x_ref[rs, :], mod_ref, pl.program_id(1), n1_ref, d).astype(BF16)
        cos = cos_ref[rs, :]
        sin = sin_ref[rs, :]
        q = _head_rms(_dot(hb, w_ref[:, 0:aw]), bd) * gq_ref[...]
        for c in range(aw // LANES):
            sl = slice(c * LANES, (c + 1) * LANES)
            q_ref[rs, sl] = (_rope128(q[:, sl], cos, sin) * (ATTN_SCALE * LOG2E)).astype(BF16)
        k = _head_rms(_dot(hb, w_ref[:, aw:aw + kw]), bd[0:kw, 0:kw]) * gk_ref[...]
        for c in range(kw // LANES):
            sl = slice(c * LANES, (c + 1) * LANES)
            k_ref[rs, sl] = _rope128(k[:, sl], cos, sin).astype(BF16)
        v_ref[rs, :] = _dot(hb, w_ref[:, aw + kw:aw + 2 * kw]).astype(BF16)
        u_ref[rs, :] = _dot(hb, w_ref[:, aw + 2 * kw:])


def _inproj(x, mod, norm1, w_all, gq, gk, bd, cos_t, sin_t, tl):
    b, l, d = x.shape
    aw, kw = gq.shape[1], gk.shape[1]
    uw = w_all.shape[1] - aw - 2 * kw
    const = lambda i, j: (0, 0)
    tok = lambda i, j: (j, i, 0)
    return pl.pallas_call(
        _inproj_kernel,
        grid=(l // tl, b),
        in_specs=[pl.BlockSpec((None, tl, d), tok),
                  pl.BlockSpec(mod.shape, const),
                  pl.BlockSpec((1, d), const),
                  pl.BlockSpec(w_all.shape, const),
                  pl.BlockSpec((1, aw), const),
                  pl.BlockSpec((1, kw), const),
                  pl.BlockSpec(bd.shape, const),
                  pl.BlockSpec((tl, LANES), lambda i, j: (i, 0)),
                  pl.BlockSpec((tl, LANES), lambda i, j: (i, 0))],
        out_specs=[pl.BlockSpec((None, tl, aw), tok),
                   pl.BlockSpec((None, tl, kw), tok),
                   pl.BlockSpec((None, tl, kw), tok),
                   pl.BlockSpec((None, tl, uw), tok)],
        out_shape=[jax.ShapeDtypeStruct((b, l, aw), BF16),
                   jax.ShapeDtypeStruct((b, l, kw), BF16),
                   jax.ShapeDtypeStruct((b, l, kw), BF16),
                   jax.ShapeDtypeStruct((b, l, uw), F32)],
        compiler_params=_cparams("arbitrary", "arbitrary"),
        name="inproj",
    )(x, mod, norm1, w_all, gq, gk, bd, cos_t, sin_t)


def _ctxkv_kernel(row, x_ref, mod_ref, n1_ref, w_ref, gk_ref, bd_ref, k_ref, v_ref):
    d = x_ref.shape[-1]
    kw = k_ref.shape[-1]
    hb = _modulated(x_ref[...], mod_ref, row, n1_ref, d).astype(BF16)
    k = _head_rms(_dot(hb, w_ref[:, 0:kw]), bd_ref[...]) * gk_ref[...]
    k_ref[...] = k.astype(BF16)
    v_ref[...] = _dot(hb, w_ref[:, kw:]).astype(BF16)


def _ctxkv(ctx, mod, ctx_row, norm1, w_kv, gk, bd):
    b, lc, d = ctx.shape
    kw = gk.shape[1]
    const = lambda i: (0, 0)
    tok = lambda i: (i, 0, 0)
    return pl.pallas_call(
        functools.partial(_ctxkv_kernel, ctx_row),
        grid=(b,),
        in_specs=[pl.BlockSpec((None, lc, d), tok),
                  pl.BlockSpec(mod.shape, const),
                  pl.BlockSpec((1, d), const),
                  pl.BlockSpec(w_kv.shape, const),
                  pl.BlockSpec((1, kw), const),
                  pl.BlockSpec(bd.shape, const)],
        out_specs=[pl.BlockSpec((None, lc, kw), tok), pl.BlockSpec((None, lc, kw), tok)],
        out_shape=[jax.ShapeDtypeStruct((b, lc, kw), BF16)] * 2,
        compiler_params=_cparams("arbitrary"),
        name="ctxkv",
    )(ctx, mod, norm1, w_kv, gk, bd)


def _attn_kernel(sink_ref, q_ref, k_ref, v_ref, kx_ref, vx_ref, ga_ref, o_ref, acc_ref):
    tq = WINDOW
    nb = q_ref.shape[0] // tq
    lc = kx_ref.shape[0]
    nk = 3 * tq + lc
    pairs = GROUP // 2
    rows = pairs * tq

    def block(i, carry):
        r = lax.broadcasted_iota(jnp.int32, (rows, tq), 0) % tq
        j = lax.broadcasted_iota(jnp.int32, (rows, tq), 1)
        lo = lax.broadcasted_iota(jnp.int32, (nk, LANES), 1) < HEAD_DIM
        zero = jnp.zeros((nk, LANES), BF16)
        top = lax.broadcasted_iota(jnp.int32, (rows, 1), 0) < tq
        at = lambda blk: pl.ds(pl.multiple_of(blk * tq, tq), tq)
        cur, prev, nxt = at(i), at(jnp.maximum(i - 1, 0)), at(jnp.minimum(i + 1, nb - 1))
        ok_prev = j >= r + jnp.where(i > 0, 0, tq)
        ok_next = j <= r - jnp.where(i < nb - 1, 0, tq)
        ks, vs, qs, sinks = [], [], [], []
        for h in range(N_KV_HEADS):
            hs = slice(h * LANES, (h + 1) * LANES)
            kcat = jnp.concatenate([k_ref[prev, hs], k_ref[cur, hs], k_ref[nxt, hs], kx_ref[:, hs]], axis=0)
            vcat = jnp.concatenate([v_ref[prev, hs], v_ref[cur, hs], v_ref[nxt, hs], vx_ref[:, hs]], axis=0)
            q2 = jnp.concatenate([q_ref[cur, (h * pairs + p) * LANES:(h * pairs + p + 1) * LANES]
                                  for p in range(pairs)], axis=0)
            for half in range(2):
                ks.append(jnp.where(lo, kcat, zero) if half == 0 else jnp.where(lo, zero, kcat))
                vs.append(jnp.where(lo, vcat, zero) if half == 0 else jnp.where(lo, zero, vcat))
                qs.append(q2)
                sinks.append(jnp.where(top, sink_ref[h * GROUP + half], sink_ref[h * GROUP + 2 + half]) * LOG2E)
        ss = [lax.dot_general(qq, kk, (((1,), (1,)), ((), ())), preferred_element_type=F32)
              for qq, kk in zip(qs, ks)]
        ss = [jnp.concatenate([jnp.where(ok_prev, s[:, 0:tq], NEG_INF), s[:, tq:2 * tq],
                               jnp.where(ok_next, s[:, 2 * tq:3 * tq], NEG_INF), s[:, 3 * tq:]], axis=1)
              for s in ss]
        ms = [jnp.maximum(jnp.max(s, axis=-1, keepdims=True), sk) for s, sk in zip(ss, sinks)]
        es = [jnp.exp2(s - m) for s, m in zip(ss, ms)]
        dens = [jnp.sum(e, axis=-1, keepdims=True) + jnp.exp2(sk - m) for e, sk, m in zip(es, sinks, ms)]
        outs = [_dot(e.astype(BF16), vv) / den for e, vv, den in zip(es, vs, dens)]
        for h in range(N_KV_HEADS):
            both = outs[2 * h] + outs[2 * h + 1]
            for p in range(pairs):
                acc_ref[:, (h * pairs + p) * LANES:(h * pairs + p + 1) * LANES] = both[p * tq:(p + 1) * tq]
        a = acc_ref[...]
        o_ref[cur, :] = (a * _rms(a) * ga_ref[...]).astype(BF16)
        return carry

    lax.fori_loop(0, nb, block, 0)


def _attention(sink, q, k, v, kx, vx, ga):
    b, l, aw = q.shape
    kw = k.shape[-1]
    lc = kx.shape[1]
    seq = lambda bi: (bi, 0, 0)
    return pl.pallas_call(
        _attn_kernel,
        grid=(b,),
        in_specs=[pl.BlockSpec(memory_space=pltpu.SMEM),
                  pl.BlockSpec((None, l, aw), seq),
                  pl.BlockSpec((None, l, kw), seq), pl.BlockSpec((None, l, kw), seq),
                  pl.BlockSpec((None, lc, kw), seq), pl.BlockSpec((None, lc, kw), seq),
                  pl.BlockSpec((1, aw), lambda bi: (0, 0))],
        out_specs=pl.BlockSpec((None, l, aw), seq),
        out_shape=jax.ShapeDtypeStruct((b, l, aw), BF16),
        scratch_shapes=[pltpu.VMEM((WINDOW, aw), F32)],
        compiler_params=_cparams("arbitrary"),
        name="attn",
    )(sink, q, k, v, kx, vx, ga)


def _alternating(rows, cols):
    return (1 - 2 * (lax.broadcasted_iota(jnp.int32, (rows, cols), 0) % 2)).astype(F32)


def _hconv_kernel(u0_ref, u1_ref, u2_ref, w0_ref, w1_ref, w2_ref, b0_ref, b1_ref, b2_ref,
                  x0e_ref, x0o_ref, ze_ref, zo_ref, qr_ref, qs_ref, scr):
    l, cb = u0_ref.shape
    half = l // 2
    row = lax.broadcasted_iota(jnp.int32, (l, cb), 0)

    def conv(u_ref, w_ref, b_ref):
        u = u_ref[...]
        before = jnp.where(row == 0, 0.0, pltpu.roll(u, 1, 0))
        after = jnp.where(row == l - 1, 0.0, pltpu.roll(u, l - 1, 0))
        return b_ref[...] + before * w_ref[0:1, :] + u * w_ref[1:2, :] + after * w_ref[2:3, :]

    def parity_halves(v):
        scr[...] = v
        return scr[pl.ds(0, half, stride=2), :], scr[pl.ds(1, half, stride=2), :]

    x0e_ref[...], x0o_ref[...] = parity_halves(conv(u0_ref, w0_ref, b0_ref))
    ze, zo = parity_halves(conv(u1_ref, w1_ref, b1_ref) * conv(u2_ref, w2_ref, b2_ref))
    ze_ref[...] = ze.astype(BF16)
    zo_ref[...] = zo.astype(BF16)
    sign = _alternating(half, cb)
    qr_ref[...] = jnp.sum(ze * sign, axis=0, keepdims=True)
    qs_ref[...] = jnp.sum(zo * sign, axis=0, keepdims=True)


def _hconv(u, conv_w, conv_b, cb=LANES):
    b, l, w3 = u.shape
    c = w3 // 3
    n = c // cb
    us = [pl.BlockSpec((None, l, cb), lambda bi, j, g=g: (bi, 0, g * n + j)) for g in range(3)]
    ws = [pl.BlockSpec((3, cb), lambda bi, j, g=g: (0, g * n + j)) for g in range(3)]
    bs = [pl.BlockSpec((1, cb), lambda bi, j, g=g: (0, g * n + j)) for g in range(3)]
    out = lambda bi, j: (bi, 0, j)
    seq = pl.BlockSpec((None, l // 2, cb), out)
    vec = pl.BlockSpec((None, 1, cb), out)
    return pl.pallas_call(
        _hconv_kernel,
        grid=(b, n),
        in_specs=us + ws + bs,
        out_specs=[seq, seq, seq, seq, vec, vec],
        out_shape=[jax.ShapeDtypeStruct((b, l // 2, c), F32)] * 2 + [jax.ShapeDtypeStruct((b, l // 2, c), BF16)] * 2
        + [jax.ShapeDtypeStruct((b, 1, c), F32)] * 2,
        scratch_shapes=[pltpu.VMEM((l, cb), F32)],
        compiler_params=_cparams("arbitrary", "arbitrary"),
        name="hconv",
    )(u, u, u, conv_w, conv_w, conv_w, conv_b, conv_b, conv_b)


def _filter_kernel(fe_ref, fo_ref, w1_ref, b1_ref, w2_ref, b2_ref, w3_ref, b3_ref, fr_ref, w4f_ref, w4b_ref,
                   dl_ref, kpe_ref, kpo_ref, kme_ref, kmo_ref, krq_ref, ksq_ref, he_scr, ho_scr):
    half, cf = kpe_ref.shape
    l = 2 * half

    @pl.when(pl.program_id(0) == 0)
    def _():
        fr = fr_ref[...]
        for f_ref, h_scr in ((fe_ref, he_scr), (fo_ref, ho_scr)):
            h = jnp.sin(fr * (_dot3(f_ref[...], w1_ref[...]) + b1_ref[...]))
            h = jnp.sin(fr * (_dot3(h, w2_ref[...]) + b2_ref[...]))
            h_scr[...] = jnp.sin(fr * (_dot3(h, w3_ref[...]) + b3_ref[...]))

    row = lax.broadcasted_iota(jnp.int32, (half, cf), 0)

    def taps(h_scr, parity):
        t = (2 * row + parity).astype(F32) / (l - 1)
        decay = jnp.exp(-t * dl_ref[...])
        h = h_scr[...]
        return _dot3(h, w4f_ref[...]) * decay, _dot3(h, w4b_ref[...]) * decay

    kfe, kbe = taps(he_scr, 0)
    kbe = jnp.where(row == 0, 0.0, kbe)
    kfo, kbo = taps(ho_scr, 1)
    nrm = lax.rsqrt(jnp.sum(kfe * kfe + kbe * kbe + kfo * kfo + kbo * kbo, axis=0, keepdims=True) + EPS)
    kpe = (kfe + kbe) * nrm
    kmo = (kfo - kbo) * nrm
    kpe_ref[...] = kpe.astype(BF16)
    kpo_ref[...] = ((kfo + kbo) * nrm).astype(BF16)
    kme_ref[...] = ((kfe - kbe) * nrm).astype(BF16)
    kmo_ref[...] = kmo.astype(BF16)
    sign = _alternating(half, cf)
    krq_ref[...] = jnp.sum(kpe * sign, axis=0, keepdims=True)
    ksq_ref[...] = jnp.sum(kmo * sign, axis=0, keepdims=True)


def _filter(fe, fo, w1, b1, w2, b2, w3, b3, fr, w4, deltas, cf=LANES):
    half, zw = fe.shape
    ffn = w2.shape[0]
    c = w4.shape[1] // 2
    n = c // cf
    const = lambda j: (0, 0)
    col = lambda j: (0, j)
    vec = pl.BlockSpec((1, ffn), const)
    mat = pl.BlockSpec((ffn, ffn), const)
    return pl.pallas_call(
        _filter_kernel,
        grid=(n,),
        in_specs=[pl.BlockSpec((half, zw), const), pl.BlockSpec((half, zw), const),
                  pl.BlockSpec((zw, ffn), const), vec, mat, vec, mat, vec, vec,
                  pl.BlockSpec((ffn, cf), col),
                  pl.BlockSpec((ffn, cf), lambda j: (0, n + j)),
                  pl.BlockSpec((1, cf), col)],
        out_specs=[pl.BlockSpec((half, cf), col)] * 4 + [pl.BlockSpec((1, cf), col)] * 2,
        out_shape=[jax.ShapeDtypeStruct((half, c), BF16)] * 4 + [jax.ShapeDtypeStruct((1, c), F32)] * 2,
        scratch_shapes=[pltpu.VMEM((half, ffn), F32)] * 2,
        compiler_params=_cparams("arbitrary"),
        name="filt",
    )(fe, fo, w1, b1, w2, b2, w3, b3, fr, w4, w4, deltas)


DFT_FINE = 64


def _dftgen_kernel(ca_ref, sa_ref, cb_ref, sb_ref, c_ref, s_ref):
    cb = cb_ref[...]
    sb = sb_ref[...]
    for a in range(ca_ref.shape[0]):
        ca = ca_ref[a:a + 1, :]
        sa = sa_ref[a:a + 1, :]
        rs = slice(a * DFT_FINE, (a + 1) * DFT_FINE)
        c_ref[rs, :] = (ca * cb - sa * sb).astype(BF16)
        s_ref[rs, :] = (sa * cb + ca * sb).astype(BF16)


def _dftgen(coarse, fine, l, tf, name):
    tabs = [jnp.asarray(fn((k % (2 * l)) * (math.pi / l)), F32) for k in (coarse, fine) for fn in (np.cos, np.sin)]
    rows, width = coarse.shape[0] * DFT_FINE, coarse.shape[1]
    na = tf // DFT_FINE
    tile = lambda i: (i, 0)
    const = lambda i: (0, 0)
    return pl.pallas_call(
        _dftgen_kernel,
        grid=(rows // tf,),
        in_specs=[pl.BlockSpec((na, width), tile), pl.BlockSpec((na, width), tile),
                  pl.BlockSpec((DFT_FINE, width), const), pl.BlockSpec((DFT_FINE, width), const)],
        out_specs=[pl.BlockSpec((tf, width), tile), pl.BlockSpec((tf, width), tile)],
        out_shape=[jax.ShapeDtypeStruct((rows, width), BF16)] * 2,
        compiler_params=_cparams("arbitrary"),
        name=name,
    )(*tabs)


def _dft_matrices(l, tf):
    half = l // 2
    lo = np.arange(half, dtype=np.int64)[None, :]
    a = np.arange(half // DFT_FINE, dtype=np.int64)[:, None] * DFT_FINE
    i = np.arange(DFT_FINE, dtype=np.int64)[:, None]
    ce, se = _dftgen(a * 2 * lo, i * 2 * lo, l, tf, "dft_even")
    co, so = _dftgen(a * (2 * lo + 1), i * (2 * lo + 1), l, tf, "dft_odd")
    cot, sot = _dftgen(2 * a * lo, (2 * i + 1) * lo, l, tf, "dft_odd_t")
    return ce, se, co, so, cot, sot


def _kspec_kernel(n_fft, ce_ref, se_ref, co_ref, so_ref, kpe_ref, kpo_ref, kme_ref, kmo_ref,
                  krl_ref, krm_ref, ksl_ref, ksm_ref):
    tf = ce_ref.shape[0]
    f = pl.program_id(0) * tf + lax.broadcasted_iota(jnp.int32, (tf, 1), 0)
    w = jnp.where(f == 0, 1.0 / n_fft, 2.0 / n_fft)
    ce = _dot(ce_ref[...], kpe_ref[...])
    co = _dot(co_ref[...], kpo_ref[...])
    se = _dot(se_ref[...], kme_ref[...])
    so = _dot(so_ref[...], kmo_ref[...])
    krl_ref[...] = (ce + co) * w
    krm_ref[...] = (ce - co) * w
    ksl_ref[...] = (so + se) * w
    ksm_ref[...] = (so - se) * w


def _kspec(mats, kpe, kpo, kme, kmo, tf):
    half, c = kpe.shape
    const = lambda i: (0, 0)
    tile = lambda i: (i, 0)
    return pl.pallas_call(
        functools.partial(_kspec_kernel, 4 * half),
        grid=(half // tf,),
        in_specs=[pl.BlockSpec((tf, half), tile)] * 4 + [pl.BlockSpec((half, c), const)] * 4,
        out_specs=[pl.BlockSpec((tf, c), tile)] * 4,
        out_shape=[jax.ShapeDtypeStruct((half, c), F32)] * 4,
        compiler_params=_cparams("arbitrary"),
        name="kspec",
    )(*mats, kpe, kpo, kme, kmo)


def _hfwd_kernel(ce_ref, se_ref, co_ref, so_ref, ze_ref, zo_ref, krl_ref, krm_ref, ksl_ref, ksm_ref,
                 p1_ref, p2_ref, p3_ref, p4_ref):
    ze = ze_ref[...]
    zo = zo_ref[...]
    ce = _dot(ce_ref[...], ze)
    co = _dot(co_ref[...], zo)
    se = _dot(se_ref[...], ze)
    so = _dot(so_ref[...], zo)

    def product(zr, zs, kr_ref, ks_ref):
        kr = kr_ref[...]
        ks = ks_ref[...]
        return zr * kr - zs * ks, zr * ks + zs * kr

    al, bl = product(ce + co, so + se, krl_ref, ksl_ref)
    am, bm = product(ce - co, so - se, krm_ref, ksm_ref)
    p1_ref[...] = (al + am).astype(BF16)
    p2_ref[...] = (bl - bm).astype(BF16)
    p3_ref[...] = (al - am).astype(BF16)
    p4_ref[...] = (bl + bm).astype(BF16)


def _hfwd(mats, ze, zo, spec, tf):
    b, half, c = ze.shape
    tile = lambda i, bi: (i, 0)
    seq = lambda i, bi: (bi, 0, 0)
    out = lambda i, bi: (bi, i, 0)
    return pl.pallas_call(
        _hfwd_kernel,
        grid=(half // tf, b),
        in_specs=[pl.BlockSpec((tf, half), tile)] * 4 + [pl.BlockSpec((None, half, c), seq)] * 2
        + [pl.BlockSpec((tf, c), tile)] * 4,
        out_specs=[pl.BlockSpec((None, tf, c), out)] * 4,
        out_shape=[jax.ShapeDtypeStruct((b, half, c), BF16)] * 4,
        compiler_params=_cparams("arbitrary", "arbitrary"),
        name="hfwd",
    )(*mats, ze, zo, *spec)


def _hinv_kernel(ce_ref, se_ref, cot_ref, sot_ref, p1_ref, p2_ref, p3_ref, p4_ref, x0e_ref, x0o_ref,
                 ze_ref, zo_ref, qr_ref, qs_ref, krq_ref, ksq_ref, bias_ref, g_ref, o_ref, scr):
    tt, half = ce_ref.shape
    n_fft = 4 * half
    ye = _dot(ce_ref[...], p1_ref[...]) + _dot(se_ref[...], p2_ref[...])
    yo = _dot(cot_ref[...], p3_ref[...]) + _dot(sot_ref[...], p4_ref[...])
    sign = (1 - 2 * ((pl.program_id(0) * tt + lax.broadcasted_iota(jnp.int32, (tt, 1), 0)) % 2)).astype(F32)
    qr, qs, krq, ksq = qr_ref[...], qs_ref[...], krq_ref[...], ksq_ref[...]
    ye = ye + sign * ((qr * krq - qs * ksq) * (2.0 / n_fft))
    yo = yo + sign * ((qr * ksq + qs * krq) * (2.0 / n_fft))

    def finish(y, x0_ref, z_ref):
        hy = x0_ref[...] * (y + z_ref[...].astype(F32) * bias_ref[...])
        return hy * _rms(hy) * g_ref[...]

    he = finish(ye, x0e_ref, ze_ref)
    ho = finish(yo, x0o_ref, zo_ref)
    for c in range(scr.shape[0]):
        cs = slice(c * LANES, (c + 1) * LANES)
        scr[c, pl.ds(0, tt, stride=2), :] = he[:, cs]
        scr[c, pl.ds(1, tt, stride=2), :] = ho[:, cs]
        o_ref[:, cs] = scr[c].astype(BF16)


def _hinv(ce, se, cot, sot, ps, x0e, x0o, ze, zo, qr, qs, krq, ksq, bias, gain, tt):
    b, half, c = ze.shape
    tile = lambda i, bi: (i, 0)
    full = lambda i, bi: (bi, 0, 0)
    tok = lambda i, bi: (bi, i, 0)
    const = lambda i, bi: (0, 0)
    vec = pl.BlockSpec((1, c), const)
    return pl.pallas_call(
        _hinv_kernel,
        grid=(half // tt, b),
        in_specs=[pl.BlockSpec((tt, half), tile)] * 4 + [pl.BlockSpec((None, half, c), full)] * 4
        + [pl.BlockSpec((None, tt, c), tok)] * 4 + [pl.BlockSpec((None, 1, c), full)] * 2 + [vec] * 4,
        out_specs=pl.BlockSpec((None, 2 * tt, c), tok),
        out_shape=jax.ShapeDtypeStruct((b, 2 * half, c), BF16),
        scratch_shapes=[pltpu.VMEM((c // LANES, 2 * tt, LANES), F32)],
        compiler_params=_cparams("arbitrary", "arbitrary"),
        name="hinv",
    )(ce, se, cot, sot, *ps, x0e, x0o, ze, zo, qr, qs, krq, ksq, bias, gain)


def _mixout_kernel(an_ref, yn_ref, x_ref, mod_ref, wo_ref, n2_ref, wr_ref, br_ref, tri_ref,
                   xn_ref, h2_ref, route_ref, route_t_ref, cnt_ref, carry):
    bi = pl.program_id(0)
    d = x_ref.shape[-1]
    half = an_ref.shape[-1]
    tl = x_ref.shape[0]

    @pl.when((bi == 0) & (pl.program_id(1) == 0))
    def _():
        carry[...] = jnp.zeros_like(carry)

    mix = _dot(an_ref[...], wo_ref[0:half, :]) + _dot(yn_ref[...], wo_ref[half:, :])
    g1 = mod_ref[pl.ds(bi, 1), 2 * d:3 * d]
    sh2 = mod_ref[pl.ds(bi, 1), 3 * d:4 * d]
    sc2 = mod_ref[pl.ds(bi, 1), 4 * d:5 * d]
    xn = x_ref[...] + g1 * mix
    xn_ref[...] = xn
    h2 = (xn * _rms(xn)) * n2_ref[...] * (1 + sc2) + sh2
    for j in range(d // LANES):
        _token_chunk(h2_ref, 0, tl, j)[...] = h2[:, j * LANES:(j + 1) * LANES]

    logits = _dot3(h2, wr_ref[...]) + br_ref[...]
    lane = lax.broadcasted_iota(jnp.int32, (tl, LANES), 1).astype(F32)
    vals, idxs, sels = [], [], []
    cur = logits
    for _ in range(TOP_K):
        m = jnp.max(cur, axis=-1, keepdims=True)
        idx = jnp.min(jnp.where(cur == m, lane, float(LANES)), axis=-1, keepdims=True)
        sel = lane == idx
        vals.append(m)
        idxs.append(idx)
        sels.append(sel)
        cur = jnp.where(sel, -jnp.inf, cur)
    es = [jnp.exp(v - vals[0]) for v in vals]
    den = es[0] + es[1] + es[2] + es[3]
    hot = sum(s.astype(F32) for s in sels)
    before = _dot(tri_ref[...], hot.astype(BF16)) + carry[...]
    carry[...] = carry[...] + jnp.sum(hot, axis=0, keepdims=True)
    cnt_ref[...] = carry[...]
    route = jnp.zeros((tl, LANES), F32)
    for k in range(TOP_K):
        pos = jnp.sum(jnp.where(sels[k], before, 0.0), axis=-1, keepdims=True)
        route = jnp.where(lane == k, idxs[k], route)
        route = jnp.where(lane == TOP_K + k, es[k] / den, route)
        route = jnp.where(lane == 2 * TOP_K + k, pos, route)
    route_ref[...] = route
    route_t_ref[...] = route.T[0:route_t_ref.shape[0], :]


def _mixout(an, yn, x, mod, w_out, norm2, wr, br, tri, tl):
    b, l, d = x.shape
    half = an.shape[-1]
    nt = l // tl
    tok = lambda bi, i: (bi, i, 0)
    const = lambda bi, i: (0, 0)
    flat = lambda bi, i: (bi * nt + i, 0)
    return pl.pallas_call(
        _mixout_kernel,
        grid=(b, nt),
        in_specs=[pl.BlockSpec((None, tl, half), tok), pl.BlockSpec((None, tl, half), tok),
                  pl.BlockSpec((None, tl, d), tok),
                  pl.BlockSpec(mod.shape, const), pl.BlockSpec(w_out.shape, const),
                  pl.BlockSpec((1, d), const), pl.BlockSpec(wr.shape, const),
                  pl.BlockSpec((1, LANES), const), pl.BlockSpec((tl, tl), const)],
        out_specs=[pl.BlockSpec((None, tl, d), tok),
                   pl.BlockSpec((tl * SUBLANES, LANES), flat),
                   pl.BlockSpec((tl, LANES), flat),
                   pl.BlockSpec((2 * SUBLANES, tl), lambda bi, i: (0, bi * nt + i)),
                   pl.BlockSpec((1, LANES), const)],
        out_shape=[jax.ShapeDtypeStruct((b, l, d), F32),
                   jax.ShapeDtypeStruct((b * l * SUBLANES, LANES), F32),
                   jax.ShapeDtypeStruct((b * l, LANES), F32),
                   jax.ShapeDtypeStruct((2 * SUBLANES, b * l), F32),
                   jax.ShapeDtypeStruct((1, LANES), F32)],
        scratch_shapes=[pltpu.VMEM((1, LANES), F32)],
        compiler_params=_cparams("arbitrary", "arbitrary"),
        name="mixout",
    )(an, yn, x, mod, w_out, norm2, wr, br, tri)


def _token_chunk(ref, first_token, n, j):
    return ref.at[pl.ds(first_token * SUBLANES + j, n, stride=SUBLANES), :]


def _token_copy(idx_ref, r, src_hbm, dst, sem):
    first_row = lambda tok: tok * SUBLANES if isinstance(tok, int) else pl.multiple_of(tok * SUBLANES, SUBLANES)
    return pltpu.make_async_copy(src_hbm.at[pl.ds(first_row(idx_ref[0, r]), SUBLANES)],
                                 dst.at[pl.ds(first_row(r), SUBLANES)], sem)


def _gather_tokens(idx_ref, n, src_hbm, dst, sem):
    def body(r, carry):
        _token_copy(idx_ref, 2 * r, src_hbm, dst, sem).start(priority=0)
        _token_copy(idx_ref, 2 * r + 1, src_hbm, dst, sem).start(priority=1)
        return carry
    lax.fori_loop(0, n // 2, body, 0, unroll=16)


def _wait_tokens(src_hbm, dst, sem):
    pltpu.make_async_copy(src_hbm.at[pl.ds(0, dst.shape[0])], dst, sem).wait()


EXPERT_STAGES = 4
def _dispatch_kernel(last_ref, nvb_ref, dest_ref, h2_ref, xs_hbm, sbuf, zbuf, sem, zsem):
    i = pl.program_id(0)
    n = pl.num_programs(0)
    tc = h2_ref.shape[0] // SUBLANES
    slot = i % 2

    def wait_slot(s):
        for _ in range(TOP_K):
            pltpu.make_async_copy(sbuf.at[s], xs_hbm.at[pl.ds(0, tc * SUBLANES)], sem.at[s]).wait()

    @pl.when(i == 0)
    def _():
        zbuf[...] = jnp.zeros_like(zbuf)
        n_blk = xs_hbm.shape[0] // zbuf.shape[0]
        nvb = nvb_ref[0]
        targets = [(last_ref[e] >= 0, last_ref[e]) for e in range(N_EXPERTS)]
        targets += [(nvb + j < n_blk, nvb + j) for j in range(N_EXPERTS)]

        def zero_copy(blk):
            first = pl.multiple_of(blk * zbuf.shape[0], zbuf.shape[0])
            return pltpu.make_async_copy(zbuf, xs_hbm.at[pl.ds(first, zbuf.shape[0])], zsem)

        for op in ("start", "wait"):
            for needed, blk in targets:
                @pl.when(needed)
                def _():
                    getattr(zero_copy(blk), op)()

    @pl.when(i >= 2)
    def _():
        wait_slot(slot)

    sbuf[slot] = h2_ref[...]

    def body(r, carry):
        first = pl.multiple_of(r * SUBLANES, SUBLANES)
        for k in range(TOP_K):
            dst = pl.multiple_of(dest_ref[0, k * tc + r] * SUBLANES, SUBLANES)
            pltpu.make_async_copy(sbuf.at[slot, pl.ds(first, SUBLANES)], xs_hbm.at[pl.ds(dst, SUBLANES)],
                                  sem.at[slot]).start(priority=k % 2)
        return carry
    lax.fori_loop(0, tc, body, 0, unroll=8)

    @pl.when(i == n - 1)
    def _():
        wait_slot(slot)
        wait_slot(1 - slot)


def _dispatch(last_block, nvb, dest_kmaj, h2, n_blk, rows, tc):
    n = dest_kmaj.shape[0]
    assert n >= 2
    grid_spec = pltpu.PrefetchScalarGridSpec(
        num_scalar_prefetch=2,
        grid=(n,),
        in_specs=[pl.BlockSpec((None, 1, TOP_K * tc), lambda i, lb, nv: (i, 0, 0), memory_space=pltpu.SMEM),
                  pl.BlockSpec((tc * SUBLANES, LANES), lambda i, lb, nv: (i, 0))],
        out_specs=pl.BlockSpec(memory_space=pl.ANY),
        scratch_shapes=[pltpu.VMEM((2, tc * SUBLANES, LANES), F32), pltpu.VMEM((rows * SUBLANES, LANES), F32),
                        pltpu.SemaphoreType.DMA((2,)), pltpu.SemaphoreType.DMA(())],
    )
    return pl.pallas_call(
        _dispatch_kernel,
        grid_spec=grid_spec,
        out_shape=jax.ShapeDtypeStruct((n_blk * rows * SUBLANES, LANES), F32),
        compiler_params=_cparams("arbitrary"),
        name="dispatch",
    )(last_block, nvb, dest_kmaj, h2)


def _experts_kernel(be_ref, nvb_ref, xs_ref, wgu_ref, bgu_ref, wdn_ref, bdn_ref, ys_ref, xs_bf, wgu_bf, wdn_bf):
    i = pl.program_id(0)
    nvb = nvb_ref[0]
    rows = xs_bf.shape[0]
    nchunk = wgu_ref.shape[0] // LANES
    ff = wdn_ref.shape[0]
    fc = ff // EXPERT_STAGES

    @pl.when((i == 0) | (be_ref[i] != be_ref[jnp.maximum(i - 1, 0)]))
    def _():
        wgu_bf[...] = wgu_ref[...].astype(BF16)
        wdn_bf[...] = wdn_ref[...].astype(BF16)

    @pl.when(i < nvb)
    def _():
        for j in range(nchunk):
            xs_bf[:, j * LANES:(j + 1) * LANES] = _token_chunk(xs_ref, 0, rows, j)[...].astype(BF16)
        xs = xs_bf[...]
        def gate_up(c):
            cg = slice(c * fc, (c + 1) * fc)
            cu = slice(ff + c * fc, ff + (c + 1) * fc)
            return _dot(xs, wgu_bf[:, cg]) + bgu_ref[:, cg], _dot(xs, wgu_bf[:, cu]) + bgu_ref[:, cu]

        acc = None
        nxt = gate_up(0)
        for c in range(EXPERT_STAGES):
            gate, up = nxt
            if c + 1 < EXPERT_STAGES:
                nxt = gate_up(c + 1)
            gate = jnp.minimum(gate, SWIGLU_LIMIT)
            up = jnp.clip(up, -SWIGLU_LIMIT, SWIGLU_LIMIT)
            act = ((up + 1) * (gate * jax.nn.sigmoid(SWIGLU_ALPHA * gate))).astype(BF16)
            part = _dot(act, wdn_bf[c * fc:(c + 1) * fc, :])
            acc = part if acc is None else acc + part
        y = acc + bdn_ref[...]
        for j in range(nchunk):
            _token_chunk(ys_ref, 0, rows, j)[...] = y[:, j * LANES:(j + 1) * LANES]

    @pl.when(i >= nvb)
    def _():
        ys_ref[...] = jnp.zeros_like(ys_ref)


def _experts(block_e, nvb, xs, wgu, bgu, wdn, bdn, rows):
    n_blk = block_e.shape[0]
    d, ff2 = wgu.shape[1:]
    ff = wdn.shape[1]
    expert = lambda i, be, nv: (be[i], 0, 0)
    grid_spec = pltpu.PrefetchScalarGridSpec(
        num_scalar_prefetch=2,
        grid=(n_blk,),
        in_specs=[pl.BlockSpec((rows * SUBLANES, LANES), lambda i, be, nv: (jnp.minimum(i, nv[0] - 1), 0)),
                  pl.BlockSpec((None, d, ff2), expert), pl.BlockSpec((None, 1, ff2), expert),
                  pl.BlockSpec((None, ff, d), expert), pl.BlockSpec((None, 1, d), expert)],
        out_specs=pl.BlockSpec((rows * SUBLANES, LANES), lambda i, be, nv: (i, 0)),
        scratch_shapes=[pltpu.VMEM((rows, d), BF16), pltpu.VMEM((d, ff2), BF16), pltpu.VMEM((ff, d), BF16)],
    )
    return pl.pallas_call(
        _experts_kernel,
        grid_spec=grid_spec,
        out_shape=jax.ShapeDtypeStruct((n_blk * rows * SUBLANES, LANES), F32),
        compiler_params=_cparams("arbitrary"),
        name="experts",
    )(block_e, nvb, xs, wgu, bgu, wdn, bdn)


def _combine_kernel(dest_ref, destn_ref, ys_hbm, xn_ref, route_ref, g2_ref, o_ref, ybuf, sem):
    i = pl.program_id(0)
    n = pl.num_programs(0)
    tc = xn_ref.shape[0]
    slot = i % 2

    @pl.when(i == 0)
    def _():
        _gather_tokens(dest_ref, TOP_K * tc, ys_hbm, ybuf.at[0], sem.at[0])

    @pl.when(i + 1 < n)
    def _():
        _gather_tokens(destn_ref, TOP_K * tc, ys_hbm, ybuf.at[1 - slot], sem.at[1 - slot])

    _wait_tokens(ys_hbm, ybuf.at[slot], sem.at[slot])
    route = route_ref[...]
    gates = [jnp.broadcast_to(route[:, TOP_K + k:TOP_K + k + 1], (tc, LANES)) for k in range(TOP_K)]
    for j in range(xn_ref.shape[1] // LANES):
        cs = slice(j * LANES, (j + 1) * LANES)
        acc = gates[0] * _token_chunk(ybuf.at[slot], 0, tc, j)[...]
        for k in range(1, TOP_K):
            acc = acc + gates[k] * _token_chunk(ybuf.at[slot], k * tc, tc, j)[...]
        o_ref[:, cs] = xn_ref[:, cs] + g2_ref[:, cs] * acc


def _combine(dest_kmaj, ys, xn, route, g2, tc):
    t, d = xn.shape
    n = t // tc
    per_batch = t // g2.shape[0] // tc
    return pl.pallas_call(
        _combine_kernel,
        grid=(n,),
        in_specs=[pl.BlockSpec((None, 1, TOP_K * tc), lambda i: (i, 0, 0), memory_space=pltpu.SMEM),
                  pl.BlockSpec((None, 1, TOP_K * tc), lambda i: (jnp.minimum(i + 1, n - 1), 0, 0),
                               memory_space=pltpu.SMEM),
                  pl.BlockSpec(memory_space=pl.ANY),
                  pl.BlockSpec((tc, d), lambda i: (i, 0)),
                  pl.BlockSpec((tc, LANES), lambda i: (i, 0)),
                  pl.BlockSpec((None, 1, d), lambda i: (i // per_batch, 0, 0))],
        out_specs=pl.BlockSpec((tc, d), lambda i: (i, 0)),
        out_shape=jax.ShapeDtypeStruct((t, d), F32),
        scratch_shapes=[pltpu.VMEM((2, TOP_K * tc * SUBLANES, LANES), F32), pltpu.SemaphoreType.DMA((2,))],
        compiler_params=_cparams("arbitrary"),
        name="combine",
    )(dest_kmaj, dest_kmaj, ys, xn, route, g2)


def _rope_tables(l):
    n_freq = HEAD_DIM // 4
    inv_freq = ROPE_BASE ** (-np.arange(n_freq, dtype=np.float64) / n_freq)
    tpos = np.arange(l)
    lane = np.arange(LANES) % HEAD_DIM
    pos = np.where(lane[None, :] < HEAD_DIM // 2, (tpos // GRID_W)[:, None], (tpos % GRID_W)[:, None])
    ang = pos * inv_freq[lane % n_freq][None, :]
    sign = np.where(lane % (2 * n_freq) < n_freq, -1.0, 1.0)[None, :]
    return jnp.asarray(np.cos(ang), F32), jnp.asarray(np.sin(ang) * sign, F32)


def _filter_features(l, width):
    t = np.linspace(0.0, 1.0, l)[:, None]
    w = 2.0 * math.pi * np.arange(l)[:, None] / l
    bands = np.linspace(1e-4, POS_BANDS - 1, POS_BANDS)[None, :]
    z = np.concatenate([t, np.cos(bands * w), -np.sin(bands * w)], axis=-1)
    return np.pad(z, ((0, 0), (0, width - z.shape[1]))).astype(np.float32)


def kernel(x, c, ctx, c_ctx, w_mod, b_mod, norm1, norm2, w_in, q_norm, k_norm, sink, conv_w, conv_b,
           filt_w1, filt_b1, filt_w2, filt_b2, filt_w3, filt_b3, filt_w4, filt_freq, filt_bias,
           attn_out_norm, hyena_out_norm, w_out, w_router, b_router, w_gu, b_gu, w_down, b_down):
    assert w_mod.shape[0] == 1, "single-layer configuration"
    b, l, d = x.shape
    assert d == SUBLANES * LANES, "token-tiled rows assume one (8, 128) tile per token"
    t = b * l
    aw = N_Q_HEADS * HEAD_DIM
    kvw = N_KV_HEADS * HEAD_DIM
    hw = conv_w.shape[-1] // 3
    tl = min(512, l)

    ctx_row = b
    pad_rows = -(b + 1) % SUBLANES
    c_all = jnp.concatenate([c, c_ctx[None], jnp.zeros((pad_rows, d), F32)], axis=0)
    mod = _adaln(c_all, w_mod[0], b_mod[0])

    w = w_in[0]
    wq, wk, wv, wu = w[:, :aw], w[:, aw:aw + kvw], w[:, aw + kvw:aw + 2 * kvw], w[:, aw + 2 * kvw:]
    dup = lambda m: jnp.concatenate([m[:, h * HEAD_DIM:(h + 1) * HEAD_DIM]
                                     for h in range(N_KV_HEADS) for _ in range(2)], axis=1)
    w_all = jnp.concatenate([wq, dup(wk), dup(wv), wu], axis=1).astype(BF16)
    w_kv = jnp.concatenate([dup(wk), dup(wv)], axis=1).astype(BF16)
    gq = jnp.tile(q_norm[0], N_Q_HEADS)[None]
    gk = jnp.tile(k_norm[0], 2 * N_KV_HEADS)[None]
    bd = jnp.asarray(np.kron(np.eye(N_Q_HEADS), np.full((HEAD_DIM, HEAD_DIM), 1.0 / HEAD_DIM)), BF16)
    cos_t, sin_t = _rope_tables(l)

    q, k, v, u = _inproj(x, mod, norm1, w_all, gq, gk, bd, cos_t, sin_t, tl)
    kx, vx = _ctxkv(ctx, mod, ctx_row, norm1, w_kv, gk, bd[:2 * kvw, :2 * kvw])
    an = _attention(sink[0], q, k, v, kx, vx, attn_out_norm)

    x0e, x0o, ze, zo, qr, qs = _hconv(u, conv_w[0], conv_b)
    ffn = filt_w2.shape[-1]
    zf = _filter_features(l, ffn)
    w1 = jnp.pad(filt_w1[0], ((0, ffn - POS_EMB_DIM), (0, 0)))
    deltas = jnp.asarray(np.linspace(MIN_DECAY, MAX_DECAY, hw)[None, :], F32)
    kpe, kpo, kme, kmo, krq, ksq = _filter(jnp.asarray(zf[0::2]), jnp.asarray(zf[1::2]), w1, filt_b1, filt_w2[0],
                                           filt_b2, filt_w3[0], filt_b3, filt_freq, filt_w4[0], deltas)
    tf = min(512, l // 2)
    ce, se, co, so, cot, sot = _dft_matrices(l, tf)
    spec = _kspec((ce, se, co, so), kpe, kpo, kme, kmo, tf)
    ps = _hfwd((ce, se, co, so), ze, zo, spec, tf)
    yn = _hinv(ce, se, cot, sot, ps, x0e, x0o, ze, zo, qr, qs, krq, ksq, filt_bias, hyena_out_norm,
               min(256, l // 2))

    wr = jnp.pad(w_router[0], ((0, 0), (0, LANES - N_EXPERTS)))
    br = jnp.concatenate([b_router[0], jnp.full((LANES - N_EXPERTS,), NEG_INF, F32)])[None]
    tri = jnp.asarray(np.tril(np.ones((tl, tl)), -1), BF16)
    xn, h2, route, route_t, cnt = _mixout(an, yn, x, mod, w_out[0].astype(BF16), norm2, wr, br, tri, tl)

    rows = EXPERT_ROWS
    a_tot = t * TOP_K
    n_blk = -(-a_tot // rows) + N_EXPERTS
    idx = route_t[0:TOP_K].astype(jnp.int32)
    pos = route_t[2 * TOP_K:3 * TOP_K].astype(jnp.int32)
    counts = cnt[0, :N_EXPERTS].astype(jnp.int32)
    pcounts = (counts + rows - 1) // rows * rows
    pends = jnp.cumsum(pcounts)
    pstarts = pends - pcounts
    experts = jnp.arange(N_EXPERTS, dtype=jnp.int32)[:, None, None]
    dest = jnp.sum(jnp.where(idx[None] == experts, pstarts[:, None, None], 0), axis=0) + pos
    block_start = jnp.arange(n_blk, dtype=jnp.int32) * rows
    block_e = jnp.minimum(jnp.sum(pends[None, :] <= block_start[:, None], axis=1), N_EXPERTS - 1).astype(jnp.int32)
    last_block = jnp.where(pcounts > 0, pends // rows - 1, -1).astype(jnp.int32)
    nvb = (pends[-1:] // rows).astype(jnp.int32)
    tc = min(COMBINE_ROWS, l)
    dest_kmaj = dest.reshape(TOP_K, t // tc, tc).transpose(1, 0, 2).reshape(t // tc, 1, TOP_K * tc)

    xs = _dispatch(last_block, nvb, dest_kmaj, h2, n_blk, rows, tc)
    ys = _experts(block_e, nvb, xs, w_gu[0], b_gu[0][:, None, :], w_down[0], b_down[0][:, None, :], rows)

    g2 = mod[:b, None, 5 * d:6 * d]
    out = _combine(dest_kmaj, ys, xn.reshape(t, d), route, g2, tc)
    return out.reshape(b, l, d)
```

```python
import functools
import math

import numpy as np
import jax
import jax.numpy as jnp
from jax import lax
from jax.experimental import pallas as pl
from jax.experimental.pallas import tpu as pltpu

F32 = jnp.float32
BF16 = jnp.bfloat16

LANES = 128
SUBLANES = 8
VMEM_LIMIT = 56 * 1024 * 1024

HEAD_DIM = 64
N_Q_HEADS = 8
N_KV_HEADS = 2
GROUP = N_Q_HEADS // N_KV_HEADS
GRID_W = 64
WINDOW = 128
ROPE_BASE = 10000.0
ATTN_SCALE = HEAD_DIM ** -0.5
LOG2E = math.log2(math.e)
POS_EMB_DIM = 33
POS_BANDS = (POS_EMB_DIM - 1) // 2
DECAY_TARGET = 1e-2
MAX_DECAY = -math.log(DECAY_TARGET) / 0.3
MIN_DECAY = -math.log(DECAY_TARGET) / 1.5
N_EXPERTS = 32
TOP_K = 4
SWIGLU_LIMIT = 7.0
SWIGLU_ALPHA = 1.702
EPS = 1e-6
NEG_INF = -1e30

EXPERT_ROWS = 512
COMBINE_ROWS = 256


def _cparams(*sem):
    return pltpu.CompilerParams(dimension_semantics=sem, vmem_limit_bytes=VMEM_LIMIT)


def _split(a):
    hi = a.astype(BF16)
    lo = (a - hi.astype(F32)).astype(BF16)
    return hi, lo


def _dot(a, b):
    return jnp.dot(a, b, preferred_element_type=F32)


def _dot3(a, b):
    ah, al = _split(a)
    bh, bl = _split(b)
    return _dot(ah, bh) + _dot(al, bh) + _dot(ah, bl)


def _rms(x):
    return lax.rsqrt(jnp.mean(x * x, axis=-1, keepdims=True) + EPS)


def _adaln_kernel(c_ref, w_ref, b_ref, o_ref):
    c = c_ref[...]
    o_ref[...] = _dot3(c * jax.nn.sigmoid(c), w_ref[...]) + b_ref[...]


def _adaln(c_all, w_mod, b_mod):
    rows, d = c_all.shape
    n = w_mod.shape[1]
    tn = 1024
    return pl.pallas_call(
        _adaln_kernel,
        grid=(n // tn,),
        in_specs=[pl.BlockSpec((rows, d), lambda j: (0, 0)),
                  pl.BlockSpec((d, tn), lambda j: (0, j)),
                  pl.BlockSpec((1, tn), lambda j: (0, j))],
        out_specs=pl.BlockSpec((rows, tn), lambda j: (0, j)),
        out_shape=jax.ShapeDtypeStruct((rows, n), F32),
        compiler_params=_cparams("arbitrary"),
        name="adaln",
    )(c_all, w_mod, b_mod[None])


def _head_rms(x, bd):
    hi, lo = _split(x * x)
    return x * lax.rsqrt(_dot(hi, bd) + _dot(lo, bd) + EPS)


def _rope128(x, cos, sin):
    lane = lax.broadcasted_iota(jnp.int32, x.shape, 1)
    partner = jnp.where(lane % 32 < 16, pltpu.roll(x, LANES - 16, 1), pltpu.roll(x, 16, 1))
    return x * cos + partner * sin


def _modulated(x, mod_ref, row, norm_ref, d):
    sh = mod_ref[pl.ds(row, 1), 0:d]
    sc = mod_ref[pl.ds(row, 1), d:2 * d]
    return (x * _rms(x)) * norm_ref[...] * (1 + sc) + sh


INPROJ_SUBTILES = 2


def _inproj_kernel(x_ref, mod_ref, n1_ref, w_ref, gq_ref, gk_ref, bd_ref, cos_ref, sin_ref,
                   q_ref, k_ref, v_ref, u_ref):
    d = x_ref.shape[-1]
    aw = q_ref.shape[-1]
    kw = k_ref.shape[-1]
    bd = bd_ref[...]
    sub = x_ref.shape[0] // INPROJ_SUBTILES
    for s in range(INPROJ_SUBTILES):
        rs = slice(s * sub, (s + 1) * sub)
        hb = _modulated(x_ref[rs, :], mod_ref, pl.program_id(1), n1_ref, d).astype(BF16)
        cos = cos_ref[rs, :]
        sin = sin_ref[rs, :]
        q = _head_rms(_dot(hb, w_ref[:, 0:aw]), bd) * gq_ref[...]
        for c in range(aw // LANES):
            sl = slice(c * LANES, (c + 1) * LANES)
            q_ref[rs, sl] = (_rope128(q[:, sl], cos, sin) * (ATTN_SCALE * LOG2E)).astype(BF16)
        k = _head_rms(_dot(hb, w_ref[:, aw:aw + kw]), bd[0:kw, 0:kw]) * gk_ref[...]
        for c in range(kw // LANES):
            sl = slice(c * LANES, (c + 1) * LANES)
            k_ref[rs, sl] = _rope128(k[:, sl], cos, sin).astype(BF16)
        v_ref[rs, :] = _dot(hb, w_ref[:, aw + kw:aw + 2 * kw]).astype(BF16)
        u_ref[rs, :] = _dot(hb, w_ref[:, aw + 2 * kw:])


def _inproj(x, mod, norm1, w_all, gq, gk, bd, cos_t, sin_t, tl):
    b, l, d = x.shape
    aw, kw = gq.shape[1], gk.shape[1]
    uw = w_all.shape[1] - aw - 2 * kw
    const = lambda i, j: (0, 0)
    tok = lambda i, j: (j, i, 0)
    return pl.pallas_call(
        _inproj_kernel,
        grid=(l // tl, b),
        in_specs=[pl.BlockSpec((None, tl, d), tok),
                  pl.BlockSpec(mod.shape, const),
                  pl.BlockSpec((1, d), const),
                  pl.BlockSpec(w_all.shape, const),
                  pl.BlockSpec((1, aw), const),
                  pl.BlockSpec((1, kw), const),
                  pl.BlockSpec(bd.shape, const),
                  pl.BlockSpec((tl, LANES), lambda i, j: (i, 0)),
                  pl.BlockSpec((tl, LANES), lambda i, j: (i, 0))],
        out_specs=[pl.BlockSpec((None, tl, aw), tok),
                   pl.BlockSpec((None, tl, kw), tok),
                   pl.BlockSpec((None, tl, kw), tok),
                   pl.BlockSpec((None, tl, uw), tok)],
        out_shape=[jax.ShapeDtypeStruct((b, l, aw), BF16),
                   jax.ShapeDtypeStruct((b, l, kw), BF16),
                   jax.ShapeDtypeStruct((b, l, kw), BF16),
                   jax.ShapeDtypeStruct((b, l, uw), F32)],
        compiler_params=_cparams("arbitrary", "arbitrary"),
        name="inproj",
    )(x, mod, norm1, w_all, gq, gk, bd, cos_t, sin_t)


def _ctxkv_kernel(row, x_ref, mod_ref, n1_ref, w_ref, gk_ref, bd_ref, k_ref, v_ref):
    d = x_ref.shape[-1]
    kw = k_ref.shape[-1]
    hb = _modulated(x_ref[...], mod_ref, row, n1_ref, d).astype(BF16)
    k = _head_rms(_dot(hb, w_ref[:, 0:kw]), bd_ref[...]) * gk_ref[...]
    k_ref[...] = k.astype(BF16)
    v_ref[...] = _dot(hb, w_ref[:, kw:]).astype(BF16)


def _ctxkv(ctx, mod, ctx_row, norm1, w_kv, gk, bd):
    b, lc, d = ctx.shape
    kw = gk.shape[1]
    const = lambda i: (0, 0)
    tok = lambda i: (i, 0, 0)
    return pl.pallas_call(
        functools.partial(_ctxkv_kernel, ctx_row),
        grid=(b,),
        in_specs=[pl.BlockSpec((None, lc, d), tok),
                  pl.BlockSpec(mod.shape, const),
                  pl.BlockSpec((1, d), const),
                  pl.BlockSpec(w_kv.shape, const),
                  pl.BlockSpec((1, kw), const),
                  pl.BlockSpec(bd.shape, const)],
        out_specs=[pl.BlockSpec((None, lc, kw), tok), pl.BlockSpec((None, lc, kw), tok)],
        out_shape=[jax.ShapeDtypeStruct((b, lc, kw), BF16)] * 2,
        compiler_params=_cparams("arbitrary"),
        name="ctxkv",
    )(ctx, mod, norm1, w_kv, gk, bd)


def _attn_kernel(sink_ref, q_ref, k_ref, v_ref, kx_ref, vx_ref, ga_ref, o_ref, acc_ref):
    tq = WINDOW
    nb = q_ref.shape[0] // tq
    lc = kx_ref.shape[0]
    nk = 3 * tq + lc
    pairs = GROUP // 2
    rows = pairs * tq

    def block(i, carry):
        r = lax.broadcasted_iota(jnp.int32, (rows, tq), 0) % tq
        j = lax.broadcasted_iota(jnp.int32, (rows, tq), 1)
        lo = lax.broadcasted_iota(jnp.int32, (nk, LANES), 1) < HEAD_DIM
        zero = jnp.zeros((nk, LANES), BF16)
        top = lax.broadcasted_iota(jnp.int32, (rows, 1), 0) < tq
        at = lambda blk: pl.ds(pl.multiple_of(blk * tq, tq), tq)
        cur, prev, nxt = at(i), at(jnp.maximum(i - 1, 0)), at(jnp.minimum(i + 1, nb - 1))
        ok_prev = j >= r + jnp.where(i > 0, 0, tq)
        ok_next = j <= r - jnp.where(i < nb - 1, 0, tq)
        ks, vs, qs, sinks = [], [], [], []
        for h in range(N_KV_HEADS):
            hs = slice(h * LANES, (h + 1) * LANES)
            kcat = jnp.concatenate([k_ref[prev, hs], k_ref[cur, hs], k_ref[nxt, hs], kx_ref[:, hs]], axis=0)
            vcat = jnp.concatenate([v_ref[prev, hs], v_ref[cur, hs], v_ref[nxt, hs], vx_ref[:, hs]], axis=0)
            q2 = jnp.concatenate([q_ref[cur, (h * pairs + p) * LANES:(h * pairs + p + 1) * LANES]
                                  for p in range(pairs)], axis=0)
            for half in range(2):
                ks.append(jnp.where(lo, kcat, zero) if half == 0 else jnp.where(lo, zero, kcat))
                vs.append(jnp.where(lo, vcat, zero) if half == 0 else jnp.where(lo, zero, vcat))
                qs.append(q2)
                sinks.append(jnp.where(top, sink_ref[h * GROUP + half], sink_ref[h * GROUP + 2 + half]) * LOG2E)
        ss = [lax.dot_general(qq, kk, (((1,), (1,)), ((), ())), preferred_element_type=F32)
              for qq, kk in zip(qs, ks)]
        ss = [jnp.concatenate([jnp.where(ok_prev, s[:, 0:tq], NEG_INF), s[:, tq:2 * tq],
                               jnp.where(ok_next, s[:, 2 * tq:3 * tq], NEG_INF), s[:, 3 * tq:]], axis=1)
              for s in ss]
        ms = [jnp.maximum(jnp.max(s, axis=-1, keepdims=True), sk) for s, sk in zip(ss, sinks)]
        es = [jnp.exp2(s - m) for s, m in zip(ss, ms)]
        dens = [jnp.sum(e, axis=-1, keepdims=True) + jnp.exp2(sk - m) for e, sk, m in zip(es, sinks, ms)]
        outs = [_dot(e.astype(BF16), vv) / den for e, vv, den in zip(es, vs, dens)]
        for h in range(N_KV_HEADS):
            both = outs[2 * h] + outs[2 * h + 1]
            for p in range(pairs):
                acc_ref[:, (h * pairs + p) * LANES:(h * pairs + p + 1) * LANES] = both[p * tq:(p + 1) * tq]
        a = acc_ref[...]
        o_ref[cur, :] = (a * _rms(a) * ga_ref[...]).astype(BF16)
        return carry

    lax.fori_loop(0, nb, block, 0)


def _attention(sink, q, k, v, kx, vx, ga):
    b, l, aw = q.shape
    kw = k.shape[-1]
    lc = kx.shape[1]
    seq = lambda bi: (bi, 0, 0)
    return pl.pallas_call(
        _attn_kernel,
        grid=(b,),
        in_specs=[pl.BlockSpec(memory_space=pltpu.SMEM),
                  pl.BlockSpec((None, l, aw), seq),
                  pl.BlockSpec((None, l, kw), seq), pl.BlockSpec((None, l, kw), seq),
                  pl.BlockSpec((None, lc, kw), seq), pl.BlockSpec((None, lc, kw), seq),
                  pl.BlockSpec((1, aw), lambda bi: (0, 0))],
        out_specs=pl.BlockSpec((None, l, aw), seq),
        out_shape=jax.ShapeDtypeStruct((b, l, aw), BF16),
        scratch_shapes=[pltpu.VMEM((WINDOW, aw), F32)],
        compiler_params=_cparams("arbitrary"),
        name="attn",
    )(sink, q, k, v, kx, vx, ga)


def _alternating(rows, cols):
    return (1 - 2 * (lax.broadcasted_iota(jnp.int32, (rows, cols), 0) % 2)).astype(F32)


def _hconv_kernel(u0_ref, u1_ref, u2_ref, w0_ref, w1_ref, w2_ref, b0_ref, b1_ref, b2_ref,
                  x0e_ref, x0o_ref, ze_ref, zo_ref, qr_ref, qs_ref, scr):
    l, cb = u0_ref.shape
    half = l // 2
    row = lax.broadcasted_iota(jnp.int32, (l, cb), 0)

    def conv(u_ref, w_ref, b_ref):
        u = u_ref[...]
        before = jnp.where(row == 0, 0.0, pltpu.roll(u, 1, 0))
        after = jnp.where(row == l - 1, 0.0, pltpu.roll(u, l - 1, 0))
        return b_ref[...] + before * w_ref[0:1, :] + u * w_ref[1:2, :] + after * w_ref[2:3, :]

    def parity_halves(v):
        scr[...] = v
        return scr[pl.ds(0, half, stride=2), :], scr[pl.ds(1, half, stride=2), :]

    x0e_ref[...], x0o_ref[...] = parity_halves(conv(u0_ref, w0_ref, b0_ref))
    ze, zo = parity_halves(conv(u1_ref, w1_ref, b1_ref) * conv(u2_ref, w2_ref, b2_ref))
    ze_ref[...] = ze.astype(BF16)
    zo_ref[...] = zo.astype(BF16)
    sign = _alternating(half, cb)
    qr_ref[...] = jnp.sum(ze * sign, axis=0, keepdims=True)
    qs_ref[...] = jnp.sum(zo * sign, axis=0, keepdims=True)


def _hconv(u, conv_w, conv_b, cb=LANES):
    b, l, w3 = u.shape
    c = w3 // 3
    n = c // cb
    us = [pl.BlockSpec((None, l, cb), lambda bi, j, g=g: (bi, 0, g * n + j)) for g in range(3)]
    ws = [pl.BlockSpec((3, cb), lambda bi, j, g=g: (0, g * n + j)) for g in range(3)]
    bs = [pl.BlockSpec((1, cb), lambda bi, j, g=g: (0, g * n + j)) for g in range(3)]
    out = lambda bi, j: (bi, 0, j)
    seq = pl.BlockSpec((None, l // 2, cb), out)
    vec = pl.BlockSpec((None, 1, cb), out)
    return pl.pallas_call(
        _hconv_kernel,
        grid=(b, n),
        in_specs=us + ws + bs,
        out_specs=[seq, seq, seq, seq, vec, vec],
        out_shape=[jax.ShapeDtypeStruct((b, l // 2, c), F32)] * 2 + [jax.ShapeDtypeStruct((b, l // 2, c), BF16)] * 2
        + [jax.ShapeDtypeStruct((b, 1, c), F32)] * 2,
        scratch_shapes=[pltpu.VMEM((l, cb), F32)],
        compiler_params=_cparams("arbitrary", "arbitrary"),
        name="hconv",
    )(u, u, u, conv_w, conv_w, conv_w, conv_b, conv_b, conv_b)


def _filter_kernel(fe_ref, fo_ref, w1_ref, b1_ref, w2_ref, b2_ref, w3_ref, b3_ref, fr_ref, w4f_ref, w4b_ref,
                   dl_ref, kpe_ref, kpo_ref, kme_ref, kmo_ref, krq_ref, ksq_ref, he_scr, ho_scr):
    half, cf = kpe_ref.shape
    l = 2 * half

    @pl.when(pl.program_id(0) == 0)
    def _():
        fr = fr_ref[...]
        for f_ref, h_scr in ((fe_ref, he_scr), (fo_ref, ho_scr)):
            h = jnp.sin(fr * (_dot3(f_ref[...], w1_ref[...]) + b1_ref[...]))
            h = jnp.sin(fr * (_dot3(h, w2_ref[...]) + b2_ref[...]))
            h_scr[...] = jnp.sin(fr * (_dot3(h, w3_ref[...]) + b3_ref[...]))

    row = lax.broadcasted_iota(jnp.int32, (half, cf), 0)

    def taps(h_scr, parity):
        t = (2 * row + parity).astype(F32) / (l - 1)
        decay = jnp.exp(-t * dl_ref[...])
        h = h_scr[...]
        return _dot3(h, w4f_ref[...]) * decay, _dot3(h, w4b_ref[...]) * decay

    kfe, kbe = taps(he_scr, 0)
    kbe = jnp.where(row == 0, 0.0, kbe)
    kfo, kbo = taps(ho_scr, 1)
    nrm = lax.rsqrt(jnp.sum(kfe * kfe + kbe * kbe + kfo * kfo + kbo * kbo, axis=0, keepdims=True) + EPS)
    kpe = (kfe + kbe) * nrm
    kmo = (kfo - kbo) * nrm
    kpe_ref[...] = kpe.astype(BF16)
    kpo_ref[...] = ((kfo + kbo) * nrm).astype(BF16)
    kme_ref[...] = ((kfe - kbe) * nrm).astype(BF16)
    kmo_ref[...] = kmo.astype(BF16)
    sign = _alternating(half, cf)
    krq_ref[...] = jnp.sum(kpe * sign, axis=0, keepdims=True)
    ksq_ref[...] = jnp.sum(kmo * sign, axis=0, keepdims=True)


def _filter(fe, fo, w1, b1, w2, b2, w3, b3, fr, w4, deltas, cf=LANES):
    half, zw = fe.shape
    ffn = w2.shape[0]
    c = w4.shape[1] // 2
    n = c // cf
    const = lambda j: (0, 0)
    col = lambda j: (0, j)
    vec = pl.BlockSpec((1, ffn), const)
    mat = pl.BlockSpec((ffn, ffn), const)
    return pl.pallas_call(
        _filter_kernel,
        grid=(n,),
        in_specs=[pl.BlockSpec((half, zw), const), pl.BlockSpec((half, zw), const),
                  pl.BlockSpec((zw, ffn), const), vec, mat, vec, mat, vec, vec,
                  pl.BlockSpec((ffn, cf), col),
                  pl.BlockSpec((ffn, cf), lambda j: (0, n + j)),
                  pl.BlockSpec((1, cf), col)],
        out_specs=[pl.BlockSpec((half, cf), col)] * 4 + [pl.BlockSpec((1, cf), col)] * 2,
        out_shape=[jax.ShapeDtypeStruct((half, c), BF16)] * 4 + [jax.ShapeDtypeStruct((1, c), F32)] * 2,
        scratch_shapes=[pltpu.VMEM((half, ffn), F32)] * 2,
        compiler_params=_cparams("arbitrary"),
        name="filt",
    )(fe, fo, w1, b1, w2, b2, w3, b3, fr, w4, w4, deltas)


DFT_FINE = 64


def _dftgen_kernel(ca_ref, sa_ref, cb_ref, sb_ref, c_ref, s_ref):
    cb = cb_ref[...]
    sb = sb_ref[...]
    for a in range(ca_ref.shape[0]):
        ca = ca_ref[a:a + 1, :]
        sa = sa_ref[a:a + 1, :]
        rs = slice(a * DFT_FINE, (a + 1) * DFT_FINE)
        c_ref[rs, :] = (ca * cb - sa * sb).astype(BF16)
        s_ref[rs, :] = (sa * cb + ca * sb).astype(BF16)


def _dftgen(coarse, fine, l, tf, name):
    tabs = [jnp.asarray(fn((k % (2 * l)) * (math.pi / l)), F32) for k in (coarse, fine) for fn in (np.cos, np.sin)]
    rows, width = coarse.shape[0] * DFT_FINE, coarse.shape[1]
    na = tf // DFT_FINE
    tile = lambda i: (i, 0)
    const = lambda i: (0, 0)
    return pl.pallas_call(
        _dftgen_kernel,
        grid=(rows // tf,),
        in_specs=[pl.BlockSpec((na, width), tile), pl.BlockSpec((na, width), tile),
                  pl.BlockSpec((DFT_FINE, width), const), pl.BlockSpec((DFT_FINE, width), const)],
        out_specs=[pl.BlockSpec((tf, width), tile), pl.BlockSpec((tf, width), tile)],
        out_shape=[jax.ShapeDtypeStruct((rows, width), BF16)] * 2,
        compiler_params=_cparams("arbitrary"),
        name=name,
    )(*tabs)


def _dft_matrices(l, tf):
    half = l // 2
    lo = np.arange(half, dtype=np.int64)[None, :]
    a = np.arange(half // DFT_FINE, dtype=np.int64)[:, None] * DFT_FINE
    i = np.arange(DFT_FINE, dtype=np.int64)[:, None]
    ce, se = _dftgen(a * 2 * lo, i * 2 * lo, l, tf, "dft_even")
    co, so = _dftgen(a * (2 * lo + 1), i * (2 * lo + 1), l, tf, "dft_odd")
    cot, sot = _dftgen(2 * a * lo, (2 * i + 1) * lo, l, tf, "dft_odd_t")
    return ce, se, co, so, cot, sot


def _kspec_kernel(n_fft, ce_ref, se_ref, co_ref, so_ref, kpe_ref, kpo_ref, kme_ref, kmo_ref,
                  krl_ref, krm_ref, ksl_ref, ksm_ref):
    tf = ce_ref.shape[0]
    f = pl.program_id(0) * tf + lax.broadcasted_iota(jnp.int32, (tf, 1), 0)
    w = jnp.where(f == 0, 1.0 / n_fft, 2.0 / n_fft)
    ce = _dot(ce_ref[...], kpe_ref[...])
    co = _dot(co_ref[...], kpo_ref[...])
    se = _dot(se_ref[...], kme_ref[...])
    so = _dot(so_ref[...], kmo_ref[...])
    krl_ref[...] = (ce + co) * w
    krm_ref[...] = (ce - co) * w
    ksl_ref[...] = (so + se) * w
    ksm_ref[...] = (so - se) * w


def _kspec(mats, kpe, kpo, kme, kmo, tf):
    half, c = kpe.shape
    const = lambda i: (0, 0)
    tile = lambda i: (i, 0)
    return pl.pallas_call(
        functools.partial(_kspec_kernel, 4 * half),
        grid=(half // tf,),
        in_specs=[pl.BlockSpec((tf, half), tile)] * 4 + [pl.BlockSpec((half, c), const)] * 4,
        out_specs=[pl.BlockSpec((tf, c), tile)] * 4,
        out_shape=[jax.ShapeDtypeStruct((half, c), F32)] * 4,
        compiler_params=_cparams("arbitrary"),
        name="kspec",
    )(*mats, kpe, kpo, kme, kmo)


def _hfwd_kernel(ce_ref, se_ref, co_ref, so_ref, ze_ref, zo_ref, krl_ref, krm_ref, ksl_ref, ksm_ref,
                 p1_ref, p2_ref, p3_ref, p4_ref):
    ze = ze_ref[...]
    zo = zo_ref[...]
    ce = _dot(ce_ref[...], ze)
    co = _dot(co_ref[...], zo)
    se = _dot(se_ref[...], ze)
    so = _dot(so_ref[...], zo)

    def product(zr, zs, kr_ref, ks_ref):
        kr = kr_ref[...]
        ks = ks_ref[...]
        return zr * kr - zs * ks, zr * ks + zs * kr

    al, bl = product(ce + co, so + se, krl_ref, ksl_ref)
    am, bm = product(ce - co, so - se, krm_ref, ksm_ref)
    p1_ref[...] = (al + am).astype(BF16)
    p2_ref[...] = (bl - bm).astype(BF16)
    p3_ref[...] = (al - am).astype(BF16)
    p4_ref[...] = (bl + bm).astype(BF16)


def _hfwd(mats, ze, zo, spec, tf):
    b, half, c = ze.shape
    tile = lambda i, bi: (i, 0)
    seq = lambda i, bi: (bi, 0, 0)
    out = lambda i, bi: (bi, i, 0)
    return pl.pallas_call(
        _hfwd_kernel,
        grid=(half // tf, b),
        in_specs=[pl.BlockSpec((tf, half), tile)] * 4 + [pl.BlockSpec((None, half, c), seq)] * 2
        + [pl.BlockSpec((tf, c), tile)] * 4,
        out_specs=[pl.BlockSpec((None, tf, c), out)] * 4,
        out_shape=[jax.ShapeDtypeStruct((b, half, c), BF16)] * 4,
        compiler_params=_cparams("arbitrary", "arbitrary"),
        name="hfwd",
    )(*mats, ze, zo, *spec)


def _hinv_kernel(ce_ref, se_ref, cot_ref, sot_ref, p1_ref, p2_ref, p3_ref, p4_ref, x0e_ref, x0o_ref,
                 ze_ref, zo_ref, qr_ref, qs_ref, krq_ref, ksq_ref, bias_ref, g_ref, o_ref, scr):
    tt, half = ce_ref.shape
    n_fft = 4 * half
    ye = _dot(ce_ref[...], p1_ref[...]) + _dot(se_ref[...], p2_ref[...])
    yo = _dot(cot_ref[...], p3_ref[...]) + _dot(sot_ref[...], p4_ref[...])
    sign = (1 - 2 * ((pl.program_id(0) * tt + lax.broadcasted_iota(jnp.int32, (tt, 1), 0)) % 2)).astype(F32)
    qr, qs, krq, ksq = qr_ref[...], qs_ref[...], krq_ref[...], ksq_ref[...]
    ye = ye + sign * ((qr * krq - qs * ksq) * (2.0 / n_fft))
    yo = yo + sign * ((qr * ksq + qs * krq) * (2.0 / n_fft))

    def finish(y, x0_ref, z_ref):
        hy = x0_ref[...] * (y + z_ref[...].astype(F32) * bias_ref[...])
        return hy * _rms(hy) * g_ref[...]

    he = finish(ye, x0e_ref, ze_ref)
    ho = finish(yo, x0o_ref, zo_ref)
    for c in range(scr.shape[0]):
        cs = slice(c * LANES, (c + 1) * LANES)
        scr[c, pl.ds(0, tt, stride=2), :] = he[:, cs]
        scr[c, pl.ds(1, tt, stride=2), :] = ho[:, cs]
        o_ref[:, cs] = scr[c].astype(BF16)


def _hinv(ce, se, cot, sot, ps, x0e, x0o, ze, zo, qr, qs, krq, ksq, bias, gain, tt):
    b, half, c = ze.shape
    tile = lambda i, bi: (i, 0)
    full = lambda i, bi: (bi, 0, 0)
    tok = lambda i, bi: (bi, i, 0)
    const = lambda i, bi: (0, 0)
    vec = pl.BlockSpec((1, c), const)
    return pl.pallas_call(
        _hinv_kernel,
        grid=(half // tt, b),
        in_specs=[pl.BlockSpec((tt, half), tile)] * 4 + [pl.BlockSpec((None, half, c), full)] * 4
        + [pl.BlockSpec((None, tt, c), tok)] * 4 + [pl.BlockSpec((None, 1, c), full)] * 2 + [vec] * 4,
        out_specs=pl.BlockSpec((None, 2 * tt, c), tok),
        out_shape=jax.ShapeDtypeStruct((b, 2 * half, c), BF16),
        scratch_shapes=[pltpu.VMEM((c // LANES, 2 * tt, LANES), F32)],
        compiler_params=_cparams("arbitrary", "arbitrary"),
        name="hinv",
    )(ce, se, cot, sot, *ps, x0e, x0o, ze, zo, qr, qs, krq, ksq, bias, gain)


def _mixout_kernel(an_ref, yn_ref, x_ref, mod_ref, wo_ref, n2_ref, wr_ref, br_ref, tri_ref,
                   xn_ref, h2_ref, route_ref, route_t_ref, cnt_ref, carry):
    bi = pl.program_id(0)
    d = x_ref.shape[-1]
    half = an_ref.shape[-1]
    tl = x_ref.shape[0]

    @pl.when((bi == 0) & (pl.program_id(1) == 0))
    def _():
        carry[...] = jnp.zeros_like(carry)

    mix = _dot(an_ref[...], wo_ref[0:half, :]) + _dot(yn_ref[...], wo_ref[half:, :])
    g1 = mod_ref[pl.ds(bi, 1), 2 * d:3 * d]
    sh2 = mod_ref[pl.ds(bi, 1), 3 * d:4 * d]
    sc2 = mod_ref[pl.ds(bi, 1), 4 * d:5 * d]
    xn = x_ref[...] + g1 * mix
    xn_ref[...] = xn
    h2 = (xn * _rms(xn)) * n2_ref[...] * (1 + sc2) + sh2
    for j in range(d // LANES):
        _token_chunk(h2_ref, 0, tl, j)[...] = h2[:, j * LANES:(j + 1) * LANES]

    logits = _dot3(h2, wr_ref[...]) + br_ref[...]
    lane = lax.broadcasted_iota(jnp.int32, (tl, LANES), 1).astype(F32)
    vals, idxs, sels = [], [], []
    cur = logits
    for _ in range(TOP_K):
        m = jnp.max(cur, axis=-1, keepdims=True)
        idx = jnp.min(jnp.where(cur == m, lane, float(LANES)), axis=-1, keepdims=True)
        sel = lane == idx
        vals.append(m)
        idxs.append(idx)
        sels.append(sel)
        cur = jnp.where(sel, -jnp.inf, cur)
    es = [jnp.exp(v - vals[0]) for v in vals]
    den = es[0] + es[1] + es[2] + es[3]
    hot = sum(s.astype(F32) for s in sels)
    before = _dot(tri_ref[...], hot.astype(BF16)) + carry[...]
    carry[...] = carry[...] + jnp.sum(hot, axis=0, keepdims=True)
    cnt_ref[...] = carry[...]
    route = jnp.zeros((tl, LANES), F32)
    for k in range(TOP_K):
        pos = jnp.sum(jnp.where(sels[k], before, 0.0), axis=-1, keepdims=True)
        route = jnp.where(lane == k, idxs[k], route)
        route = jnp.where(lane == TOP_K + k, es[k] / den, route)
        route = jnp.where(lane == 2 * TOP_K + k, pos, route)
    route_ref[...] = route
    route_t_ref[...] = route.T[0:route_t_ref.shape[0], :]


def _mixout(an, yn, x, mod, w_out, norm2, wr, br, tri, tl):
    b, l, d = x.shape
    half = an.shape[-1]
    nt = l // tl
    tok = lambda bi, i: (bi, i, 0)
    const = lambda bi, i: (0, 0)
    flat = lambda bi, i: (bi * nt + i, 0)
    return pl.pallas_call(
        _mixout_kernel,
        grid=(b, nt),
        in_specs=[pl.BlockSpec((None, tl, half), tok), pl.BlockSpec((None, tl, half), tok),
                  pl.BlockSpec((None, tl, d), tok),
                  pl.BlockSpec(mod.shape, const), pl.BlockSpec(w_out.shape, const),
                  pl.BlockSpec((1, d), const), pl.BlockSpec(wr.shape, const),
                  pl.BlockSpec((1, LANES), const), pl.BlockSpec((tl, tl), const)],
        out_specs=[pl.BlockSpec((None, tl, d), tok),
                   pl.BlockSpec((tl * SUBLANES, LANES), flat),
                   pl.BlockSpec((tl, LANES), flat),
                   pl.BlockSpec((2 * SUBLANES, tl), lambda bi, i: (0, bi * nt + i)),
                   pl.BlockSpec((1, LANES), const)],
        out_shape=[jax.ShapeDtypeStruct((b, l, d), F32),
                   jax.ShapeDtypeStruct((b * l * SUBLANES, LANES), F32),
                   jax.ShapeDtypeStruct((b * l, LANES), F32),
                   jax.ShapeDtypeStruct((2 * SUBLANES, b * l), F32),
                   jax.ShapeDtypeStruct((1, LANES), F32)],
        scratch_shapes=[pltpu.VMEM((1, LANES), F32)],
        compiler_params=_cparams("arbitrary", "arbitrary"),
        name="mixout",
    )(an, yn, x, mod, w_out, norm2, wr, br, tri)


def _token_chunk(ref, first_token, n, j):
    return ref.at[pl.ds(first_token * SUBLANES + j, n, stride=SUBLANES), :]


def _token_copy(idx_ref, r, src_hbm, dst, sem):
    first_row = lambda tok: tok * SUBLANES if isinstance(tok, int) else pl.multiple_of(tok * SUBLANES, SUBLANES)
    return pltpu.make_async_copy(src_hbm.at[pl.ds(first_row(idx_ref[0, r]), SUBLANES)],
                                 dst.at[pl.ds(first_row(r), SUBLANES)], sem)


def _gather_tokens(idx_ref, n, src_hbm, dst, sem):
    def body(r, carry):
        _token_copy(idx_ref, 2 * r, src_hbm, dst, sem).start(priority=0)
        _token_copy(idx_ref, 2 * r + 1, src_hbm, dst, sem).start(priority=1)
        return carry
    lax.fori_loop(0, n // 2, body, 0, unroll=16)


def _wait_tokens(src_hbm, dst, sem):
    pltpu.make_async_copy(src_hbm.at[pl.ds(0, dst.shape[0])], dst, sem).wait()


EXPERT_STAGES = 4
def _dispatch_kernel(last_ref, nvb_ref, dest_ref, h2_ref, xs_hbm, sbuf, zbuf, sem, zsem):
    i = pl.program_id(0)
    n = pl.num_programs(0)
    tc = h2_ref.shape[0] // SUBLANES
    slot = i % 2

    def wait_slot(s):
        for _ in range(TOP_K):
            pltpu.make_async_copy(sbuf.at[s], xs_hbm.at[pl.ds(0, tc * SUBLANES)], sem.at[s]).wait()

    @pl.when(i == 0)
    def _():
        zbuf[...] = jnp.zeros_like(zbuf)
        n_blk = xs_hbm.shape[0] // zbuf.shape[0]
        nvb = nvb_ref[0]
        targets = [(last_ref[e] >= 0, last_ref[e]) for e in range(N_EXPERTS)]
        targets += [(nvb + j < n_blk, nvb + j) for j in range(N_EXPERTS)]

        def zero_copy(blk):
            first = pl.multiple_of(blk * zbuf.shape[0], zbuf.shape[0])
            return pltpu.make_async_copy(zbuf, xs_hbm.at[pl.ds(first, zbuf.shape[0])], zsem)

        for op in ("start", "wait"):
            for needed, blk in targets:
                @pl.when(needed)
                def _():
                    getattr(zero_copy(blk), op)()

    @pl.when(i >= 2)
    def _():
        wait_slot(slot)

    sbuf[slot] = h2_ref[...]

    def body(r, carry):
        first = pl.multiple_of(r * SUBLANES, SUBLANES)
        for k in range(TOP_K):
            dst = pl.multiple_of(dest_ref[0, k * tc + r] * SUBLANES, SUBLANES)
            pltpu.make_async_copy(sbuf.at[slot, pl.ds(first, SUBLANES)], xs_hbm.at[pl.ds(dst, SUBLANES)],
                                  sem.at[slot]).start(priority=k % 2)
        return carry
    lax.fori_loop(0, tc, body, 0, unroll=8)

    @pl.when(i == n - 1)
    def _():
        wait_slot(slot)
        wait_slot(1 - slot)


def _dispatch(last_block, nvb, dest_kmaj, h2, n_blk, rows, tc):
    n = dest_kmaj.shape[0]
    assert n >= 2
    grid_spec = pltpu.PrefetchScalarGridSpec(
        num_scalar_prefetch=2,
        grid=(n,),
        in_specs=[pl.BlockSpec((None, 1, TOP_K * tc), lambda i, lb, nv: (i, 0, 0), memory_space=pltpu.SMEM),
                  pl.BlockSpec((tc * SUBLANES, LANES), lambda i, lb, nv: (i, 0))],
        out_specs=pl.BlockSpec(memory_space=pl.ANY),
        scratch_shapes=[pltpu.VMEM((2, tc * SUBLANES, LANES), F32), pltpu.VMEM((rows * SUBLANES, LANES), F32),
                        pltpu.SemaphoreType.DMA((2,)), pltpu.SemaphoreType.DMA(())],
    )
    return pl.pallas_call(
        _dispatch_kernel,
        grid_spec=grid_spec,
        out_shape=jax.ShapeDtypeStruct((n_blk * rows * SUBLANES, LANES), F32),
        compiler_params=_cparams("arbitrary"),
        name="dispatch",
    )(last_block, nvb, dest_kmaj, h2)


def _experts_kernel(be_ref, nvb_ref, xs_ref, wgu_ref, bgu_ref, wdn_ref, bdn_ref, ys_ref, xs_bf, wgu_bf, wdn_bf):
    i = pl.program_id(0)
    nvb = nvb_ref[0]
    rows = xs_bf.shape[0]
    nchunk = wgu_ref.shape[0] // LANES
    ff = wdn_ref.shape[0]
    fc = ff // EXPERT_STAGES

    @pl.when((i == 0) | (be_ref[i] != be_ref[jnp.maximum(i - 1, 0)]))
    def _():
        wgu_bf[...] = wgu_ref[...].astype(BF16)
        wdn_bf[...] = wdn_ref[...].astype(BF16)

    @pl.when(i < nvb)
    def _():
        for j in range(nchunk):
            xs_bf[:, j * LANES:(j + 1) * LANES] = _token_chunk(xs_ref, 0, rows, j)[...].astype(BF16)
        xs = xs_bf[...]
        def gate_up(c):
            cg = slice(c * fc, (c + 1) * fc)
            cu = slice(ff + c * fc, ff + (c + 1) * fc)
            return _dot(xs, wgu_bf[:, cg]) + bgu_ref[:, cg], _dot(xs, wgu_bf[:, cu]) + bgu_ref[:, cu]

        acc = None
        nxt = gate_up(0)
        for c in range(EXPERT_STAGES):
            gate, up = nxt
            if c + 1 < EXPERT_STAGES:
                nxt = gate_up(c + 1)
            gate = jnp.minimum(gate, SWIGLU_LIMIT)
            up = jnp.clip(up, -SWIGLU_LIMIT, SWIGLU_LIMIT)
            act = ((up + 1) * (gate * jax.nn.sigmoid(SWIGLU_ALPHA * gate))).astype(BF16)
            part = _dot(act, wdn_bf[c * fc:(c + 1) * fc, :])
            acc = part if acc is None else acc + part
        y = acc + bdn_ref[...]
        for j in range(nchunk):
            _token_chunk(ys_ref, 0, rows, j)[...] = y[:, j * LANES:(j + 1) * LANES]

    @pl.when(i >= nvb)
    def _():
        ys_ref[...] = jnp.zeros_like(ys_ref)


def _experts(block_e, nvb, xs, wgu, bgu, wdn, bdn, rows):
    n_blk = block_e.shape[0]
    d, ff2 = wgu.shape[1:]
    ff = wdn.shape[1]
    expert = lambda i, be, nv: (be[i], 0, 0)
    grid_spec = pltpu.PrefetchScalarGridSpec(
        num_scalar_prefetch=2,
        grid=(n_blk,),
        in_specs=[pl.BlockSpec((rows * SUBLANES, LANES), lambda i, be, nv: (jnp.minimum(i, nv[0] - 1), 0)),
                  pl.BlockSpec((None, d, ff2), expert), pl.BlockSpec((None, 1, ff2), expert),
                  pl.BlockSpec((None, ff, d), expert), pl.BlockSpec((None, 1, d), expert)],
        out_specs=pl.BlockSpec((rows * SUBLANES, LANES), lambda i, be, nv: (i, 0)),
        scratch_shapes=[pltpu.VMEM((rows, d), BF16), pltpu.VMEM((d, ff2), BF16), pltpu.VMEM((ff, d), BF16)],
    )
    return pl.pallas_call(
        _experts_kernel,
        grid_spec=grid_spec,
        out_shape=jax.ShapeDtypeStruct((n_blk * rows * SUBLANES, LANES), F32),
        compiler_params=_cparams("arbitrary"),
        name="experts",
    )(block_e, nvb, xs, wgu, bgu, wdn, bdn)


def _combine_kernel(dest_ref, destn_ref, ys_hbm, xn_ref, route_ref, g2_ref, o_ref, ybuf, sem):
    i = pl.program_id(0)
    n = pl.num_programs(0)
    tc = xn_ref.shape[0]
    slot = i % 2

    @pl.when(i == 0)
    def _():
        _gather_tokens(dest_ref, TOP_K * tc, ys_hbm, ybuf.at[0], sem.at[0])

    @pl.when(i + 1 < n)
    def _():
        _gather_tokens(destn_ref, TOP_K * tc, ys_hbm, ybuf.at[1 - slot], sem.at[1 - slot])

    _wait_tokens(ys_hbm, ybuf.at[slot], sem.at[slot])
    route = route_ref[...]
    gates = [jnp.broadcast_to(route[:, TOP_K + k:TOP_K + k + 1], (tc, LANES)) for k in range(TOP_K)]
    for j in range(xn_ref.shape[1] // LANES):
        cs = slice(j * LANES, (j + 1) * LANES)
        acc = gates[0] * _token_chunk(ybuf.at[slot], 0, tc, j)[...]
        for k in range(1, TOP_K):
            acc = acc + gates[k] * _token_chunk(ybuf.at[slot], k * tc, tc, j)[...]
        o_ref[:, cs] = xn_ref[:, cs] + g2_ref[:, cs] * acc


def _combine(dest_kmaj, ys, xn, route, g2, tc):
    t, d = xn.shape
    n = t // tc
    per_batch = t // g2.shape[0] // tc
    return pl.pallas_call(
        _combine_kernel,
        grid=(n,),
        in_specs=[pl.BlockSpec((None, 1, TOP_K * tc), lambda i: (i, 0, 0), memory_space=pltpu.SMEM),
                  pl.BlockSpec((None, 1, TOP_K * tc), lambda i: (jnp.minimum(i + 1, n - 1), 0, 0),
                               memory_space=pltpu.SMEM),
                  pl.BlockSpec(memory_space=pl.ANY),
                  pl.BlockSpec((tc, d), lambda i: (i, 0)),
                  pl.BlockSpec((tc, LANES), lambda i: (i, 0)),
                  pl.BlockSpec((None, 1, d), lambda i: (i // per_batch, 0, 0))],
        out_specs=pl.BlockSpec((tc, d), lambda i: (i, 0)),
        out_shape=jax.ShapeDtypeStruct((t, d), F32),
        scratch_shapes=[pltpu.VMEM((2, TOP_K * tc * SUBLANES, LANES), F32), pltpu.SemaphoreType.DMA((2,))],
        compiler_params=_cparams("arbitrary"),
        name="combine",
    )(dest_kmaj, dest_kmaj, ys, xn, route, g2)


def _rope_tables(l):
    n_freq = HEAD_DIM // 4
    inv_freq = ROPE_BASE ** (-np.arange(n_freq, dtype=np.float64) / n_freq)
    tpos = np.arange(l)
    lane = np.arange(LANES) % HEAD_DIM
    pos = np.where(lane[None, :] < HEAD_DIM // 2, (tpos // GRID_W)[:, None], (tpos % GRID_W)[:, None])
    ang = pos * inv_freq[lane % n_freq][None, :]
    sign = np.where(lane % (2 * n_freq) < n_freq, -1.0, 1.0)[None, :]
    return jnp.asarray(np.cos(ang), F32), jnp.asarray(np.sin(ang) * sign, F32)


def _filter_features(l, width):
    t = np.linspace(0.0, 1.0, l)[:, None]
    w = 2.0 * math.pi * np.arange(l)[:, None] / l
    bands = np.linspace(1e-4, POS_BANDS - 1, POS_BANDS)[None, :]
    z = np.concatenate([t, np.cos(bands * w), -np.sin(bands * w)], axis=-1)
    return np.pad(z, ((0, 0), (0, width - z.shape[1]))).astype(np.float32)


def kernel(x, c, ctx, c_ctx, w_mod, b_mod, norm1, norm2, w_in, q_norm, k_norm, sink, conv_w, conv_b,
           filt_w1, filt_b1, filt_w2, filt_b2, filt_w3, filt_b3, filt_w4, filt_freq, filt_bias,
           attn_out_norm, hyena_out_norm, w_out, w_router, b_router, w_gu, b_gu, w_down, b_down):
    assert w_mod.shape[0] == 1, "single-layer configuration"
    b, l, d = x.shape
    assert d == SUBLANES * LANES, "token-tiled rows assume one (8, 128) tile per token"
    t = b * l
    aw = N_Q_HEADS * HEAD_DIM
    kvw = N_KV_HEADS * HEAD_DIM
    hw = conv_w.shape[-1] // 3
    tl = min(512, l)

    ctx_row = b
    pad_rows = -(b + 1) % SUBLANES
    c_all = jnp.concatenate([c, c_ctx[None], jnp.zeros((pad_rows, d), F32)], axis=0)
    mod = _adaln(c_all, w_mod[0], b_mod[0])

    w = w_in[0]
    wq, wk, wv, wu = w[:, :aw], w[:, aw:aw + kvw], w[:, aw + kvw:aw + 2 * kvw], w[:, aw + 2 * kvw:]
    dup = lambda m: jnp.concatenate([m[:, h * HEAD_DIM:(h + 1) * HEAD_DIM]
                                     for h in range(N_KV_HEADS) for _ in range(2)], axis=1)
    w_all = jnp.concatenate([wq, dup(wk), dup(wv), wu], axis=1).astype(BF16)
    w_kv = jnp.concatenate([dup(wk), dup(wv)], axis=1).astype(BF16)
    gq = jnp.tile(q_norm[0], N_Q_HEADS)[None]
    gk = jnp.tile(k_norm[0], 2 * N_KV_HEADS)[None]
    bd = jnp.asarray(np.kron(np.eye(N_Q_HEADS), np.full((HEAD_DIM, HEAD_DIM), 1.0 / HEAD_DIM)), BF16)
    cos_t, sin_t = _rope_tables(l)

    q, k, v, u = _inproj(x, mod, norm1, w_all, gq, gk, bd, cos_t, sin_t, tl)
    kx, vx = _ctxkv(ctx, mod, ctx_row, norm1, w_kv, gk, bd[:2 * kvw, :2 * kvw])
    an = _attention(sink[0], q, k, v, kx, vx, attn_out_norm)

    x0e, x0o, ze, zo, qr, qs = _hconv(u, conv_w[0], conv_b)
    ffn = filt_w2.shape[-1]
    zf = _filter_features(l, ffn)
    w1 = jnp.pad(filt_w1[0], ((0, ffn - POS_EMB_DIM), (0, 0)))
    deltas = jnp.asarray(np.linspace(MIN_DECAY, MAX_DECAY, hw)[None, :], F32)
    kpe, kpo, kme, kmo, krq, ksq = _filter(jnp.asarray(zf[0::2]), jnp.asarray(zf[1::2]), w1, filt_b1, filt_w2[0],
                                           filt_b2, filt_w3[0], filt_b3, filt_freq, filt_w4[0], deltas)
    tf = min(512, l // 2)
    ce, se, co, so, cot, sot = _dft_matrices(l, tf)
    spec = _kspec((ce, se, co, so), kpe, kpo, kme, kmo, tf)
    ps = _hfwd((ce, se, co, so), ze, zo, spec, tf)
    yn = _hinv(ce, se, cot, sot, ps, x0e, x0o, ze, zo, qr, qs, krq, ksq, filt_bias, hyena_out_norm,
               min(256, l // 2))

    wr = jnp.pad(w_router[0], ((0, 0), (0, LANES - N_EXPERTS)))
    br = jnp.concatenate([b_router[0], jnp.full((LANES - N_EXPERTS,), NEG_INF, F32)])[None]
    tri = jnp.asarray(np.tril(np.ones((tl, tl)), -1), BF16)
    xn, h2, route, route_t, cnt = _mixout(an, yn, x, mod, w_out[0].astype(BF16), norm2, wr, br, tri, tl)

    rows = EXPERT_ROWS
    a_tot = t * TOP_K
    n_blk = -(-a_tot // rows) + N_EXPERTS
    idx = route_t[0:TOP_K].astype(jnp.int32)
    pos = route_t[2 * TOP_K:3 * TOP_K].astype(jnp.int32)
    counts = cnt[0, :N_EXPERTS].astype(jnp.int32)
    pcounts = (counts + rows - 1) // rows * rows
    pends = jnp.cumsum(pcounts)
    pstarts = pends - pcounts
    experts = jnp.arange(N_EXPERTS, dtype=jnp.int32)[:, None, None]
    dest = jnp.sum(jnp.where(idx[None] == experts, pstarts[:, None, None], 0), axis=0) + pos
    block_start = jnp.arange(n_blk, dtype=jnp.int32) * rows
    block_e = jnp.minimum(jnp.sum(pends[None, :] <= block_start[:, None], axis=1), N_EXPERTS - 1).astype(jnp.int32)
    last_block = jnp.where(pcounts > 0, pends // rows - 1, -1).astype(jnp.int32)
    nvb = (pends[-1:] // rows).astype(jnp.int32)
    tc = min(COMBINE_ROWS, l)
    dest_kmaj = dest.reshape(TOP_K, t // tc, tc).transpose(1, 0, 2).reshape(t // tc, 1, TOP_K * tc)

    xs = _dispatch(last_block, nvb, dest_kmaj, h2, n_blk, rows, tc)
    ys = _experts(block_e, nvb, xs, w_gu[0], b_gu[0][:, None, :], w_down[0], b_down[0][:, None, :], rows)

    g2 = mod[:b, None, 5 * d:6 * d]
    out = _combine(dest_kmaj, ys, xn.reshape(t, d), route, g2, tc)
    return out.reshape(b, l, d)
```

```python
import functools
import math

import numpy as np
import jax
import jax.numpy as jnp
from jax import lax
from jax.experimental import pallas as pl
from jax.experimental.pallas import tpu as pltpu

F32 = jnp.float32
BF16 = jnp.bfloat16

LANES = 128
SUBLANES = 8
VMEM_LIMIT = 56 * 1024 * 1024

HEAD_DIM = 64
N_Q_HEADS = 8
N_KV_HEADS = 2
GROUP = N_Q_HEADS // N_KV_HEADS
GRID_W = 64
WINDOW = 128
ROPE_BASE = 10000.0
ATTN_SCALE = HEAD_DIM ** -0.5
LOG2E = math.log2(math.e)
POS_EMB_DIM = 33
POS_BANDS = (POS_EMB_DIM - 1) // 2
DECAY_TARGET = 1e-2
MAX_DECAY = -math.log(DECAY_TARGET) / 0.3
MIN_DECAY = -math.log(DECAY_TARGET) / 1.5
N_EXPERTS = 32
TOP_K = 4
SWIGLU_LIMIT = 7.0
SWIGLU_ALPHA = 1.702
EPS = 1e-6
NEG_INF = -1e30

EXPERT_ROWS = 512
COMBINE_ROWS = 256


def _cparams(*sem):
    return pltpu.CompilerParams(dimension_semantics=sem, vmem_limit_bytes=VMEM_LIMIT)


def _split(a):
    hi = a.astype(BF16)
    lo = (a - hi.astype(F32)).astype(BF16)
    return hi, lo


def _dot(a, b):
    return jnp.dot(a, b, preferred_element_type=F32)


def _dot3(a, b):
    ah, al = _split(a)
    bh, bl = _split(b)
    return _dot(ah, bh) + _dot(al, bh) + _dot(ah, bl)


def _rms(x):
    return lax.rsqrt(jnp.mean(x * x, axis=-1, keepdims=True) + EPS)


def _adaln_kernel(c_ref, w_ref, b_ref, o_ref):
    c = c_ref[...]
    o_ref[...] = _dot3(c * jax.nn.sigmoid(c), w_ref[...]) + b_ref[...]


def _adaln(c_all, w_mod, b_mod):
    rows, d = c_all.shape
    n = w_mod.shape[1]
    tn = 1024
    return pl.pallas_call(
        _adaln_kernel,
        grid=(n // tn,),
        in_specs=[pl.BlockSpec((rows, d), lambda j: (0, 0)),
                  pl.BlockSpec((d, tn), lambda j: (0, j)),
                  pl.BlockSpec((1, tn), lambda j: (0, j))],
        out_specs=pl.BlockSpec((rows, tn), lambda j: (0, j)),
        out_shape=jax.ShapeDtypeStruct((rows, n), F32),
        compiler_params=_cparams("arbitrary"),
        name="adaln",
    )(c_all, w_mod, b_mod[None])


def _head_rms(x, bd):
    hi, lo = _split(x * x)
    return x * lax.rsqrt(_dot(hi, bd) + _dot(lo, bd) + EPS)


def _rope128(x, cos, sin):
    lane = lax.broadcasted_iota(jnp.int32, x.shape, 1)
    partner = jnp.where(lane % 32 < 16, pltpu.roll(x, LANES - 16, 1), pltpu.roll(x, 16, 1))
    return x * cos + partner * sin


def _modulated(x, mod_ref, row, norm_ref, d):
    sh = mod_ref[pl.ds(row, 1), 0:d]
    sc = mod_ref[pl.ds(row, 1), d:2 * d]
    return (x * _rms(x)) * norm_ref[...] * (1 + sc) + sh


INPROJ_SUBTILES = 2


def _inproj_kernel(x_ref, mod_ref, n1_ref, w_ref, gq_ref, gk_ref, bd_ref, cos_ref, sin_ref,
                   q_ref, k_ref, v_ref, u_ref):
    d = x_ref.shape[-1]
    aw = q_ref.shape[-1]
    kw = k_ref.shape[-1]
    bd = bd_ref[...]
    sub = x_ref.shape[0] // INPROJ_SUBTILES
    for s in range(INPROJ_SUBTILES):
        rs = slice(s * sub, (s + 1) * sub)
        hb = _modulated(x_ref[rs, :], mod_ref, pl.program_id(1), n1_ref, d).astype(BF16)
        cos = cos_ref[rs, :]
        sin = sin_ref[rs, :]
        q = _head_rms(_dot(hb, w_ref[:, 0:aw]), bd) * gq_ref[...]
        for c in range(aw // LANES):
            sl = slice(c * LANES, (c + 1) * LANES)
            q_ref[rs, sl] = (_rope128(q[:, sl], cos, sin) * (ATTN_SCALE * LOG2E)).astype(BF16)
        k = _head_rms(_dot(hb, w_ref[:, aw:aw + kw]), bd[0:kw, 0:kw]) * gk_ref[...]
        for c in range(kw // LANES):
            sl = slice(c * LANES, (c + 1) * LANES)
            k_ref[rs, sl] = _rope128(k[:, sl], cos, sin).astype(BF16)
        v_ref[rs, :] = _dot(hb, w_ref[:, aw + kw:aw + 2 * kw]).astype(BF16)
        u_ref[rs, :] = _dot(hb, w_ref[:, aw + 2 * kw:])


def _inproj(x, mod, norm1, w_all, gq, gk, bd, cos_t, sin_t, tl):
    b, l, d = x.shape
    aw, kw = gq.shape[1], gk.shape[1]
    uw = w_all.shape[1] - aw - 2 * kw
    const = lambda i, j: (0, 0)
    tok = lambda i, j: (j, i, 0)
    return pl.pallas_call(
        _inproj_kernel,
        grid=(l // tl, b),
        in_specs=[pl.BlockSpec((None, tl, d), tok),
                  pl.BlockSpec(mod.shape, const),
                  pl.BlockSpec((1, d), const),
                  pl.BlockSpec(w_all.shape, const),
                  pl.BlockSpec((1, aw), const),
                  pl.BlockSpec((1, kw), const),
                  pl.BlockSpec(bd.shape, const),
                  pl.BlockSpec((tl, LANES), lambda i, j: (i, 0)),
                  pl.BlockSpec((tl, LANES), lambda i, j: (i, 0))],
        out_specs=[pl.BlockSpec((None, tl, aw), tok),
                   pl.BlockSpec((None, tl, kw), tok),
                   pl.BlockSpec((None, tl, kw), tok),
                   pl.BlockSpec((None, tl, uw), tok)],
        out_shape=[jax.ShapeDtypeStruct((b, l, aw), BF16),
                   jax.ShapeDtypeStruct((b, l, kw), BF16),
                   jax.ShapeDtypeStruct((b, l, kw), BF16),
                   jax.ShapeDtypeStruct((b, l, uw), F32)],
        compiler_params=_cparams("arbitrary", "arbitrary"),
        name="inproj",
    )(x, mod, norm1, w_all, gq, gk, bd, cos_t, sin_t)


def _ctxkv_kernel(row, x_ref, mod_ref, n1_ref, w_ref, gk_ref, bd_ref, k_ref, v_ref):
    d = x_ref.shape[-1]
    kw = k_ref.shape[-1]
    hb = _modulated(x_ref[...], mod_ref, row, n1_ref, d).astype(BF16)
    k = _head_rms(_dot(hb, w_ref[:, 0:kw]), bd_ref[...]) * gk_ref[...]
    k_ref[...] = k.astype(BF16)
    v_ref[...] = _dot(hb, w_ref[:, kw:]).astype(BF16)


def _ctxkv(ctx, mod, ctx_row, norm1, w_kv, gk, bd):
    b, lc, d = ctx.shape
    kw = gk.shape[1]
    const = lambda i: (0, 0)
    tok = lambda i: (i, 0, 0)
    return pl.pallas_call(
        functools.partial(_ctxkv_kernel, ctx_row),
        grid=(b,),
        in_specs=[pl.BlockSpec((None, lc, d), tok),
                  pl.BlockSpec(mod.shape, const),
                  pl.BlockSpec((1, d), const),
                  pl.BlockSpec(w_kv.shape, const),
                  pl.BlockSpec((1, kw), const),
                  pl.BlockSpec(bd.shape, const)],
        out_specs=[pl.BlockSpec((None, lc, kw), tok), pl.BlockSpec((None, lc, kw), tok)],
        out_shape=[jax.ShapeDtypeStruct((b, lc, kw), BF16)] * 2,
        compiler_params=_cparams("arbitrary"),
        name="ctxkv",
    )(ctx, mod, norm1, w_kv, gk, bd)


def _attn_kernel(sink_ref, q_ref, k_ref, v_ref, kx_ref, vx_ref, ga_ref, o_ref, acc_ref):
    tq = WINDOW
    nb = q_ref.shape[0] // tq
    lc = kx_ref.shape[0]
    nk = 3 * tq + lc
    pairs = GROUP // 2
    rows = pairs * tq

    def block(i, carry):
        r = lax.broadcasted_iota(jnp.int32, (rows, tq), 0) % tq
        j = lax.broadcasted_iota(jnp.int32, (rows, tq), 1)
        lo = lax.broadcasted_iota(jnp.int32, (nk, LANES), 1) < HEAD_DIM
        zero = jnp.zeros((nk, LANES), BF16)
        top = lax.broadcasted_iota(jnp.int32, (rows, 1), 0) < tq
        at = lambda blk: pl.ds(pl.multiple_of(blk * tq, tq), tq)
        cur, prev, nxt = at(i), at(jnp.maximum(i - 1, 0)), at(jnp.minimum(i + 1, nb - 1))
        ok_prev = j >= r + jnp.where(i > 0, 0, tq)
        ok_next = j <= r - jnp.where(i < nb - 1, 0, tq)
        ks, vs, qs, sinks = [], [], [], []
        for h in range(N_KV_HEADS):
            hs = slice(h * LANES, (h + 1) * LANES)
            kcat = jnp.concatenate([k_ref[prev, hs], k_ref[cur, hs], k_ref[nxt, hs], kx_ref[:, hs]], axis=0)
            vcat = jnp.concatenate([v_ref[prev, hs], v_ref[cur, hs], v_ref[nxt, hs], vx_ref[:, hs]], axis=0)
            q2 = jnp.concatenate([q_ref[cur, (h * pairs + p) * LANES:(h * pairs + p + 1) * LANES]
                                  for p in range(pairs)], axis=0)
            for half in range(2):
                ks.append(jnp.where(lo, kcat, zero) if half == 0 else jnp.where(lo, zero, kcat))
                vs.append(jnp.where(lo, vcat, zero) if half == 0 else jnp.where(lo, zero, vcat))
                qs.append(q2)
                sinks.append(jnp.where(top, sink_ref[h * GROUP + half], sink_ref[h * GROUP + 2 + half]) * LOG2E)
        ss = [lax.dot_general(qq, kk, (((1,), (1,)), ((), ())), preferred_element_type=F32)
              for qq, kk in zip(qs, ks)]
        ss = [jnp.concatenate([jnp.where(ok_prev, s[:, 0:tq], NEG_INF), s[:, tq:2 * tq],
                               jnp.where(ok_next, s[:, 2 * tq:3 * tq], NEG_INF), s[:, 3 * tq:]], axis=1)
              for s in ss]
        ms = [jnp.maximum(jnp.max(s, axis=-1, keepdims=True), sk) for s, sk in zip(ss, sinks)]
        es = [jnp.exp2(s - m) for s, m in zip(ss, ms)]
        dens = [jnp.sum(e, axis=-1, keepdims=True) + jnp.exp2(sk - m) for e, sk, m in zip(es, sinks, ms)]
        outs = [_dot(e.astype(BF16), vv) / den for e, vv, den in zip(es, vs, dens)]
        for h in range(N_KV_HEADS):
            both = outs[2 * h] + outs[2 * h + 1]
            for p in range(pairs):
                acc_ref[:, (h * pairs + p) * LANES:(h * pairs + p + 1) * LANES] = both[p * tq:(p + 1) * tq]
        a = acc_ref[...]
        o_ref[cur, :] = (a * _rms(a) * ga_ref[...]).astype(BF16)
        return carry

    lax.fori_loop(0, nb, block, 0, unroll=2)


def _attention(sink, q, k, v, kx, vx, ga):
    b, l, aw = q.shape
    kw = k.shape[-1]
    lc = kx.shape[1]
    seq = lambda bi: (bi, 0, 0)
    return pl.pallas_call(
        _attn_kernel,
        grid=(b,),
        in_specs=[pl.BlockSpec(memory_space=pltpu.SMEM),
                  pl.BlockSpec((None, l, aw), seq),
                  pl.BlockSpec((None, l, kw), seq), pl.BlockSpec((None, l, kw), seq),
                  pl.BlockSpec((None, lc, kw), seq), pl.BlockSpec((None, lc, kw), seq),
                  pl.BlockSpec((1, aw), lambda bi: (0, 0))],
        out_specs=pl.BlockSpec((None, l, aw), seq),
        out_shape=jax.ShapeDtypeStruct((b, l, aw), BF16),
        scratch_shapes=[pltpu.VMEM((WINDOW, aw), F32)],
        compiler_params=_cparams("arbitrary"),
        name="attn",
    )(sink, q, k, v, kx, vx, ga)


def _alternating(rows, cols):
    return (1 - 2 * (lax.broadcasted_iota(jnp.int32, (rows, cols), 0) % 2)).astype(F32)


def _hconv_kernel(u0_ref, u1_ref, u2_ref, w0_ref, w1_ref, w2_ref, b0_ref, b1_ref, b2_ref,
                  x0e_ref, x0o_ref, ze_ref, zo_ref, qr_ref, qs_ref, scr):
    l, cb = u0_ref.shape
    half = l // 2
    row = lax.broadcasted_iota(jnp.int32, (l, cb), 0)

    def conv(u_ref, w_ref, b_ref):
        u = u_ref[...]
        before = jnp.where(row == 0, 0.0, pltpu.roll(u, 1, 0))
        after = jnp.where(row == l - 1, 0.0, pltpu.roll(u, l - 1, 0))
        return b_ref[...] + before * w_ref[0:1, :] + u * w_ref[1:2, :] + after * w_ref[2:3, :]

    def parity_halves(v):
        scr[...] = v
        return scr[pl.ds(0, half, stride=2), :], scr[pl.ds(1, half, stride=2), :]

    x0e_ref[...], x0o_ref[...] = parity_halves(conv(u0_ref, w0_ref, b0_ref))
    ze, zo = parity_halves(conv(u1_ref, w1_ref, b1_ref) * conv(u2_ref, w2_ref, b2_ref))
    ze_ref[...] = ze.astype(BF16)
    zo_ref[...] = zo.astype(BF16)
    sign = _alternating(half, cb)
    qr_ref[...] = jnp.sum(ze * sign, axis=0, keepdims=True)
    qs_ref[...] = jnp.sum(zo * sign, axis=0, keepdims=True)


def _hconv(u, conv_w, conv_b, cb=LANES):
    b, l, w3 = u.shape
    c = w3 // 3
    n = c // cb
    us = [pl.BlockSpec((None, l, cb), lambda bi, j, g=g: (bi, 0, g * n + j)) for g in range(3)]
    ws = [pl.BlockSpec((3, cb), lambda bi, j, g=g: (0, g * n + j)) for g in range(3)]
    bs = [pl.BlockSpec((1, cb), lambda bi, j, g=g: (0, g * n + j)) for g in range(3)]
    out = lambda bi, j: (bi, 0, j)
    seq = pl.BlockSpec((None, l // 2, cb), out)
    vec = pl.BlockSpec((None, 1, cb), out)
    return pl.pallas_call(
        _hconv_kernel,
        grid=(b, n),
        in_specs=us + ws + bs,
        out_specs=[seq, seq, seq, seq, vec, vec],
        out_shape=[jax.ShapeDtypeStruct((b, l // 2, c), F32)] * 2 + [jax.ShapeDtypeStruct((b, l // 2, c), BF16)] * 2
        + [jax.ShapeDtypeStruct((b, 1, c), F32)] * 2,
        scratch_shapes=[pltpu.VMEM((l, cb), F32)],
        compiler_params=_cparams("arbitrary", "arbitrary"),
        name="hconv",
    )(u, u, u, conv_w, conv_w, conv_w, conv_b, conv_b, conv_b)


def _filter_kernel(fe_ref, fo_ref, w1_ref, b1_ref, w2_ref, b2_ref, w3_ref, b3_ref, fr_ref, w4f_ref, w4b_ref,
                   dl_ref, kpe_ref, kpo_ref, kme_ref, kmo_ref, krq_ref, ksq_ref, he_scr, ho_scr):
    half, cf = kpe_ref.shape
    l = 2 * half

    @pl.when(pl.program_id(0) == 0)
    def _():
        fr = fr_ref[...]
        for f_ref, h_scr in ((fe_ref, he_scr), (fo_ref, ho_scr)):
            h = jnp.sin(fr * (_dot3(f_ref[...], w1_ref[...]) + b1_ref[...]))
            h = jnp.sin(fr * (_dot3(h, w2_ref[...]) + b2_ref[...]))
            h_scr[...] = jnp.sin(fr * (_dot3(h, w3_ref[...]) + b3_ref[...]))

    row = lax.broadcasted_iota(jnp.int32, (half, cf), 0)

    def taps(h_scr, parity):
        t = (2 * row + parity).astype(F32) / (l - 1)
        decay = jnp.exp(-t * dl_ref[...])
        h = h_scr[...]
        return _dot3(h, w4f_ref[...]) * decay, _dot3(h, w4b_ref[...]) * decay

    kfe, kbe = taps(he_scr, 0)
    kbe = jnp.where(row == 0, 0.0, kbe)
    kfo, kbo = taps(ho_scr, 1)
    nrm = lax.rsqrt(jnp.sum(kfe * kfe + kbe * kbe + kfo * kfo + kbo * kbo, axis=0, keepdims=True) + EPS)
    kpe = (kfe + kbe) * nrm
    kmo = (kfo - kbo) * nrm
    kpe_ref[...] = kpe.astype(BF16)
    kpo_ref[...] = ((kfo + kbo) * nrm).astype(BF16)
    kme_ref[...] = ((kfe - kbe) * nrm).astype(BF16)
    kmo_ref[...] = kmo.astype(BF16)
    sign = _alternating(half, cf)
    krq_ref[...] = jnp.sum(kpe * sign, axis=0, keepdims=True)
    ksq_ref[...] = jnp.sum(kmo * sign, axis=0, keepdims=True)


def _filter(fe, fo, w1, b1, w2, b2, w3, b3, fr, w4, deltas, cf=LANES):
    half, zw = fe.shape
    ffn = w2.shape[0]
    c = w4.shape[1] // 2
    n = c // cf
    const = lambda j: (0, 0)
    col = lambda j: (0, j)
    vec = pl.BlockSpec((1, ffn), const)
    mat = pl.BlockSpec((ffn, ffn), const)
    return pl.pallas_call(
        _filter_kernel,
        grid=(n,),
        in_specs=[pl.BlockSpec((half, zw), const), pl.BlockSpec((half, zw), const),
                  pl.BlockSpec((zw, ffn), const), vec, mat, vec, mat, vec, vec,
                  pl.BlockSpec((ffn, cf), col),
                  pl.BlockSpec((ffn, cf), lambda j: (0, n + j)),
                  pl.BlockSpec((1, cf), col)],
        out_specs=[pl.BlockSpec((half, cf), col)] * 4 + [pl.BlockSpec((1, cf), col)] * 2,
        out_shape=[jax.ShapeDtypeStruct((half, c), BF16)] * 4 + [jax.ShapeDtypeStruct((1, c), F32)] * 2,
        scratch_shapes=[pltpu.VMEM((half, ffn), F32)] * 2,
        compiler_params=_cparams("arbitrary"),
        name="filt",
    )(fe, fo, w1, b1, w2, b2, w3, b3, fr, w4, w4, deltas)


DFT_FINE = 64


def _dftgen_kernel(ca_ref, sa_ref, cb_ref, sb_ref, c_ref, s_ref):
    cb = cb_ref[...]
    sb = sb_ref[...]
    for a in range(ca_ref.shape[0]):
        ca = ca_ref[a:a + 1, :]
        sa = sa_ref[a:a + 1, :]
        rs = slice(a * DFT_FINE, (a + 1) * DFT_FINE)
        c_ref[rs, :] = (ca * cb - sa * sb).astype(BF16)
        s_ref[rs, :] = (sa * cb + ca * sb).astype(BF16)


def _dftgen(coarse, fine, l, tf, name):
    tabs = [jnp.asarray(fn((k % (2 * l)) * (math.pi / l)), F32) for k in (coarse, fine) for fn in (np.cos, np.sin)]
    rows, width = coarse.shape[0] * DFT_FINE, coarse.shape[1]
    na = tf // DFT_FINE
    tile = lambda i: (i, 0)
    const = lambda i: (0, 0)
    return pl.pallas_call(
        _dftgen_kernel,
        grid=(rows // tf,),
        in_specs=[pl.BlockSpec((na, width), tile), pl.BlockSpec((na, width), tile),
                  pl.BlockSpec((DFT_FINE, width), const), pl.BlockSpec((DFT_FINE, width), const)],
        out_specs=[pl.BlockSpec((tf, width), tile), pl.BlockSpec((tf, width), tile)],
        out_shape=[jax.ShapeDtypeStruct((rows, width), BF16)] * 2,
        compiler_params=_cparams("arbitrary"),
        name=name,
    )(*tabs)


def _dft_matrices(l, tf):
    half = l // 2
    lo = np.arange(half, dtype=np.int64)[None, :]
    a = np.arange(half // DFT_FINE, dtype=np.int64)[:, None] * DFT_FINE
    i = np.arange(DFT_FINE, dtype=np.int64)[:, None]
    ce, se = _dftgen(a * 2 * lo, i * 2 * lo, l, tf, "dft_even")
    co, so = _dftgen(a * (2 * lo + 1), i * (2 * lo + 1), l, tf, "dft_odd")
    cot, sot = _dftgen(2 * a * lo, (2 * i + 1) * lo, l, tf, "dft_odd_t")
    return ce, se, co, so, cot, sot


def _kspec_kernel(n_fft, ce_ref, se_ref, co_ref, so_ref, kpe_ref, kpo_ref, kme_ref, kmo_ref,
                  krl_ref, krm_ref, ksl_ref, ksm_ref):
    tf = ce_ref.shape[0]
    f = pl.program_id(0) * tf + lax.broadcasted_iota(jnp.int32, (tf, 1), 0)
    w = jnp.where(f == 0, 1.0 / n_fft, 2.0 / n_fft)
    ce = _dot(ce_ref[...], kpe_ref[...])
    co = _dot(co_ref[...], kpo_ref[...])
    se = _dot(se_ref[...], kme_ref[...])
    so = _dot(so_ref[...], kmo_ref[...])
    krl_ref[...] = (ce + co) * w
    krm_ref[...] = (ce - co) * w
    ksl_ref[...] = (so + se) * w
    ksm_ref[...] = (so - se) * w


def _kspec(mats, kpe, kpo, kme, kmo, tf):
    half, c = kpe.shape
    const = lambda i: (0, 0)
    tile = lambda i: (i, 0)
    return pl.pallas_call(
        functools.partial(_kspec_kernel, 4 * half),
        grid=(half // tf,),
        in_specs=[pl.BlockSpec((tf, half), tile)] * 4 + [pl.BlockSpec((half, c), const)] * 4,
        out_specs=[pl.BlockSpec((tf, c), tile)] * 4,
        out_shape=[jax.ShapeDtypeStruct((half, c), F32)] * 4,
        compiler_params=_cparams("arbitrary"),
        name="kspec",
    )(*mats, kpe, kpo, kme, kmo)


def _hfwd_kernel(ce_ref, se_ref, co_ref, so_ref, ze_ref, zo_ref, krl_ref, krm_ref, ksl_ref, ksm_ref,
                 p1_ref, p2_ref, p3_ref, p4_ref):
    ze = ze_ref[...]
    zo = zo_ref[...]
    ce = _dot(ce_ref[...], ze)
    co = _dot(co_ref[...], zo)
    se = _dot(se_ref[...], ze)
    so = _dot(so_ref[...], zo)

    def product(zr, zs, kr_ref, ks_ref):
        kr = kr_ref[...]
        ks = ks_ref[...]
        return zr * kr - zs * ks, zr * ks + zs * kr

    al, bl = product(ce + co, so + se, krl_ref, ksl_ref)
    am, bm = product(ce - co, so - se, krm_ref, ksm_ref)
    p1_ref[...] = (al + am).astype(BF16)
    p2_ref[...] = (bl - bm).astype(BF16)
    p3_ref[...] = (al - am).astype(BF16)
    p4_ref[...] = (bl + bm).astype(BF16)


def _hfwd(mats, ze, zo, spec, tf):
    b, half, c = ze.shape
    tile = lambda i, bi: (i, 0)
    seq = lambda i, bi: (bi, 0, 0)
    out = lambda i, bi: (bi, i, 0)
    return pl.pallas_call(
        _hfwd_kernel,
        grid=(half // tf, b),
        in_specs=[pl.BlockSpec((tf, half), tile)] * 4 + [pl.BlockSpec((None, half, c), seq)] * 2
        + [pl.BlockSpec((tf, c), tile)] * 4,
        out_specs=[pl.BlockSpec((None, tf, c), out)] * 4,
        out_shape=[jax.ShapeDtypeStruct((b, half, c), BF16)] * 4,
        compiler_params=_cparams("arbitrary", "arbitrary"),
        name="hfwd",
    )(*mats, ze, zo, *spec)


def _hinv_kernel(ce_ref, se_ref, cot_ref, sot_ref, p1_ref, p2_ref, p3_ref, p4_ref, x0e_ref, x0o_ref,
                 ze_ref, zo_ref, qr_ref, qs_ref, krq_ref, ksq_ref, bias_ref, g_ref, o_ref, scr):
    tt, half = ce_ref.shape
    n_fft = 4 * half
    ye = _dot(ce_ref[...], p1_ref[...]) + _dot(se_ref[...], p2_ref[...])
    yo = _dot(cot_ref[...], p3_ref[...]) + _dot(sot_ref[...], p4_ref[...])
    sign = (1 - 2 * ((pl.program_id(0) * tt + lax.broadcasted_iota(jnp.int32, (tt, 1), 0)) % 2)).astype(F32)
    qr, qs, krq, ksq = qr_ref[...], qs_ref[...], krq_ref[...], ksq_ref[...]
    ye = ye + sign * ((qr * krq - qs * ksq) * (2.0 / n_fft))
    yo = yo + sign * ((qr * ksq + qs * krq) * (2.0 / n_fft))

    def finish(y, x0_ref, z_ref):
        hy = x0_ref[...] * (y + z_ref[...].astype(F32) * bias_ref[...])
        return hy * _rms(hy) * g_ref[...]

    he = finish(ye, x0e_ref, ze_ref)
    ho = finish(yo, x0o_ref, zo_ref)
    for c in range(scr.shape[0]):
        cs = slice(c * LANES, (c + 1) * LANES)
        scr[c, pl.ds(0, tt, stride=2), :] = he[:, cs]
        scr[c, pl.ds(1, tt, stride=2), :] = ho[:, cs]
        o_ref[:, cs] = scr[c].astype(BF16)


def _hinv(ce, se, cot, sot, ps, x0e, x0o, ze, zo, qr, qs, krq, ksq, bias, gain, tt):
    b, half, c = ze.shape
    tile = lambda i, bi: (i, 0)
    full = lambda i, bi: (bi, 0, 0)
    tok = lambda i, bi: (bi, i, 0)
    const = lambda i, bi: (0, 0)
    vec = pl.BlockSpec((1, c), const)
    return pl.pallas_call(
        _hinv_kernel,
        grid=(half // tt, b),
        in_specs=[pl.BlockSpec((tt, half), tile)] * 4 + [pl.BlockSpec((None, half, c), full)] * 4
        + [pl.BlockSpec((None, tt, c), tok)] * 4 + [pl.BlockSpec((None, 1, c), full)] * 2 + [vec] * 4,
        out_specs=pl.BlockSpec((None, 2 * tt, c), tok),
        out_shape=jax.ShapeDtypeStruct((b, 2 * half, c), BF16),
        scratch_shapes=[pltpu.VMEM((c // LANES, 2 * tt, LANES), F32)],
        compiler_params=_cparams("arbitrary", "arbitrary"),
        name="hinv",
    )(ce, se, cot, sot, *ps, x0e, x0o, ze, zo, qr, qs, krq, ksq, bias, gain)


def _mixout_kernel(an_ref, yn_ref, x_ref, mod_ref, wo_ref, n2_ref, wr_ref, br_ref, tri_ref,
                   xn_ref, h2_ref, route_ref, route_t_ref, cnt_ref, carry):
    bi = pl.program_id(0)
    d = x_ref.shape[-1]
    half = an_ref.shape[-1]
    tl = x_ref.shape[0]

    @pl.when((bi == 0) & (pl.program_id(1) == 0))
    def _():
        carry[...] = jnp.zeros_like(carry)

    mix = _dot(an_ref[...], wo_ref[0:half, :]) + _dot(yn_ref[...], wo_ref[half:, :])
    g1 = mod_ref[pl.ds(bi, 1), 2 * d:3 * d]
    sh2 = mod_ref[pl.ds(bi, 1), 3 * d:4 * d]
    sc2 = mod_ref[pl.ds(bi, 1), 4 * d:5 * d]
    xn = x_ref[...] + g1 * mix
    xn_ref[...] = xn
    h2 = (xn * _rms(xn)) * n2_ref[...] * (1 + sc2) + sh2
    for j in range(d // LANES):
        _token_chunk(h2_ref, 0, tl, j)[...] = h2[:, j * LANES:(j + 1) * LANES]

    logits = _dot3(h2, wr_ref[...]) + br_ref[...]
    lane = lax.broadcasted_iota(jnp.int32, (tl, LANES), 1).astype(F32)
    vals, idxs, sels = [], [], []
    cur = logits
    for _ in range(TOP_K):
        m = jnp.max(cur, axis=-1, keepdims=True)
        idx = jnp.min(jnp.where(cur == m, lane, float(LANES)), axis=-1, keepdims=True)
        sel = lane == idx
        vals.append(m)
        idxs.append(idx)
        sels.append(sel)
        cur = jnp.where(sel, -jnp.inf, cur)
    es = [jnp.exp(v - vals[0]) for v in vals]
    den = es[0] + es[1] + es[2] + es[3]
    hot = sum(s.astype(F32) for s in sels)
    before = _dot(tri_ref[...], hot.astype(BF16)) + carry[...]
    carry[...] = carry[...] + jnp.sum(hot, axis=0, keepdims=True)
    cnt_ref[...] = carry[...]
    route = jnp.zeros((tl, LANES), F32)
    for k in range(TOP_K):
        pos = jnp.sum(jnp.where(sels[k], before, 0.0), axis=-1, keepdims=True)
        route = jnp.where(lane == k, idxs[k], route)
        route = jnp.where(lane == TOP_K + k, es[k] / den, route)
        route = jnp.where(lane == 2 * TOP_K + k, pos, route)
    route_ref[...] = route
    route_t_ref[...] = route.T[0:route_t_ref.shape[0], :]


def _mixout(an, yn, x, mod, w_out, norm2, wr, br, tri, tl):
    b, l, d = x.shape
    half = an.shape[-1]
    nt = l // tl
    tok = lambda bi, i: (bi, i, 0)
    const = lambda bi, i: (0, 0)
    flat = lambda bi, i: (bi * nt + i, 0)
    return pl.pallas_call(
        _mixout_kernel,
        grid=(b, nt),
        in_specs=[pl.BlockSpec((None, tl, half), tok), pl.BlockSpec((None, tl, half), tok),
                  pl.BlockSpec((None, tl, d), tok),
                  pl.BlockSpec(mod.shape, const), pl.BlockSpec(w_out.shape, const),
                  pl.BlockSpec((1, d), const), pl.BlockSpec(wr.shape, const),
                  pl.BlockSpec((1, LANES), const), pl.BlockSpec((tl, tl), const)],
        out_specs=[pl.BlockSpec((None, tl, d), tok),
                   pl.BlockSpec((tl * SUBLANES, LANES), flat),
                   pl.BlockSpec((tl, LANES), flat),
                   pl.BlockSpec((2 * SUBLANES, tl), lambda bi, i: (0, bi * nt + i)),
                   pl.BlockSpec((1, LANES), const)],
        out_shape=[jax.ShapeDtypeStruct((b, l, d), F32),
                   jax.ShapeDtypeStruct((b * l * SUBLANES, LANES), F32),
                   jax.ShapeDtypeStruct((b * l, LANES), F32),
                   jax.ShapeDtypeStruct((2 * SUBLANES, b * l), F32),
                   jax.ShapeDtypeStruct((1, LANES), F32)],
        scratch_shapes=[pltpu.VMEM((1, LANES), F32)],
        compiler_params=_cparams("arbitrary", "arbitrary"),
        name="mixout",
    )(an, yn, x, mod, w_out, norm2, wr, br, tri)


def _token_chunk(ref, first_token, n, j):
    return ref.at[pl.ds(first_token * SUBLANES + j, n, stride=SUBLANES), :]


def _token_copy(idx_ref, r, src_hbm, dst, sem):
    first_row = lambda tok: tok * SUBLANES if isinstance(tok, int) else pl.multiple_of(tok * SUBLANES, SUBLANES)
    return pltpu.make_async_copy(src_hbm.at[pl.ds(first_row(idx_ref[0, r]), SUBLANES)],
                                 dst.at[pl.ds(first_row(r), SUBLANES)], sem)


def _gather_tokens(idx_ref, n, src_hbm, dst, sem):
    def body(r, carry):
        _token_copy(idx_ref, 2 * r, src_hbm, dst, sem).start(priority=0)
        _token_copy(idx_ref, 2 * r + 1, src_hbm, dst, sem).start(priority=1)
        return carry
    lax.fori_loop(0, n // 2, body, 0, unroll=16)


def _wait_tokens(src_hbm, dst, sem):
    pltpu.make_async_copy(src_hbm.at[pl.ds(0, dst.shape[0])], dst, sem).wait()


EXPERT_STAGES = 4
def _dispatch_kernel(last_ref, nvb_ref, dest_ref, h2_ref, xs_hbm, sbuf, zbuf, sem, zsem):
    i = pl.program_id(0)
    n = pl.num_programs(0)
    tc = h2_ref.shape[0] // SUBLANES
    slot = i % 2

    def wait_slot(s):
        for _ in range(TOP_K):
            pltpu.make_async_copy(sbuf.at[s], xs_hbm.at[pl.ds(0, tc * SUBLANES)], sem.at[s]).wait()

    @pl.when(i == 0)
    def _():
        zbuf[...] = jnp.zeros_like(zbuf)
        n_blk = xs_hbm.shape[0] // zbuf.shape[0]
        nvb = nvb_ref[0]
        targets = [(last_ref[e] >= 0, last_ref[e]) for e in range(N_EXPERTS)]
        targets += [(nvb + j < n_blk, nvb + j) for j in range(N_EXPERTS)]

        def zero_copy(blk):
            first = pl.multiple_of(blk * zbuf.shape[0], zbuf.shape[0])
            return pltpu.make_async_copy(zbuf, xs_hbm.at[pl.ds(first, zbuf.shape[0])], zsem)

        for op in ("start", "wait"):
            for needed, blk in targets:
                @pl.when(needed)
                def _():
                    getattr(zero_copy(blk), op)()

    @pl.when(i >= 2)
    def _():
        wait_slot(slot)

    sbuf[slot] = h2_ref[...]

    def body(r, carry):
        first = pl.multiple_of(r * SUBLANES, SUBLANES)
        for k in range(TOP_K):
            dst = pl.multiple_of(dest_ref[0, k * tc + r] * SUBLANES, SUBLANES)
            pltpu.make_async_copy(sbuf.at[slot, pl.ds(first, SUBLANES)], xs_hbm.at[pl.ds(dst, SUBLANES)],
                                  sem.at[slot]).start(priority=k % 2)
        return carry
    lax.fori_loop(0, tc, body, 0, unroll=8)

    @pl.when(i == n - 1)
    def _():
        wait_slot(slot)
        wait_slot(1 - slot)


def _dispatch(last_block, nvb, dest_kmaj, h2, n_blk, rows, tc):
    n = dest_kmaj.shape[0]
    assert n >= 2
    grid_spec = pltpu.PrefetchScalarGridSpec(
        num_scalar_prefetch=2,
        grid=(n,),
        in_specs=[pl.BlockSpec((None, 1, TOP_K * tc), lambda i, lb, nv: (i, 0, 0), memory_space=pltpu.SMEM),
                  pl.BlockSpec((tc * SUBLANES, LANES), lambda i, lb, nv: (i, 0))],
        out_specs=pl.BlockSpec(memory_space=pl.ANY),
        scratch_shapes=[pltpu.VMEM((2, tc * SUBLANES, LANES), F32), pltpu.VMEM((rows * SUBLANES, LANES), F32),
                        pltpu.SemaphoreType.DMA((2,)), pltpu.SemaphoreType.DMA(())],
    )
    return pl.pallas_call(
        _dispatch_kernel,
        grid_spec=grid_spec,
        out_shape=jax.ShapeDtypeStruct((n_blk * rows * SUBLANES, LANES), F32),
        compiler_params=_cparams("arbitrary"),
        name="dispatch",
    )(last_block, nvb, dest_kmaj, h2)


def _experts_kernel(be_ref, nvb_ref, xs_ref, wgu_ref, bgu_ref, wdn_ref, bdn_ref, ys_ref, xs_bf, wgu_bf, wdn_bf):
    i = pl.program_id(0)
    nvb = nvb_ref[0]
    rows = xs_bf.shape[0]
    nchunk = wgu_ref.shape[0] // LANES
    ff = wdn_ref.shape[0]
    fc = ff // EXPERT_STAGES

    @pl.when((i == 0) | (be_ref[i] != be_ref[jnp.maximum(i - 1, 0)]))
    def _():
        wgu_bf[...] = wgu_ref[...].astype(BF16)
        wdn_bf[...] = wdn_ref[...].astype(BF16)

    @pl.when(i < nvb)
    def _():
        for j in range(nchunk):
            xs_bf[:, j * LANES:(j + 1) * LANES] = _token_chunk(xs_ref, 0, rows, j)[...].astype(BF16)
        xs = xs_bf[...]
        def gate_up(c):
            cg = slice(c * fc, (c + 1) * fc)
            cu = slice(ff + c * fc, ff + (c + 1) * fc)
            return _dot(xs, wgu_bf[:, cg]) + bgu_ref[:, cg], _dot(xs, wgu_bf[:, cu]) + bgu_ref[:, cu]

        acc = None
        nxt = gate_up(0)
        for c in range(EXPERT_STAGES):
            gate, up = nxt
            if c + 1 < EXPERT_STAGES:
                nxt = gate_up(c + 1)
            gate = jnp.minimum(gate, SWIGLU_LIMIT)
            up = jnp.clip(up, -SWIGLU_LIMIT, SWIGLU_LIMIT)
            act = ((up + 1) * (gate * jax.nn.sigmoid(SWIGLU_ALPHA * gate))).astype(BF16)
            part = _dot(act, wdn_bf[c * fc:(c + 1) * fc, :])
            acc = part if acc is None else acc + part
        y = acc + bdn_ref[...]
        for j in range(nchunk):
            _token_chunk(ys_ref, 0, rows, j)[...] = y[:, j * LANES:(j + 1) * LANES]

    @pl.when(i >= nvb)
    def _():
        ys_ref[...] = jnp.zeros_like(ys_ref)


def _experts(block_e, nvb, xs, wgu, bgu, wdn, bdn, rows):
    n_blk = block_e.shape[0]
    d, ff2 = wgu.shape[1:]
    ff = wdn.shape[1]
    expert = lambda i, be, nv: (be[i], 0, 0)
    grid_spec = pltpu.PrefetchScalarGridSpec(
        num_scalar_prefetch=2,
        grid=(n_blk,),
        in_specs=[pl.BlockSpec((rows * SUBLANES, LANES), lambda i, be, nv: (jnp.minimum(i, nv[0] - 1), 0)),
                  pl.BlockSpec((None, d, ff2), expert), pl.BlockSpec((None, 1, ff2), expert),
                  pl.BlockSpec((None, ff, d), expert), pl.BlockSpec((None, 1, d), expert)],
        out_specs=pl.BlockSpec((rows * SUBLANES, LANES), lambda i, be, nv: (i, 0)),
        scratch_shapes=[pltpu.VMEM((rows, d), BF16), pltpu.VMEM((d, ff2), BF16), pltpu.VMEM((ff, d), BF16)],
    )
    return pl.pallas_call(
        _experts_kernel,
        grid_spec=grid_spec,
        out_shape=jax.ShapeDtypeStruct((n_blk * rows * SUBLANES, LANES), F32),
        compiler_params=_cparams("arbitrary"),
        name="experts",
    )(block_e, nvb, xs, wgu, bgu, wdn, bdn)


def _combine_kernel(dest_ref, destn_ref, ys_hbm, xn_ref, route_ref, g2_ref, o_ref, ybuf, sem):
    i = pl.program_id(0)
    n = pl.num_programs(0)
    tc = xn_ref.shape[0]
    slot = i % 2

    @pl.when(i == 0)
    def _():
        _gather_tokens(dest_ref, TOP_K * tc, ys_hbm, ybuf.at[0], sem.at[0])

    @pl.when(i + 1 < n)
    def _():
        _gather_tokens(destn_ref, TOP_K * tc, ys_hbm, ybuf.at[1 - slot], sem.at[1 - slot])

    _wait_tokens(ys_hbm, ybuf.at[slot], sem.at[slot])
    route = route_ref[...]
    gates = [jnp.broadcast_to(route[:, TOP_K + k:TOP_K + k + 1], (tc, LANES)) for k in range(TOP_K)]
    for j in range(xn_ref.shape[1] // LANES):
        cs = slice(j * LANES, (j + 1) * LANES)
        acc = gates[0] * _token_chunk(ybuf.at[slot], 0, tc, j)[...]
        for k in range(1, TOP_K):
            acc = acc + gates[k] * _token_chunk(ybuf.at[slot], k * tc, tc, j)[...]
        o_ref[:, cs] = xn_ref[:, cs] + g2_ref[:, cs] * acc


def _combine(dest_kmaj, ys, xn, route, g2, tc):
    t, d = xn.shape
    n = t // tc
    per_batch = t // g2.shape[0] // tc
    return pl.pallas_call(
        _combine_kernel,
        grid=(n,),
        in_specs=[pl.BlockSpec((None, 1, TOP_K * tc), lambda i: (i, 0, 0), memory_space=pltpu.SMEM),
                  pl.BlockSpec((None, 1, TOP_K * tc), lambda i: (jnp.minimum(i + 1, n - 1), 0, 0),
                               memory_space=pltpu.SMEM),
                  pl.BlockSpec(memory_space=pl.ANY),
                  pl.BlockSpec((tc, d), lambda i: (i, 0)),
                  pl.BlockSpec((tc, LANES), lambda i: (i, 0)),
                  pl.BlockSpec((None, 1, d), lambda i: (i // per_batch, 0, 0))],
        out_specs=pl.BlockSpec((tc, d), lambda i: (i, 0)),
        out_shape=jax.ShapeDtypeStruct((t, d), F32),
        scratch_shapes=[pltpu.VMEM((2, TOP_K * tc * SUBLANES, LANES), F32), pltpu.SemaphoreType.DMA((2,))],
        compiler_params=_cparams("arbitrary"),
        name="combine",
    )(dest_kmaj, dest_kmaj, ys, xn, route, g2)


def _rope_tables(l):
    n_freq = HEAD_DIM // 4
    inv_freq = ROPE_BASE ** (-np.arange(n_freq, dtype=np.float64) / n_freq)
    tpos = np.arange(l)
    lane = np.arange(LANES) % HEAD_DIM
    pos = np.where(lane[None, :] < HEAD_DIM // 2, (tpos // GRID_W)[:, None], (tpos % GRID_W)[:, None])
    ang = pos * inv_freq[lane % n_freq][None, :]
    sign = np.where(lane % (2 * n_freq) < n_freq, -1.0, 1.0)[None, :]
    return jnp.asarray(np.cos(ang), F32), jnp.asarray(np.sin(ang) * sign, F32)


def _filter_features(l, width):
    t = np.linspace(0.0, 1.0, l)[:, None]
    w = 2.0 * math.pi * np.arange(l)[:, None] / l
    bands = np.linspace(1e-4, POS_BANDS - 1, POS_BANDS)[None, :]
    z = np.concatenate([t, np.cos(bands * w), -np.sin(bands * w)], axis=-1)
    return np.pad(z, ((0, 0), (0, width - z.shape[1]))).astype(np.float32)


def kernel(x, c, ctx, c_ctx, w_mod, b_mod, norm1, norm2, w_in, q_norm, k_norm, sink, conv_w, conv_b,
           filt_w1, filt_b1, filt_w2, filt_b2, filt_w3, filt_b3, filt_w4, filt_freq, filt_bias,
           attn_out_norm, hyena_out_norm, w_out, w_router, b_router, w_gu, b_gu, w_down, b_down):
    assert w_mod.shape[0] == 1, "single-layer configuration"
    b, l, d = x.shape
    assert d == SUBLANES * LANES, "token-tiled rows assume one (8, 128) tile per token"
    t = b * l
    aw = N_Q_HEADS * HEAD_DIM
    kvw = N_KV_HEADS * HEAD_DIM
    hw = conv_w.shape[-1] // 3
    tl = min(512, l)

    ctx_row = b
    pad_rows = -(b + 1) % SUBLANES
    c_all = jnp.concatenate([c, c_ctx[None], jnp.zeros((pad_rows, d), F32)], axis=0)
    mod = _adaln(c_all, w_mod[0], b_mod[0])

    w = w_in[0]
    wq, wk, wv, wu = w[:, :aw], w[:, aw:aw + kvw], w[:, aw + kvw:aw + 2 * kvw], w[:, aw + 2 * kvw:]
    dup = lambda m: jnp.concatenate([m[:, h * HEAD_DIM:(h + 1) * HEAD_DIM]
                                     for h in range(N_KV_HEADS) for _ in range(2)], axis=1)
    w_all = jnp.concatenate([wq, dup(wk), dup(wv), wu], axis=1).astype(BF16)
    w_kv = jnp.concatenate([dup(wk), dup(wv)], axis=1).astype(BF16)
    gq = jnp.tile(q_norm[0], N_Q_HEADS)[None]
    gk = jnp.tile(k_norm[0], 2 * N_KV_HEADS)[None]
    bd = jnp.asarray(np.kron(np.eye(N_Q_HEADS), np.full((HEAD_DIM, HEAD_DIM), 1.0 / HEAD_DIM)), BF16)
    cos_t, sin_t = _rope_tables(l)

    q, k, v, u = _inproj(x, mod, norm1, w_all, gq, gk, bd, cos_t, sin_t, tl)
    kx, vx = _ctxkv(ctx, mod, ctx_row, norm1, w_kv, gk, bd[:2 * kvw, :2 * kvw])
    an = _attention(sink[0], q, k, v, kx, vx, attn_out_norm)

    x0e, x0o, ze, zo, qr, qs = _hconv(u, conv_w[0], conv_b)
    ffn = filt_w2.shape[-1]
    zf = _filter_features(l, ffn)
    w1 = jnp.pad(filt_w1[0], ((0, ffn - POS_EMB_DIM), (0, 0)))
    deltas = jnp.asarray(np.linspace(MIN_DECAY, MAX_DECAY, hw)[None, :], F32)
    kpe, kpo, kme, kmo, krq, ksq = _filter(jnp.asarray(zf[0::2]), jnp.asarray(zf[1::2]), w1, filt_b1, filt_w2[0],
                                           filt_b2, filt_w3[0], filt_b3, filt_freq, filt_w4[0], deltas)
    tf = min(512, l // 2)
    ce, se, co, so, cot, sot = _dft_matrices(l, tf)
    spec = _kspec((ce, se, co, so), kpe, kpo, kme, kmo, tf)
    ps = _hfwd((ce, se, co, so), ze, zo, spec, tf)
    yn = _hinv(ce, se, cot, sot, ps, x0e, x0o, ze, zo, qr, qs, krq, ksq, filt_bias, hyena_out_norm,
               min(256, l // 2))

    wr = jnp.pad(w_router[0], ((0, 0), (0, LANES - N_EXPERTS)))
    br = jnp.concatenate([b_router[0], jnp.full((LANES - N_EXPERTS,), NEG_INF, F32)])[None]
    tri = jnp.asarray(np.tril(np.ones((tl, tl)), -1), BF16)
    xn, h2, route, route_t, cnt = _mixout(an, yn, x, mod, w_out[0].astype(BF16), norm2, wr, br, tri, tl)

    rows = EXPERT_ROWS
    a_tot = t * TOP_K
    n_blk = -(-a_tot // rows) + N_EXPERTS
    idx = route_t[0:TOP_K].astype(jnp.int32)
    pos = route_t[2 * TOP_K:3 * TOP_K].astype(jnp.int32)
    counts = cnt[0, :N_EXPERTS].astype(jnp.int32)
    pcounts = (counts + rows - 1) // rows * rows
    pends = jnp.cumsum(pcounts)
    pstarts = pends - pcounts
    experts = jnp.arange(N_EXPERTS, dtype=jnp.int32)[:, None, None]
    dest = jnp.sum(jnp.where(idx[None] == experts, pstarts[:, None, None], 0), axis=0) + pos
    block_start = jnp.arange(n_blk, dtype=jnp.int32) * rows
    block_e = jnp.minimum(jnp.sum(pends[None, :] <= block_start[:, None], axis=1), N_EXPERTS - 1).astype(jnp.int32)
    last_block = jnp.where(pcounts > 0, pends // rows - 1, -1).astype(jnp.int32)
    nvb = (pends[-1:] // rows).astype(jnp.int32)
    tc = min(COMBINE_ROWS, l)
    dest_kmaj = dest.reshape(TOP_K, t // tc, tc).transpose(1, 0, 2).reshape(t // tc, 1, TOP_K * tc)

    xs = _dispatch(last_block, nvb, dest_kmaj, h2, n_blk, rows, tc)
    ys = _experts(block_e, nvb, xs, w_gu[0], b_gu[0][:, None, :], w_down[0], b_down[0][:, None, :], rows)

    g2 = mod[:b, None, 5 * d:6 * d]
    out = _combine(dest_kmaj, ys, xn.reshape(t, d), route, g2, tc)
    return out.reshape(b, l, d)
```
